```python
import math
import jax, jax.numpy as jnp
from jax import lax
import numpy as np

D_MODEL = 1024
BATCH = 2
SEQ = 16384
DEPTH = 4

N_MEM = 256
GLA_HEADS = 4
GLA_DK = D_MODEL // 16
GLA_DV = D_MODEL // 8
GLA_RANK = 16
GLA_GATE_NORM = 16.0
GLA_CHUNK = 64
SB_HEADS = 4
SB_DH = D_MODEL // 16
SB_BLOCK = 128
SB_GROUPS = 8
MEM_HEADS = 4
MEM_DH = D_MODEL // 16
GLA_QK_W = GLA_HEADS * GLA_DK
GLA_V_W = GLA_HEADS * GLA_DV
SB_W = SB_HEADS * SB_DH
MEM_W = MEM_HEADS * MEM_DH
D_MIX = GLA_V_W + SB_W + MEM_W
SPLIT_NAMES = ("gla_q", "gla_k", "gla_v", "gla_gate", "gla_alpha",
               "sb_q", "sb_k", "sb_v", "sb_gate", "mem_q", "mem_gate")
SPLIT_SIZES = (GLA_QK_W, GLA_QK_W, GLA_V_W, GLA_V_W, GLA_RANK,
               SB_W, SB_W, SB_W, SB_W, MEM_W, MEM_W)
D_IN = 3088
ALPHA = (2.0 * DEPTH) ** 0.25
BETA = (8.0 * DEPTH) ** -0.25
LN_EPS = 1e-5
RMS_EPS = 1e-6

kernel_name = "gla_stickbreaking_memxattn_deepnorm_hybrid"


def _split_cols(proj):
    parts, off = [], 0
    for n in SPLIT_SIZES:
        parts.append(proj[..., off:off + n])
        off += n
    return parts


def _layer_norm(x, g, b):
    xf = x.astype(jnp.float32)
    mu = jnp.mean(xf, axis=-1, keepdims=True)
    var = jnp.mean(jnp.square(xf - mu), axis=-1, keepdims=True)
    return ((xf - mu) * lax.rsqrt(var + LN_EPS) * g + b).astype(x.dtype)


def _gla(q, k, v, log_a):
    B, T, H, dk = q.shape
    dv = v.shape[-1]
    n = T // GLA_CHUNK

    def to_chunks(t):
        return t.astype(jnp.float32).reshape(B, n, GLA_CHUNK, H, t.shape[-1]).transpose(1, 0, 3, 2, 4)

    qc = to_chunks(q) * (dk ** -0.5)
    kc, vc, gc = to_chunks(k), to_chunks(v), to_chunks(log_a)
    causal = jnp.asarray(np.tril(np.ones((GLA_CHUNK, GLA_CHUNK), dtype=bool)))

    def step(S, inp):
        qb, kb, vb, gb = inp
        G = jnp.cumsum(gb, axis=2)
        o_inter = jnp.einsum('bhck,bhkv->bhcv', qb * jnp.exp(G), S)
        diff = G[:, :, :, None, :] - G[:, :, None, :, :]
        decay = jnp.exp(jnp.where(causal[None, None, :, :, None], diff, -jnp.inf))
        A = jnp.einsum('bhik,bhjk,bhijk->bhij', qb, kb, decay)
        o_intra = jnp.einsum('bhij,bhjv->bhiv', A, vb)
        G_last = G[:, :, -1]
        k_dec = kb * jnp.exp(G_last[:, :, None, :] - G)
        S = jnp.exp(G_last)[..., None] * S + jnp.einsum('bhck,bhcv->bhkv', k_dec, vb)
        return S, o_inter + o_intra

    S0 = jnp.zeros((B, H, dk, dv), jnp.float32)
    _, o = lax.scan(step, S0, (qc, kc, vc, gc))
    return o.transpose(1, 0, 3, 2, 4).reshape(B, T, H, dv)


def _stick_breaking(q, k, v):
    B, T, H, d = q.shape
    nq = T // SB_BLOCK
    n_groups = math.gcd(nq, SB_GROUPS)
    gs = nq // n_groups
    qf = (q.astype(jnp.float32) * (d ** -0.5)).transpose(0, 2, 1, 3)
    kt = k.astype(jnp.float32).transpose(0, 2, 1, 3)
    vt = v.astype(jnp.float32).transpose(0, 2, 1, 3)
    tri_in = jnp.asarray(np.tril(np.ones((SB_BLOCK, SB_BLOCK), np.float32), -1))
    outs = []
    for g in range(n_groups):
        nb = (g + 1) * gs
        L = nb * SB_BLOCK
        k_g, v_g = kt[:, :, :L], vt[:, :, :L]
        tri_blk = jnp.asarray(np.tril(np.ones((nb, nb), np.float32), -1))
        key_pos = jnp.arange(L)
        q_g = qf[:, :, g * gs * SB_BLOCK:(g + 1) * gs * SB_BLOCK].reshape(
            B, H, gs, SB_BLOCK, d).transpose(2, 0, 1, 3, 4)
        starts = (g * gs + jnp.arange(gs, dtype=jnp.int32)) * SB_BLOCK

        def block(args, k_g=k_g, v_g=v_g, tri_blk=tri_blk, key_pos=key_pos, nb=nb, L=L):
            qblk, start = args
            z = jnp.einsum('bhqd,bhkd->bhqk', qblk, k_g)
            q_pos = start + jnp.arange(SB_BLOCK)
            mask = key_pos[None, :] < q_pos[:, None]
            lsn = jax.nn.log_sigmoid(-z)
            lf = jnp.where(mask, lsn, 0.0).reshape(B, H, SB_BLOCK, nb, SB_BLOCK)
            within = jnp.einsum('bhqnk,kj->bhqnj', lf, tri_in)
            across = jnp.einsum('bhqn,nm->bhqm', jnp.sum(lf, axis=-1), tri_blk)
            rest = (within + across[..., None]).reshape(B, H, SB_BLOCK, L)
            w = jnp.exp(jnp.where(mask, z + lsn + rest, -jnp.inf))
            return jnp.einsum('bhqk,bhkd->bhqd', w, v_g)

        o = lax.map(block, (q_g, starts))
        outs.append(o.transpose(1, 2, 0, 3, 4).reshape(B, H, gs * SB_BLOCK, d))
    o = jnp.concatenate(outs, axis=2)
    return o.transpose(0, 2, 1, 3).reshape(B, T, H * d)


def _mem_attn(q, mem, w_mkv):
    B, T, _ = q.shape
    kv = mem @ w_mkv
    km = kv[..., :MEM_W].reshape(B, N_MEM, MEM_HEADS, MEM_DH)
    vm = kv[..., MEM_W:].reshape(B, N_MEM, MEM_HEADS, MEM_DH)
    qh = q.reshape(B, T, MEM_HEADS, MEM_DH)
    s = jnp.einsum('bthd,bmhd->bhtm', qh, km).astype(jnp.float32) * (MEM_DH ** -0.5)
    p = jax.nn.softmax(s, axis=-1)
    o = jnp.einsum('bhtm,bmhd->bthd', p, vm.astype(jnp.float32))
    return o.reshape(B, T, MEM_W)


def _layer(x, mem, w_in, w_a2, b_a, gla_nw, w_mkv, w_out, ln_g, ln_b):
    B, T, _ = x.shape
    proj = x @ w_in
    gq, gk, gv, gg, ga, sq, sk, sv, sg, mq, mg = _split_cols(proj)
    log_a = jax.nn.log_sigmoid((ga @ w_a2 + b_a).astype(jnp.float32)) / GLA_GATE_NORM
    o_gla = _gla(gq.reshape(B, T, GLA_HEADS, GLA_DK), gk.reshape(B, T, GLA_HEADS, GLA_DK),
                 gv.reshape(B, T, GLA_HEADS, GLA_DV), log_a.reshape(B, T, GLA_HEADS, GLA_DK))
    o_gla = o_gla * lax.rsqrt(jnp.mean(jnp.square(o_gla), axis=-1, keepdims=True) + RMS_EPS) * gla_nw
    o_gla = o_gla.reshape(B, T, GLA_V_W)
    o_sb = _stick_breaking(sq.reshape(B, T, SB_HEADS, SB_DH), sk.reshape(B, T, SB_HEADS, SB_DH),
                           sv.reshape(B, T, SB_HEADS, SB_DH))
    o_mem = _mem_attn(mq, mem, w_mkv)
    mix = jnp.concatenate([o_gla * jax.nn.silu(gg.astype(jnp.float32)),
                           o_sb * jax.nn.silu(sg.astype(jnp.float32)),
                           o_mem * jax.nn.silu(mg.astype(jnp.float32))], axis=-1).astype(x.dtype)
    y = mix @ w_out
    return _layer_norm(ALPHA * x + y, ln_g, ln_b)


def setup_inputs(seed: int = 0) -> dict:
    key = jax.random.key(seed)
    ks = jax.random.split(key, 10)
    x = jax.random.normal(ks[0], (BATCH, SEQ, D_MODEL), jnp.float32)
    mem = jax.random.normal(ks[1], (BATCH, N_MEM, D_MODEL), jnp.float32)
    col_scale = np.full((D_IN,), D_MODEL ** -0.5, np.float32)
    off = 0
    for name, n in zip(SPLIT_NAMES, SPLIT_SIZES):
        if name in ("gla_v", "sb_v"):
            col_scale[off:off + n] *= BETA
        off += n
    w_in = jax.random.normal(ks[2], (DEPTH, D_MODEL, D_IN), jnp.float32) * jnp.asarray(col_scale)
    w_alpha2 = jax.random.normal(ks[3], (DEPTH, GLA_RANK, GLA_QK_W), jnp.float32) * GLA_RANK ** -0.5
    b_alpha = 0.1 * jax.random.normal(ks[4], (DEPTH, GLA_QK_W), jnp.float32)
    gla_norm_w = 1.0 + 0.02 * jax.random.normal(ks[5], (DEPTH, GLA_DV), jnp.float32)
    mkv_scale = np.concatenate([np.full((MEM_W,), D_MODEL ** -0.5, np.float32),
                                np.full((MEM_W,), D_MODEL ** -0.5 * BETA, np.float32)])
    w_mem_kv = jax.random.normal(ks[6], (DEPTH, D_MODEL, 2 * MEM_W), jnp.float32) * jnp.asarray(mkv_scale)
    w_out = jax.random.normal(ks[7], (DEPTH, D_MIX, D_MODEL), jnp.float32) * (D_MIX ** -0.5 * BETA)
    ln_g = 1.0 + 0.02 * jax.random.normal(ks[8], (DEPTH, D_MODEL), jnp.float32)
    ln_b = 0.02 * jax.random.normal(ks[9], (DEPTH, D_MODEL), jnp.float32)
    return {"x": x, "mem": mem, "w_in": w_in, "w_alpha2": w_alpha2, "b_alpha": b_alpha,
            "gla_norm_w": gla_norm_w, "w_mem_kv": w_mem_kv, "w_out": w_out,
            "ln_g": ln_g, "ln_b": ln_b}


def reference(x, mem, w_in, w_alpha2, b_alpha, gla_norm_w, w_mem_kv, w_out, ln_g, ln_b):
    for l in range(DEPTH):
        x = _layer(x, mem, w_in[l], w_alpha2[l], b_alpha[l], gla_norm_w[l], w_mem_kv[l],
                   w_out[l], ln_g[l], ln_b[l])
    return x
```

```python
import functools

import jax
import jax.numpy as jnp
from jax import lax
from jax.experimental import pallas as pl
from jax.experimental.pallas import tpu as pltpu

F32 = jnp.float32
BF16 = jnp.bfloat16

D_MODEL = 1024
DEPTH = 4
N_MEM = 256
GLA_HEADS = 4
GLA_DK = 64
GLA_DV = 128
GLA_RANK = 16
GLA_GATE_NORM = 16.0
GLA_CHUNK = 64
SB_HEADS = 4
SB_DH = 64
SB_BLOCK = 128
MEM_HEADS = 4
MEM_DH = 64
GLA_QK_W = GLA_HEADS * GLA_DK
GLA_V_W = GLA_HEADS * GLA_DV
SB_W = SB_HEADS * SB_DH
MEM_W = MEM_HEADS * MEM_DH
ALPHA = (2.0 * DEPTH) ** 0.25
LN_EPS = 1e-5
RMS_EPS = 1e-6

LANES = 128
GLA_COLS = 2 * GLA_QK_W + 2 * GLA_V_W
SB_COLS = 3 * SB_W
C3_COLS = SB_W + 2 * MEM_W
GA_COLS = LANES
PROJ_COLS = GLA_COLS + SB_COLS + C3_COLS + GA_COLS
VMEM_LIMIT = 56 * 1024 * 1024
SB_SKIP_LOG = -106.0


def _dot(a, b):
    return jnp.dot(a, b, preferred_element_type=F32)


def _dot_nt(a, b):
    return lax.dot_general(a, b, (((1,), (1,)), ((), ())), preferred_element_type=F32)


def _dot_tn(a, b):
    return lax.dot_general(a, b, (((0,), (0,)), ((), ())), preferred_element_type=F32)


def _split_bf16(x, n):
    parts = []
    for _ in range(n - 1):
        h = x.astype(BF16)
        parts.append(h)
        x = x - h.astype(F32)
    parts.append(x.astype(BF16))
    return parts


def _log_sigmoid(z):
    return jnp.minimum(z, 0.0) - jnp.log(1.0 + jnp.exp(-jnp.abs(z)))


def _silu(g):
    return g / (1.0 + jnp.exp(-g))


def _proj_kernel(x_ref, w_ref, gla_ref, sb_ref, c3_ref, ga_ref):
    xb = x_ref[...].astype(BF16)
    off = 0
    for ref, width in ((gla_ref, GLA_COLS), (sb_ref, SB_COLS), (c3_ref, C3_COLS), (ga_ref, GA_COLS)):
        step = 512 if width % 512 == 0 else width if width < 512 else 256
        for c in range(0, width, step):
            ref[:, c:c + step] = _dot(xb, w_ref[:, off + c:off + c + step]).astype(BF16)
        off += width


def _proj(x2, w, tm=512):
    n = x2.shape[0]
    outs = (GLA_COLS, SB_COLS, C3_COLS, GA_COLS)
    return pl.pallas_call(
        _proj_kernel,
        grid=(n // tm,),
        in_specs=[pl.BlockSpec((tm, D_MODEL), lambda i: (i, 0)),
                  pl.BlockSpec((D_MODEL, PROJ_COLS), lambda i: (0, 0))],
        out_specs=[pl.BlockSpec((tm, c), lambda i: (i, 0)) for c in outs],
        out_shape=[jax.ShapeDtypeStruct((n, c), BF16) for c in outs],
        compiler_params=pltpu.CompilerParams(dimension_semantics=("arbitrary",),
                                             vmem_limit_bytes=VMEM_LIMIT),
        name="proj",
    )(x2, w)


def _memkv_kernel(m_ref, w_ref, o_ref):
    o_ref[0] = _dot(m_ref[...].astype(BF16), w_ref[0]).astype(BF16)


def _memkv(mem2, w_mkv):
    n = mem2.shape[0]
    return pl.pallas_call(
        _memkv_kernel,
        grid=(DEPTH,),
        in_specs=[pl.BlockSpec((n, D_MODEL), lambda l: (0, 0)),
                  pl.BlockSpec((1, D_MODEL, 2 * MEM_W), lambda l: (l, 0, 0))],
        out_specs=pl.BlockSpec((1, n, 2 * MEM_W), lambda l: (l, 0, 0)),
        out_shape=jax.ShapeDtypeStruct((DEPTH, n, 2 * MEM_W), BF16),
        compiler_params=pltpu.CompilerParams(dimension_semantics=("arbitrary",),
                                             vmem_limit_bytes=VMEM_LIMIT),
        name="memkv",
    )(mem2, w_mkv)


def _gla_kernel(gla_ref, ga_ref, wa2_ref, ba_ref, nw_ref, out_ref, st_ref, *, tt):
    c_sz = GLA_CHUNK

    @pl.when(pl.program_id(1) == 0)
    def _():
        st_ref[...] = jnp.zeros_like(st_ref)

    row = lax.broadcasted_iota(jnp.int32, (c_sz, c_sz), 0)
    col = lax.broadcasted_iota(jnp.int32, (c_sz, c_sz), 1)
    causal = col <= row
    tri_incl = jnp.where(causal, 1.0, 0.0).astype(BF16)
    low_c = lax.broadcasted_iota(jnp.int32, (c_sz, LANES), 1) < GLA_DK
    low_v = lax.broadcasted_iota(jnp.int32, (GLA_DV, LANES), 1) < GLA_DK

    def chunk(c, carry):
        rows = pl.ds(pl.multiple_of(c * c_sz, c_sz), c_sz)
        zz = _dot(ga_ref[0, rows, :], wa2_ref[...]) + ba_ref[...]
        log_a = _log_sigmoid(zz) * (1.0 / GLA_GATE_NORM)
        g_cum = sum(_dot(tri_incl, part) for part in _split_bf16(log_a, 3))
        g_last = g_cum[c_sz - 1:c_sz, :]
        q = gla_ref[0, rows, 0:GLA_QK_W].astype(F32)
        k = gla_ref[0, rows, GLA_QK_W:2 * GLA_QK_W].astype(F32)
        qg = (q * jnp.exp(g_cum)).astype(BF16)
        kg = (k * jnp.exp(-g_cum)).astype(BF16)
        kd = (k * jnp.exp(g_last - g_cum)).astype(BF16)
        decay = jnp.exp(g_last)
        for p in range(GLA_HEADS // 2):
            sl = slice(p * LANES, (p + 1) * LANES)
            qgp = qg[:, sl]
            q2 = jnp.concatenate([jnp.where(low_c, qgp, jnp.zeros_like(qgp)),
                                  jnp.where(low_c, jnp.zeros_like(qgp), qgp)], axis=0)
            st = st_ref[p]
            a2 = _dot_nt(q2, kg[:, sl])
            inter2 = _dot_nt(q2, st.astype(BF16))
            upd = []
            for hh in range(2):
                h = 2 * p + hh
                v_h = gla_ref[0, rows, 2 * GLA_QK_W + h * GLA_DV:2 * GLA_QK_W + (h + 1) * GLA_DV]
                a = jnp.where(causal, a2[hh * c_sz:(hh + 1) * c_sz], 0.0).astype(BF16)
                o = _dot(a, v_h) + inter2[hh * c_sz:(hh + 1) * c_sz]
                ms = jnp.mean(o * o, axis=-1, keepdims=True)
                gate = gla_ref[0, rows, 2 * GLA_QK_W + GLA_V_W + h * GLA_DV:
                               2 * GLA_QK_W + GLA_V_W + (h + 1) * GLA_DV].astype(F32)
                res = o * lax.rsqrt(ms + RMS_EPS) * nw_ref[...] * _silu(gate)
                out_ref[0, rows, h * GLA_DV:(h + 1) * GLA_DV] = res.astype(BF16)
                upd.append(_dot_tn(v_h, kd[:, sl]))
            st_ref[p] = st * decay[:, sl] + jnp.where(low_v, upd[0], upd[1])
        return carry

    lax.fori_loop(0, tt // c_sz, chunk, 0)


def _gla(gla, ga, wa2, ba, nw, tt=512):
    b, t, _ = gla.shape
    return pl.pallas_call(
        functools.partial(_gla_kernel, tt=tt),
        grid=(b, t // tt),
        in_specs=[pl.BlockSpec((1, tt, GLA_COLS), lambda i, j: (i, j, 0)),
                  pl.BlockSpec((1, tt, GA_COLS), lambda i, j: (i, j, 0)),
                  pl.BlockSpec((GA_COLS, GLA_QK_W), lambda i, j: (0, 0)),
                  pl.BlockSpec((1, GLA_QK_W), lambda i, j: (0, 0)),
                  pl.BlockSpec((1, GLA_DV), lambda i, j: (0, 0))],
        out_specs=pl.BlockSpec((1, tt, GLA_V_W), lambda i, j: (i, j, 0)),
        out_shape=jax.ShapeDtypeStruct((b, t, GLA_V_W), BF16),
        scratch_shapes=[pltpu.VMEM((GLA_HEADS // 2, GLA_DV, LANES), F32)],
        compiler_params=pltpu.CompilerParams(dimension_semantics=("arbitrary", "arbitrary"),
                                             vmem_limit_bytes=VMEM_LIMIT),
        name="gla",
    )(gla, ga, wa2, ba, nw)


def _sb_kernel(q_ref, k_ref, v_ref, g_ref, o_ref, oacc_ref, lacc_ref):
    qb = SB_BLOCK
    i = pl.program_id(2)
    row = lax.broadcasted_iota(jnp.int32, (qb, qb), 0)
    lane = lax.broadcasted_iota(jnp.int32, (qb, qb), 1)
    low = lane < SB_DH
    strict = lane < row
    r2 = lax.broadcasted_iota(jnp.int32, (qb, 2 * qb), 0)
    c2 = lax.broadcasted_iota(jnp.int32, (qb, 2 * qb), 1)
    tri_ones = jnp.where((r2 > c2) | (c2 >= qb), 1.0, 0.0).astype(BF16)

    q = q_ref[0]
    qm = (jnp.where(low, q, jnp.zeros_like(q)), jnp.where(low, jnp.zeros_like(q), q))
    oacc_ref[...] = jnp.zeros_like(oacc_ref)
    lacc_ref[...] = jnp.zeros_like(lacc_ref)

    def block(j, diagonal):
        rows = pl.ds(pl.multiple_of(j * qb, qb), qb)
        kb = k_ref[0, rows, :]
        vb = v_ref[0, rows, :]
        for h in range(2):
            z = _dot_nt(qm[h], kb)
            lsn = _log_sigmoid(-z)
            lf = jnp.where(strict, lsn, 0.0) if diagonal else lsn
            sums = sum(_dot(part, tri_ones) for part in _split_bf16(lf, 2))
            t = z + lsn + sums[:, :qb] + lacc_ref[h]
            w = jnp.exp(t)
            if diagonal:
                w = jnp.where(strict, w, 0.0)
            oacc_ref[h] += _dot(w.astype(BF16), vb)
            lacc_ref[h] += sums[:, qb:]

    def live():
        return jnp.maximum(jnp.max(lacc_ref[0]), jnp.max(lacc_ref[1]))

    block(i, True)

    def cond(c):
        return jnp.logical_and(c[0] >= 0, c[1] > SB_SKIP_LOG)

    def body(c):
        block(c[0], False)
        return c[0] - 1, live()

    lax.while_loop(cond, body, (i - 1, live()))
    o = jnp.where(low, oacc_ref[0], oacc_ref[1])
    o_ref[0] = (o * _silu(g_ref[0].astype(F32))).astype(BF16)


def _sb(sb, c3):
    b, t, _ = sb.shape
    nq = t // SB_BLOCK
    pairs = SB_HEADS // 2
    return pl.pallas_call(
        _sb_kernel,
        grid=(b, pairs, nq),
        in_specs=[pl.BlockSpec((1, SB_BLOCK, LANES), lambda bi, p, i: (bi, i, p)),
                  pl.BlockSpec((1, t, LANES), lambda bi, p, i: (bi, 0, pairs + p)),
                  pl.BlockSpec((1, t, LANES), lambda bi, p, i: (bi, 0, 2 * pairs + p)),
                  pl.BlockSpec((1, SB_BLOCK, LANES), lambda bi, p, i: (bi, i, p))],
        out_specs=pl.BlockSpec((1, SB_BLOCK, LANES), lambda bi, p, i: (bi, i, p)),
        out_shape=jax.ShapeDtypeStruct((b, t, SB_W), BF16),
        scratch_shapes=[pltpu.VMEM((2, SB_BLOCK, LANES), F32),
                        pltpu.VMEM((2, SB_BLOCK, LANES), F32)],
        compiler_params=pltpu.CompilerParams(dimension_semantics=("arbitrary", "arbitrary", "arbitrary"),
                                             vmem_limit_bytes=VMEM_LIMIT),
        name="sb",
    )(sb, sb, sb, c3)


def _out_kernel(x_ref, mixg_ref, mixs_ref, c3_ref, kvm_ref, wout_ref, lng_ref, lnb_ref, o_ref, *, tm):
    low = lax.broadcasted_iota(jnp.int32, (tm, LANES), 1) < MEM_DH
    y = _dot(mixg_ref[0], wout_ref[0:GLA_V_W, :])
    y += _dot(mixs_ref[0], wout_ref[GLA_V_W:GLA_V_W + SB_W, :])
    for p in range(MEM_HEADS // 2):
        qp = c3_ref[0, :, SB_W + p * LANES:SB_W + (p + 1) * LANES]
        km = kvm_ref[0, 0, :, p * LANES:(p + 1) * LANES]
        vm = kvm_ref[0, 0, :, MEM_W + p * LANES:MEM_W + (p + 1) * LANES]
        heads = []
        for hh in range(2):
            qm = jnp.where(low, qp, jnp.zeros_like(qp)) if hh == 0 else jnp.where(low, jnp.zeros_like(qp), qp)
            s = _dot_nt(qm, km)
            e = jnp.exp(s - jnp.max(s, axis=-1, keepdims=True))
            den = jnp.sum(e, axis=-1, keepdims=True)
            heads.append(_dot(e.astype(BF16), vm) / den)
        om = jnp.where(low, heads[0], heads[1])
        gate = c3_ref[0, :, SB_W + MEM_W + p * LANES:SB_W + MEM_W + (p + 1) * LANES].astype(F32)
        mixm = (om * _silu(gate)).astype(BF16)
        base = GLA_V_W + SB_W + p * LANES
        y += _dot(mixm, wout_ref[base:base + LANES, :])
    r = ALPHA * x_ref[0] + y
    mu = jnp.mean(r, axis=-1, keepdims=True)
    d = r - mu
    var = jnp.mean(d * d, axis=-1, keepdims=True)
    o_ref[0] = d * lax.rsqrt(var + LN_EPS) * lng_ref[...] + lnb_ref[...]


def _out(x, mixg, mixs, c3, kvm, layer, wout, lng, lnb, tm=256):
    b, t, _ = x.shape
    return pl.pallas_call(
        functools.partial(_out_kernel, tm=tm),
        grid=(b, t // tm),
        in_specs=[pl.BlockSpec((1, tm, D_MODEL), lambda i, j: (i, j, 0)),
                  pl.BlockSpec((1, tm, GLA_V_W), lambda i, j: (i, j, 0)),
                  pl.BlockSpec((1, tm, SB_W), lambda i, j: (i, j, 0)),
                  pl.BlockSpec((1, tm, C3_COLS), lambda i, j: (i, j, 0)),
                  pl.BlockSpec((1, 1, N_MEM, 2 * MEM_W), lambda i, j: (layer, i, 0, 0)),
                  pl.BlockSpec((D_MODEL, D_MODEL), lambda i, j: (0, 0)),
                  pl.BlockSpec((1, D_MODEL), lambda i, j: (0, 0)),
                  pl.BlockSpec((1, D_MODEL), lambda i, j: (0, 0))],
        out_specs=pl.BlockSpec((1, tm, D_MODEL), lambda i, j: (i, j, 0)),
        out_shape=jax.ShapeDtypeStruct((b, t, D_MODEL), F32),
        compiler_params=pltpu.CompilerParams(dimension_semantics=("arbitrary", "arbitrary"),
                                             vmem_limit_bytes=VMEM_LIMIT),
        name="out",
    )(x, mixg, mixs, c3, kvm, wout, lng, lnb)


def _regroup_w_in(w):
    o = 0
    seg = {}
    for name, n in (("gq", GLA_QK_W), ("gk", GLA_QK_W), ("gv", GLA_V_W), ("gg", GLA_V_W), ("ga", GLA_RANK),
                    ("sq", SB_W), ("sk", SB_W), ("sv", SB_W), ("sg", SB_W), ("mq", MEM_W), ("mg", MEM_W)):
        seg[name] = w[:, o:o + n]
        o += n
    pad = jnp.zeros((w.shape[0], GA_COLS - GLA_RANK), w.dtype)
    cols = [seg["gq"] * GLA_DK ** -0.5, seg["gk"], seg["gv"], seg["gg"],
            seg["sq"] * SB_DH ** -0.5, seg["sk"], seg["sv"],
            seg["sg"], seg["mq"] * MEM_DH ** -0.5, seg["mg"],
            seg["ga"], pad]
    return jnp.concatenate(cols, axis=1).astype(BF16)


def kernel(x, mem, w_in, w_alpha2, b_alpha, gla_norm_w, w_mem_kv, w_out, ln_g, ln_b):
    b, t, d = x.shape
    kvm = _memkv(mem.reshape(b * N_MEM, d), w_mem_kv.astype(BF16)).reshape(DEPTH, b, N_MEM, 2 * MEM_W)
    for l in range(DEPTH):
        gla, sb, c3, ga = _proj(x.reshape(b * t, d), _regroup_w_in(w_in[l]))
        wa2 = jnp.zeros((GA_COLS, GLA_QK_W), F32).at[:GLA_RANK].set(w_alpha2[l]).astype(BF16)
        mixg = _gla(gla.reshape(b, t, GLA_COLS), ga.reshape(b, t, GA_COLS), wa2,
                    b_alpha[l].reshape(1, GLA_QK_W), gla_norm_w[l].reshape(1, GLA_DV))
        c3 = c3.reshape(b, t, C3_COLS)
        mixs = _sb(sb.reshape(b, t, SB_COLS), c3)
        x = _out(x, mixg, mixs, c3, kvm, l, w_out[l].astype(BF16),
                 ln_g[l].reshape(1, d), ln_b[l].reshape(1, d))
    return x
```

```python
import functools

import jax
import jax.numpy as jnp
from jax import lax
from jax.experimental import pallas as pl
from jax.experimental.pallas import tpu as pltpu

F32 = jnp.float32
BF16 = jnp.bfloat16

D_MODEL = 1024
DEPTH = 4
N_MEM = 256
GLA_HEADS = 4
GLA_DK = 64
GLA_DV = 128
GLA_RANK = 16
GLA_GATE_NORM = 16.0
GLA_CHUNK = 64
SB_HEADS = 4
SB_DH = 64
SB_BLOCK = 128
MEM_HEADS = 4
MEM_DH = 64
GLA_QK_W = GLA_HEADS * GLA_DK
GLA_V_W = GLA_HEADS * GLA_DV
SB_W = SB_HEADS * SB_DH
MEM_W = MEM_HEADS * MEM_DH
ALPHA = (2.0 * DEPTH) ** 0.25
LN_EPS = 1e-5
RMS_EPS = 1e-6

LANES = 128
GLA_COLS = 2 * GLA_QK_W + 2 * GLA_V_W
SB_COLS = 3 * SB_W
C3_COLS = SB_W + 2 * MEM_W
GA_COLS = LANES
PROJ_COLS = GLA_COLS + SB_COLS + C3_COLS + GA_COLS
VMEM_LIMIT = 56 * 1024 * 1024
SB_SKIP_LOG = -106.0
SB_LEAD = 3
SB_QTILE = 2


def _dot(a, b):
    return jnp.dot(a, b, preferred_element_type=F32)


def _dot_nt(a, b):
    return lax.dot_general(a, b, (((1,), (1,)), ((), ())), preferred_element_type=F32)


def _dot_tn(a, b):
    return lax.dot_general(a, b, (((0,), (0,)), ((), ())), preferred_element_type=F32)


def _split_bf16(x, n):
    parts = []
    for _ in range(n - 1):
        h = x.astype(BF16)
        parts.append(h)
        x = x - h.astype(F32)
    parts.append(x.astype(BF16))
    return parts


def _log_sigmoid(z):
    return jnp.minimum(z, 0.0) - jnp.log(1.0 + jnp.exp(-jnp.abs(z)))


def _silu(g):
    return g / (1.0 + jnp.exp(-g))


def _proj_kernel(x_ref, w_ref, gla_ref, sb_ref, c3_ref, ga_ref):
    xb = x_ref[...].astype(BF16)
    off = 0
    for ref, width in ((gla_ref, GLA_COLS), (sb_ref, SB_COLS), (c3_ref, C3_COLS), (ga_ref, GA_COLS)):
        step = 512 if width % 512 == 0 else width if width < 512 else 256
        for c in range(0, width, step):
            ref[:, c:c + step] = _dot(xb, w_ref[:, off + c:off + c + step]).astype(BF16)
        off += width


def _proj(x2, w, tm=512):
    n = x2.shape[0]
    outs = (GLA_COLS, SB_COLS, C3_COLS, GA_COLS)
    return pl.pallas_call(
        _proj_kernel,
        grid=(n // tm,),
        in_specs=[pl.BlockSpec((tm, D_MODEL), lambda i: (i, 0)),
                  pl.BlockSpec((D_MODEL, PROJ_COLS), lambda i: (0, 0))],
        out_specs=[pl.BlockSpec((tm, c), lambda i: (i, 0)) for c in outs],
        out_shape=[jax.ShapeDtypeStruct((n, c), BF16) for c in outs],
        compiler_params=pltpu.CompilerParams(dimension_semantics=("arbitrary",),
                                             vmem_limit_bytes=VMEM_LIMIT),
        name="proj",
    )(x2, w)


def _memkv_kernel(m_ref, w_ref, o_ref):
    o_ref[0] = _dot(m_ref[...].astype(BF16), w_ref[0]).astype(BF16)


def _memkv(mem2, w_mkv):
    n = mem2.shape[0]
    return pl.pallas_call(
        _memkv_kernel,
        grid=(DEPTH,),
        in_specs=[pl.BlockSpec((n, D_MODEL), lambda l: (0, 0)),
                  pl.BlockSpec((1, D_MODEL, 2 * MEM_W), lambda l: (l, 0, 0))],
        out_specs=pl.BlockSpec((1, n, 2 * MEM_W), lambda l: (l, 0, 0)),
        out_shape=jax.ShapeDtypeStruct((DEPTH, n, 2 * MEM_W), BF16),
        compiler_params=pltpu.CompilerParams(dimension_semantics=("arbitrary",),
                                             vmem_limit_bytes=VMEM_LIMIT),
        name="memkv",
    )(mem2, w_mkv)


def _gla_kernel(gla_ref, ga_ref, wa2_ref, ba_ref, nw_ref, out_ref, st_ref, *, tt):
    c_sz = GLA_CHUNK

    @pl.when(pl.program_id(1) == 0)
    def _():
        st_ref[...] = jnp.zeros_like(st_ref)

    row = lax.broadcasted_iota(jnp.int32, (c_sz, c_sz), 0)
    col = lax.broadcasted_iota(jnp.int32, (c_sz, c_sz), 1)
    causal = col <= row
    tri_incl = jnp.where(causal, 1.0, 0.0).astype(BF16)
    low_c = lax.broadcasted_iota(jnp.int32, (c_sz, LANES), 1) < GLA_DK
    low_v = lax.broadcasted_iota(jnp.int32, (GLA_DV, LANES), 1) < GLA_DK

    def chunk(c, carry):
        rows = pl.ds(pl.multiple_of(c * c_sz, c_sz), c_sz)
        zz = _dot(ga_ref[0, rows, :], wa2_ref[...]) + ba_ref[...]
        log_a = _log_sigmoid(zz) * (1.0 / GLA_GATE_NORM)
        g_cum = sum(_dot(tri_incl, part) for part in _split_bf16(log_a, 3))
        g_last = g_cum[c_sz - 1:c_sz, :]
        q = gla_ref[0, rows, 0:GLA_QK_W].astype(F32)
        k = gla_ref[0, rows, GLA_QK_W:2 * GLA_QK_W].astype(F32)
        qg = (q * jnp.exp(g_cum)).astype(BF16)
        kg = (k * jnp.exp(-g_cum)).astype(BF16)
        kd = (k * jnp.exp(g_last - g_cum)).astype(BF16)
        decay = jnp.exp(g_last)
        for p in range(GLA_HEADS // 2):
            sl = slice(p * LANES, (p + 1) * LANES)
            qgp = qg[:, sl]
            q2 = jnp.concatenate([jnp.where(low_c, qgp, jnp.zeros_like(qgp)),
                                  jnp.where(low_c, jnp.zeros_like(qgp), qgp)], axis=0)
            st = st_ref[p]
            a2 = _dot_nt(q2, kg[:, sl])
            inter2 = _dot_nt(q2, st.astype(BF16))
            upd = []
            for hh in range(2):
                h = 2 * p + hh
                v_h = gla_ref[0, rows, 2 * GLA_QK_W + h * GLA_DV:2 * GLA_QK_W + (h + 1) * GLA_DV]
                a = jnp.where(causal, a2[hh * c_sz:(hh + 1) * c_sz], 0.0).astype(BF16)
                o = _dot(a, v_h) + inter2[hh * c_sz:(hh + 1) * c_sz]
                ms = jnp.mean(o * o, axis=-1, keepdims=True)
                gate = gla_ref[0, rows, 2 * GLA_QK_W + GLA_V_W + h * GLA_DV:
                               2 * GLA_QK_W + GLA_V_W + (h + 1) * GLA_DV].astype(F32)
                res = o * lax.rsqrt(ms + RMS_EPS) * nw_ref[...] * _silu(gate)
                out_ref[0, rows, h * GLA_DV:(h + 1) * GLA_DV] = res.astype(BF16)
                upd.append(_dot_tn(v_h, kd[:, sl]))
            st_ref[p] = st * decay[:, sl] + jnp.where(low_v, upd[0], upd[1])
        return carry

    lax.fori_loop(0, tt // c_sz, chunk, 0)


def _gla(gla, ga, wa2, ba, nw, tt=512):
    b, t, _ = gla.shape
    return pl.pallas_call(
        functools.partial(_gla_kernel, tt=tt),
        grid=(b, t // tt),
        in_specs=[pl.BlockSpec((1, tt, GLA_COLS), lambda i, j: (i, j, 0)),
                  pl.BlockSpec((1, tt, GA_COLS), lambda i, j: (i, j, 0)),
                  pl.BlockSpec((GA_COLS, GLA_QK_W), lambda i, j: (0, 0)),
                  pl.BlockSpec((1, GLA_QK_W), lambda i, j: (0, 0)),
                  pl.BlockSpec((1, GLA_DV), lambda i, j: (0, 0))],
        out_specs=pl.BlockSpec((1, tt, GLA_V_W), lambda i, j: (i, j, 0)),
        out_shape=jax.ShapeDtypeStruct((b, t, GLA_V_W), BF16),
        scratch_shapes=[pltpu.VMEM((GLA_HEADS // 2, GLA_DV, LANES), F32)],
        compiler_params=pltpu.CompilerParams(dimension_semantics=("arbitrary", "arbitrary"),
                                             vmem_limit_bytes=VMEM_LIMIT),
        name="gla",
    )(gla, ga, wa2, ba, nw)


def _sb_kernel(q_ref, k_ref, v_ref, g_ref, o_ref, oacc_ref, lacc_ref):
    qb = SB_BLOCK
    pairs = SB_HEADS // 2
    i = pl.program_id(1)
    row = lax.broadcasted_iota(jnp.int32, (qb, qb), 0)
    lane = lax.broadcasted_iota(jnp.int32, (qb, qb), 1)
    strict = lane < row
    low = lane < SB_DH
    r2 = lax.broadcasted_iota(jnp.int32, (2 * qb, qb), 0) & (qb - 1)
    c2 = lax.broadcasted_iota(jnp.int32, (2 * qb, qb), 1)
    tri2 = jnp.where(r2 > c2, 1.0, 0.0).astype(BF16)

    def stream_rows(qi, p, hh):
        st = (qi * pairs + p) * 2 + hh
        return slice(st * qb, (st + 1) * qb)

    qneg = {}
    for qi in range(SB_QTILE):
        for p in range(pairs):
            qn = -q_ref[0, qi * qb:(qi + 1) * qb, p * LANES:(p + 1) * LANES]
            zero = jnp.zeros_like(qn)
            qneg[qi, p, 0] = jnp.where(low, qn, zero)
            qneg[qi, p, 1] = jnp.where(low, zero, qn)
    oacc_ref[...] = jnp.zeros_like(oacc_ref)
    lacc_ref[...] = jnp.zeros_like(lacc_ref)

    def group(j_lo, n, diagonal):
        order = list(reversed(range(n)))
        streams = [(qi, p, hh) for qi in j_lo for p in range(pairs) for hh in range(2)]
        rows = {qi: pl.ds(pl.multiple_of(j_lo[qi] * qb, qb), n * qb) for qi in j_lo}
        s = {st: _dot_nt(qneg[st], k_ref[0, rows[st[0]], st[1] * LANES:(st[1] + 1) * LANES])
             for st in streams}
        log_beta, lf, within = {}, {}, {}
        for g in order:
            for st in streams:
                sg = s[st][:, g * qb:(g + 1) * qb]
                lsn = _log_sigmoid(sg)
                log_beta[st, g] = lsn - sg
                lf[st, g] = jnp.where(strict, lsn, 0.0) if (diagonal and g == n - 1) else lsn
        for g in order:
            for st in streams:
                within[st, g] = _dot(jnp.concatenate(_split_bf16(lf[st, g], 2), axis=1), tri2)
        acc = {st: lacc_ref[stream_rows(*st)] for st in streams}
        ws = {st: [None] * n for st in streams}
        for g in order:
            for st in streams:
                w = jnp.exp(log_beta[st, g] + within[st, g] + acc[st])
                if diagonal and g == n - 1:
                    w = jnp.where(strict, w, 0.0)
                ws[st][g] = w.astype(BF16)
                acc[st] = acc[st] + jnp.sum(lf[st, g], axis=-1, keepdims=True)
        for st in streams:
            lacc_ref[stream_rows(*st)] = acc[st]
            oacc_ref[stream_rows(*st)] += _dot(jnp.concatenate(ws[st], axis=1),
                                               v_ref[0, rows[st[0]], st[1] * LANES:(st[1] + 1) * LANES])

    def sweep(qi, j_hi):
        per_q = 2 * pairs * qb

        def live():
            return jnp.max(lacc_ref[qi * per_q:(qi + 1) * per_q])

        def cond(c):
            return jnp.logical_and(c[0] >= 0, c[1] > SB_SKIP_LOG)

        def body(c):
            group({qi: c[0]}, 1, False)
            return c[0] - 1, live()

        lax.while_loop(cond, body, (j_hi, live()))

    first = i * SB_QTILE

    @pl.when(first >= SB_LEAD - 1)
    def _():
        group({qi: first + qi - (SB_LEAD - 1) for qi in range(SB_QTILE)}, SB_LEAD, True)
        for qi in range(SB_QTILE):
            sweep(qi, first + qi - SB_LEAD)

    @pl.when(first < SB_LEAD - 1)
    def _():
        for qi in range(SB_QTILE):
            group({qi: first + qi}, 1, True)
            sweep(qi, first + qi - 1)

    for qi in range(SB_QTILE):
        for p in range(pairs):
            o = jnp.where(low, oacc_ref[stream_rows(qi, p, 0)], oacc_ref[stream_rows(qi, p, 1)])
            gate = g_ref[0, qi * qb:(qi + 1) * qb, p * LANES:(p + 1) * LANES].astype(F32)
            o_ref[0, qi * qb:(qi + 1) * qb, p * LANES:(p + 1) * LANES] = (o * _silu(gate)).astype(BF16)


def _sb(sb, c3):
    b, t, _ = sb.shape
    tq = SB_QTILE * SB_BLOCK
    n_streams = SB_QTILE * SB_HEADS
    return pl.pallas_call(
        _sb_kernel,
        grid=(b, t // tq),
        in_specs=[pl.BlockSpec((1, tq, SB_W), lambda bi, i: (bi, i, 0)),
                  pl.BlockSpec((1, t, SB_W), lambda bi, i: (bi, 0, 1)),
                  pl.BlockSpec((1, t, SB_W), lambda bi, i: (bi, 0, 2)),
                  pl.BlockSpec((1, tq, SB_W), lambda bi, i: (bi, i, 0))],
        out_specs=pl.BlockSpec((1, tq, SB_W), lambda bi, i: (bi, i, 0)),
        out_shape=jax.ShapeDtypeStruct((b, t, SB_W), BF16),
        scratch_shapes=[pltpu.VMEM((n_streams * SB_BLOCK, LANES), F32),
                        pltpu.VMEM((n_streams * SB_BLOCK, LANES), F32)],
        compiler_params=pltpu.CompilerParams(dimension_semantics=("arbitrary", "arbitrary"),
                                             vmem_limit_bytes=VMEM_LIMIT),
        name="sb",
    )(sb, sb, sb, c3)


def _out_kernel(x_ref, mixg_ref, mixs_ref, c3_ref, kvm_ref, wout_ref, lng_ref, lnb_ref, o_ref, *, tm):
    low = lax.broadcasted_iota(jnp.int32, (tm, LANES), 1) < MEM_DH
    y = _dot(mixg_ref[0], wout_ref[0:GLA_V_W, :])
    y += _dot(mixs_ref[0], wout_ref[GLA_V_W:GLA_V_W + SB_W, :])
    for p in range(MEM_HEADS // 2):
        qp = c3_ref[0, :, SB_W + p * LANES:SB_W + (p + 1) * LANES]
        km = kvm_ref[0, 0, :, p * LANES:(p + 1) * LANES]
        vm = kvm_ref[0, 0, :, MEM_W + p * LANES:MEM_W + (p + 1) * LANES]
        heads = []
        for hh in range(2):
            qm = jnp.where(low, qp, jnp.zeros_like(qp)) if hh == 0 else jnp.where(low, jnp.zeros_like(qp), qp)
            s = _dot_nt(qm, km)
            e = jnp.exp(s - jnp.max(s, axis=-1, keepdims=True))
            den = jnp.sum(e, axis=-1, keepdims=True)
            heads.append(_dot(e.astype(BF16), vm) / den)
        om = jnp.where(low, heads[0], heads[1])
        gate = c3_ref[0, :, SB_W + MEM_W + p * LANES:SB_W + MEM_W + (p + 1) * LANES].astype(F32)
        mixm = (om * _silu(gate)).astype(BF16)
        base = GLA_V_W + SB_W + p * LANES
        y += _dot(mixm, wout_ref[base:base + LANES, :])
    r = ALPHA * x_ref[0] + y
    mu = jnp.mean(r, axis=-1, keepdims=True)
    d = r - mu
    var = jnp.mean(d * d, axis=-1, keepdims=True)
    o_ref[0] = d * lax.rsqrt(var + LN_EPS) * lng_ref[...] + lnb_ref[...]


def _out(x, mixg, mixs, c3, kvm, layer, wout, lng, lnb, tm=256):
    b, t, _ = x.shape
    return pl.pallas_call(
        functools.partial(_out_kernel, tm=tm),
        grid=(b, t // tm),
        in_specs=[pl.BlockSpec((1, tm, D_MODEL), lambda i, j: (i, j, 0)),
                  pl.BlockSpec((1, tm, GLA_V_W), lambda i, j: (i, j, 0)),
                  pl.BlockSpec((1, tm, SB_W), lambda i, j: (i, j, 0)),
                  pl.BlockSpec((1, tm, C3_COLS), lambda i, j: (i, j, 0)),
                  pl.BlockSpec((1, 1, N_MEM, 2 * MEM_W), lambda i, j: (layer, i, 0, 0)),
                  pl.BlockSpec((D_MODEL, D_MODEL), lambda i, j: (0, 0)),
                  pl.BlockSpec((1, D_MODEL), lambda i, j: (0, 0)),
                  pl.BlockSpec((1, D_MODEL), lambda i, j: (0, 0))],
        out_specs=pl.BlockSpec((1, tm, D_MODEL), lambda i, j: (i, j, 0)),
        out_shape=jax.ShapeDtypeStruct((b, t, D_MODEL), F32),
        compiler_params=pltpu.CompilerParams(dimension_semantics=("arbitrary", "arbitrary"),
                                             vmem_limit_bytes=VMEM_LIMIT),
        name="out",
    )(x, mixg, mixs, c3, kvm, wout, lng, lnb)


def _regroup_w_in(w):
    o = 0
    seg = {}
    for name, n in (("gq", GLA_QK_W), ("gk", GLA_QK_W), ("gv", GLA_V_W), ("gg", GLA_V_W), ("ga", GLA_RANK),
                    ("sq", SB_W), ("sk", SB_W), ("sv", SB_W), ("sg", SB_W), ("mq", MEM_W), ("mg", MEM_W)):
        seg[name] = w[:, o:o + n]
        o += n
    pad = jnp.zeros((w.shape[0], GA_COLS - GLA_RANK), w.dtype)
    cols = [seg["gq"] * GLA_DK ** -0.5, seg["gk"], seg["gv"], seg["gg"],
            seg["sq"] * SB_DH ** -0.5, seg["sk"], seg["sv"],
            seg["sg"], seg["mq"] * MEM_DH ** -0.5, seg["mg"],
            seg["ga"], pad]
    return jnp.concatenate(cols, axis=1).astype(BF16)


def kernel(x, mem, w_in, w_alpha2, b_alpha, gla_norm_w, w_mem_kv, w_out, ln_g, ln_b):
    b, t, d = x.shape
    kvm = _memkv(mem.reshape(b * N_MEM, d), w_mem_kv.astype(BF16)).reshape(DEPTH, b, N_MEM, 2 * MEM_W)
    for l in range(DEPTH):
        gla, sb, c3, ga = _proj(x.reshape(b * t, d), _regroup_w_in(w_in[l]))
        wa2 = jnp.zeros((GA_COLS, GLA_QK_W), F32).at[:GLA_RANK].set(w_alpha2[l]).astype(BF16)
        mixg = _gla(gla.reshape(b, t, GLA_COLS), ga.reshape(b, t, GA_COLS), wa2,
                    b_alpha[l].reshape(1, GLA_QK_W), gla_norm_w[l].reshape(1, GLA_DV))
        c3 = c3.reshape(b, t, C3_COLS)
        mixs = _sb(sb.reshape(b, t, SB_COLS), c3)
        x = _out(x, mixg, mixs, c3, kvm, l, w_out[l].astype(BF16),
                 ln_g[l].reshape(1, d), ln_b[l].reshape(1, d))
    return x
```

```python
import functools

import jax
import jax.numpy as jnp
from jax import lax
from jax.experimental import pallas as pl
from jax.experimental.pallas import tpu as pltpu

F32 = jnp.float32
BF16 = jnp.bfloat16

D_MODEL = 1024
DEPTH = 4
N_MEM = 256
GLA_HEADS = 4
GLA_DK = 64
GLA_DV = 128
GLA_RANK = 16
GLA_GATE_NORM = 16.0
GLA_CHUNK = 64
SB_HEADS = 4
SB_DH = 64
SB_BLOCK = 128
MEM_HEADS = 4
MEM_DH = 64
GLA_QK_W = GLA_HEADS * GLA_DK
GLA_V_W = GLA_HEADS * GLA_DV
SB_W = SB_HEADS * SB_DH
MEM_W = MEM_HEADS * MEM_DH
ALPHA = (2.0 * DEPTH) ** 0.25
LN_EPS = 1e-5
RMS_EPS = 1e-6

LANES = 128
GLA_COLS = 2 * GLA_QK_W + 2 * GLA_V_W
SB_COLS = 3 * SB_W
C3_COLS = SB_W + 2 * MEM_W
GA_COLS = LANES
PROJ_COLS = GLA_COLS + SB_COLS + C3_COLS + GA_COLS
VMEM_LIMIT = 56 * 1024 * 1024
SB_SKIP_LOG = -106.0
SB_LEAD = 3
SB_QTILE = 2


def _dot(a, b):
    return jnp.dot(a, b, preferred_element_type=F32)


def _dot_nt(a, b):
    return lax.dot_general(a, b, (((1,), (1,)), ((), ())), preferred_element_type=F32)


def _dot_tn(a, b):
    return lax.dot_general(a, b, (((0,), (0,)), ((), ())), preferred_element_type=F32)


def _split_bf16(x, n):
    parts = []
    for _ in range(n - 1):
        h = x.astype(BF16)
        parts.append(h)
        x = x - h.astype(F32)
    parts.append(x.astype(BF16))
    return parts


def _log_sigmoid(z):
    return jnp.minimum(z, 0.0) - jnp.log(1.0 + jnp.exp(-jnp.abs(z)))


def _silu(g):
    return g / (1.0 + jnp.exp(-g))


def _proj_kernel(x_ref, w_ref, gla_ref, sb_ref, c3_ref, ga_ref):
    xb = x_ref[...].astype(BF16)
    off = 0
    for ref, width in ((gla_ref, GLA_COLS), (sb_ref, SB_COLS), (c3_ref, C3_COLS), (ga_ref, GA_COLS)):
        step = 512 if width % 512 == 0 else width if width < 512 else 256
        for c in range(0, width, step):
            ref[:, c:c + step] = _dot(xb, w_ref[:, off + c:off + c + step]).astype(BF16)
        off += width


def _proj(x2, w, tm=512):
    n = x2.shape[0]
    outs = (GLA_COLS, SB_COLS, C3_COLS, GA_COLS)
    return pl.pallas_call(
        _proj_kernel,
        grid=(n // tm,),
        in_specs=[pl.BlockSpec((tm, D_MODEL), lambda i: (i, 0)),
                  pl.BlockSpec((D_MODEL, PROJ_COLS), lambda i: (0, 0))],
        out_specs=[pl.BlockSpec((tm, c), lambda i: (i, 0)) for c in outs],
        out_shape=[jax.ShapeDtypeStruct((n, c), BF16) for c in outs],
        compiler_params=pltpu.CompilerParams(dimension_semantics=("arbitrary",),
                                             vmem_limit_bytes=VMEM_LIMIT),
        name="proj",
    )(x2, w)


def _memkv_kernel(m_ref, w_ref, o_ref):
    o_ref[0] = _dot(m_ref[...].astype(BF16), w_ref[0]).astype(BF16)


def _memkv(mem2, w_mkv):
    n = mem2.shape[0]
    return pl.pallas_call(
        _memkv_kernel,
        grid=(DEPTH,),
        in_specs=[pl.BlockSpec((n, D_MODEL), lambda l: (0, 0)),
                  pl.BlockSpec((1, D_MODEL, 2 * MEM_W), lambda l: (l, 0, 0))],
        out_specs=pl.BlockSpec((1, n, 2 * MEM_W), lambda l: (l, 0, 0)),
        out_shape=jax.ShapeDtypeStruct((DEPTH, n, 2 * MEM_W), BF16),
        compiler_params=pltpu.CompilerParams(dimension_semantics=("arbitrary",),
                                             vmem_limit_bytes=VMEM_LIMIT),
        name="memkv",
    )(mem2, w_mkv)


def _gla_kernel(gla_ref, ga_ref, wa2_ref, ba_ref, nw_ref, out_ref, st_ref, *, tt):
    c_sz = GLA_CHUNK

    @pl.when(pl.program_id(1) == 0)
    def _():
        st_ref[...] = jnp.zeros_like(st_ref)

    row = lax.broadcasted_iota(jnp.int32, (c_sz, c_sz), 0)
    col = lax.broadcasted_iota(jnp.int32, (c_sz, c_sz), 1)
    causal = col <= row
    r2 = lax.broadcasted_iota(jnp.int32, (c_sz, 2 * c_sz), 0)
    c2 = lax.broadcasted_iota(jnp.int32, (c_sz, 2 * c_sz), 1) & (c_sz - 1)
    tri_incl2 = jnp.where(c2 <= r2, 1.0, 0.0).astype(BF16)
    low_c = lax.broadcasted_iota(jnp.int32, (c_sz, LANES), 1) < GLA_DK
    low_v = lax.broadcasted_iota(jnp.int32, (GLA_DV, LANES), 1) < GLA_DK
    chunks = range(tt // c_sz)
    pairs = range(GLA_HEADS // 2)
    rows = [slice(c * c_sz, (c + 1) * c_sz) for c in chunks]
    lanes = [slice(p * LANES, (p + 1) * LANES) for p in pairs]
    v_col = 2 * GLA_QK_W
    g_col = v_col + GLA_V_W

    zz = _dot(ga_ref[0], wa2_ref[...]) + ba_ref[...]
    log_a = _log_sigmoid(zz) * (1.0 / GLA_GATE_NORM)
    hi, lo = _split_bf16(log_a, 2)
    g_cum = [_dot(tri_incl2, jnp.concatenate([hi[rows[c]], lo[rows[c]]], axis=0)) for c in chunks]
    qg, kg, kd, decay = [], [], [], []
    for c in chunks:
        g_last = g_cum[c][c_sz - 1:c_sz, :]
        q = gla_ref[0, rows[c], 0:GLA_QK_W].astype(F32)
        k = gla_ref[0, rows[c], GLA_QK_W:2 * GLA_QK_W].astype(F32)
        qg.append((q * jnp.exp(g_cum[c])).astype(BF16))
        kg.append((k * jnp.exp(-g_cum[c])).astype(BF16))
        kd.append((k * jnp.exp(g_last - g_cum[c])).astype(BF16))
        decay.append(jnp.exp(g_last))
    q2, a2, upd = {}, {}, {}
    for c in chunks:
        for p in pairs:
            qgp = qg[c][:, lanes[p]]
            q2[c, p] = jnp.concatenate([jnp.where(low_c, qgp, jnp.zeros_like(qgp)),
                                        jnp.where(low_c, jnp.zeros_like(qgp), qgp)], axis=0)
            a2[c, p] = _dot_nt(q2[c, p], kg[c][:, lanes[p]])
            for hh in range(2):
                h = 2 * p + hh
                v_h = gla_ref[0, rows[c], v_col + h * GLA_DV:v_col + (h + 1) * GLA_DV]
                upd[c, p, hh] = _dot_tn(v_h, kd[c][:, lanes[p]])
    inter2 = {}
    for p in pairs:
        st = st_ref[p]
        for c in chunks:
            inter2[c, p] = _dot_nt(q2[c, p], st.astype(BF16))
            st = st * decay[c][:, lanes[p]] + jnp.where(low_v, upd[c, p, 0], upd[c, p, 1])
        st_ref[p] = st
    for c in chunks:
        for p in pairs:
            for hh in range(2):
                h = 2 * p + hh
                v_h = gla_ref[0, rows[c], v_col + h * GLA_DV:v_col + (h + 1) * GLA_DV]
                a = jnp.where(causal, a2[c, p][hh * c_sz:(hh + 1) * c_sz], 0.0).astype(BF16)
                o = _dot(a, v_h) + inter2[c, p][hh * c_sz:(hh + 1) * c_sz]
                ms = jnp.mean(o * o, axis=-1, keepdims=True)
                gate = gla_ref[0, rows[c], g_col + h * GLA_DV:g_col + (h + 1) * GLA_DV].astype(F32)
                res = o * lax.rsqrt(ms + RMS_EPS) * nw_ref[...] * _silu(gate)
                out_ref[0, rows[c], h * GLA_DV:(h + 1) * GLA_DV] = res.astype(BF16)


def _gla(gla, ga, wa2, ba, nw, tt=512):
    b, t, _ = gla.shape
    return pl.pallas_call(
        functools.partial(_gla_kernel, tt=tt),
        grid=(b, t // tt),
        in_specs=[pl.BlockSpec((1, tt, GLA_COLS), lambda i, j: (i, j, 0)),
                  pl.BlockSpec((1, tt, GA_COLS), lambda i, j: (i, j, 0)),
                  pl.BlockSpec((GA_COLS, GLA_QK_W), lambda i, j: (0, 0)),
                  pl.BlockSpec((1, GLA_QK_W), lambda i, j: (0, 0)),
                  pl.BlockSpec((1, GLA_DV), lambda i, j: (0, 0))],
        out_specs=pl.BlockSpec((1, tt, GLA_V_W), lambda i, j: (i, j, 0)),
        out_shape=jax.ShapeDtypeStruct((b, t, GLA_V_W), BF16),
        scratch_shapes=[pltpu.VMEM((GLA_HEADS // 2, GLA_DV, LANES), F32)],
        compiler_params=pltpu.CompilerParams(dimension_semantics=("arbitrary", "arbitrary"),
                                             vmem_limit_bytes=VMEM_LIMIT),
        name="gla",
    )(gla, ga, wa2, ba, nw)


def _sb_kernel(q_ref, k_ref, v_ref, g_ref, o_ref, oacc_ref, lacc_ref):
    qb = SB_BLOCK
    pairs = SB_HEADS // 2
    i = pl.program_id(1)
    row = lax.broadcasted_iota(jnp.int32, (qb, qb), 0)
    lane = lax.broadcasted_iota(jnp.int32, (qb, qb), 1)
    strict = lane < row
    low = lane < SB_DH
    r2 = lax.broadcasted_iota(jnp.int32, (2 * qb, qb), 0) & (qb - 1)
    c2 = lax.broadcasted_iota(jnp.int32, (2 * qb, qb), 1)
    tri2 = jnp.where(r2 > c2, 1.0, 0.0).astype(BF16)

    def stream_rows(qi, p, hh):
        st = (qi * pairs + p) * 2 + hh
        return slice(st * qb, (st + 1) * qb)

    qneg = {}
    for qi in range(SB_QTILE):
        for p in range(pairs):
            qn = -q_ref[0, qi * qb:(qi + 1) * qb, p * LANES:(p + 1) * LANES]
            zero = jnp.zeros_like(qn)
            qneg[qi, p, 0] = jnp.where(low, qn, zero)
            qneg[qi, p, 1] = jnp.where(low, zero, qn)
    oacc_ref[...] = jnp.zeros_like(oacc_ref)
    lacc_ref[...] = jnp.zeros_like(lacc_ref)

    def group(j_lo, n, diagonal):
        order = list(reversed(range(n)))
        streams = [(qi, p, hh) for qi in j_lo for p in range(pairs) for hh in range(2)]
        rows = {qi: pl.ds(pl.multiple_of(j_lo[qi] * qb, qb), n * qb) for qi in j_lo}
        s = {st: _dot_nt(qneg[st], k_ref[0, rows[st[0]], st[1] * LANES:(st[1] + 1) * LANES])
             for st in streams}
        log_beta, lf, within = {}, {}, {}
        for g in order:
            for st in streams:
                sg = s[st][:, g * qb:(g + 1) * qb]
                lsn = _log_sigmoid(sg)
                log_beta[st, g] = lsn - sg
                lf[st, g] = jnp.where(strict, lsn, 0.0) if (diagonal and g == n - 1) else lsn
        for g in order:
            for st in streams:
                within[st, g] = _dot(jnp.concatenate(_split_bf16(lf[st, g], 2), axis=1), tri2)
        acc = {st: lacc_ref[stream_rows(*st)] for st in streams}
        ws = {st: [None] * n for st in streams}
        for g in order:
            for st in streams:
                w = jnp.exp(log_beta[st, g] + within[st, g] + acc[st])
                if diagonal and g == n - 1:
                    w = jnp.where(strict, w, 0.0)
                ws[st][g] = w.astype(BF16)
                acc[st] = acc[st] + jnp.sum(lf[st, g], axis=-1, keepdims=True)
        for st in streams:
            lacc_ref[stream_rows(*st)] = acc[st]
            oacc_ref[stream_rows(*st)] += _dot(jnp.concatenate(ws[st], axis=1),
                                               v_ref[0, rows[st[0]], st[1] * LANES:(st[1] + 1) * LANES])

    def sweep(qi, j_hi):
        per_q = 2 * pairs * qb

        def live():
            return jnp.max(lacc_ref[qi * per_q:(qi + 1) * per_q])

        def cond(c):
            return jnp.logical_and(c[0] >= 0, c[1] > SB_SKIP_LOG)

        def body(c):
            group({qi: c[0]}, 1, False)
            return c[0] - 1, live()

        lax.while_loop(cond, body, (j_hi, live()))

    first = i * SB_QTILE

    @pl.when(first >= SB_LEAD - 1)
    def _():
        group({qi: first + qi - (SB_LEAD - 1) for qi in range(SB_QTILE)}, SB_LEAD, True)
        for qi in range(SB_QTILE):
            sweep(qi, first + qi - SB_LEAD)

    @pl.when(first < SB_LEAD - 1)
    def _():
        for qi in range(SB_QTILE):
            group({qi: first + qi}, 1, True)
            sweep(qi, first + qi - 1)

    for qi in range(SB_QTILE):
        for p in range(pairs):
            o = jnp.where(low, oacc_ref[stream_rows(qi, p, 0)], oacc_ref[stream_rows(qi, p, 1)])
            gate = g_ref[0, qi * qb:(qi + 1) * qb, p * LANES:(p + 1) * LANES].astype(F32)
            o_ref[0, qi * qb:(qi + 1) * qb, p * LANES:(p + 1) * LANES] = (o * _silu(gate)).astype(BF16)


def _sb(sb, c3):
    b, t, _ = sb.shape
    tq = SB_QTILE * SB_BLOCK
    n_streams = SB_QTILE * SB_HEADS
    return pl.pallas_call(
        _sb_kernel,
        grid=(b, t // tq),
        in_specs=[pl.BlockSpec((1, tq, SB_W), lambda bi, i: (bi, i, 0)),
                  pl.BlockSpec((1, t, SB_W), lambda bi, i: (bi, 0, 1)),
                  pl.BlockSpec((1, t, SB_W), lambda bi, i: (bi, 0, 2)),
                  pl.BlockSpec((1, tq, SB_W), lambda bi, i: (bi, i, 0))],
        out_specs=pl.BlockSpec((1, tq, SB_W), lambda bi, i: (bi, i, 0)),
        out_shape=jax.ShapeDtypeStruct((b, t, SB_W), BF16),
        scratch_shapes=[pltpu.VMEM((n_streams * SB_BLOCK, LANES), F32),
                        pltpu.VMEM((n_streams * SB_BLOCK, LANES), F32)],
        compiler_params=pltpu.CompilerParams(dimension_semantics=("arbitrary", "arbitrary"),
                                             vmem_limit_bytes=VMEM_LIMIT),
        name="sb",
    )(sb, sb, sb, c3)


def _out_kernel(x_ref, mixg_ref, mixs_ref, c3_ref, kvm_ref, wout_ref, lng_ref, lnb_ref, o_ref, *, tm):
    low = lax.broadcasted_iota(jnp.int32, (tm, LANES), 1) < MEM_DH
    pairs = range(MEM_HEADS // 2)
    s = {}
    for p in pairs:
        qp = c3_ref[0, :, SB_W + p * LANES:SB_W + (p + 1) * LANES]
        km = kvm_ref[0, 0, :, p * LANES:(p + 1) * LANES]
        zero = jnp.zeros_like(qp)
        s[p, 0] = _dot_nt(jnp.where(low, qp, zero), km)
        s[p, 1] = _dot_nt(jnp.where(low, zero, qp), km)
    y = _dot(mixg_ref[0], wout_ref[0:GLA_V_W, :])
    y += _dot(mixs_ref[0], wout_ref[GLA_V_W:GLA_V_W + SB_W, :])
    e, den = {}, {}
    for key, sc in s.items():
        ex = jnp.exp(sc - jnp.max(sc, axis=-1, keepdims=True))
        den[key] = jnp.sum(ex, axis=-1, keepdims=True)
        e[key] = ex.astype(BF16)
    for p in pairs:
        vm = kvm_ref[0, 0, :, MEM_W + p * LANES:MEM_W + (p + 1) * LANES]
        om = jnp.where(low, _dot(e[p, 0], vm) / den[p, 0], _dot(e[p, 1], vm) / den[p, 1])
        gate = c3_ref[0, :, SB_W + MEM_W + p * LANES:SB_W + MEM_W + (p + 1) * LANES].astype(F32)
        mixm = (om * _silu(gate)).astype(BF16)
        base = GLA_V_W + SB_W + p * LANES
        y += _dot(mixm, wout_ref[base:base + LANES, :])
    r = ALPHA * x_ref[0] + y
    mu = jnp.mean(r, axis=-1, keepdims=True)
    d = r - mu
    var = jnp.mean(d * d, axis=-1, keepdims=True)
    o_ref[0] = d * lax.rsqrt(var + LN_EPS) * lng_ref[...] + lnb_ref[...]


def _out(x, mixg, mixs, c3, kvm, layer, wout, lng, lnb, tm=512):
    b, t, _ = x.shape
    return pl.pallas_call(
        functools.partial(_out_kernel, tm=tm),
        grid=(b, t // tm),
        in_specs=[pl.BlockSpec((1, tm, D_MODEL), lambda i, j: (i, j, 0)),
                  pl.BlockSpec((1, tm, GLA_V_W), lambda i, j: (i, j, 0)),
                  pl.BlockSpec((1, tm, SB_W), lambda i, j: (i, j, 0)),
                  pl.BlockSpec((1, tm, C3_COLS), lambda i, j: (i, j, 0)),
                  pl.BlockSpec((1, 1, N_MEM, 2 * MEM_W), lambda i, j: (layer, i, 0, 0)),
                  pl.BlockSpec((D_MODEL, D_MODEL), lambda i, j: (0, 0)),
                  pl.BlockSpec((1, D_MODEL), lambda i, j: (0, 0)),
                  pl.BlockSpec((1, D_MODEL), lambda i, j: (0, 0))],
        out_specs=pl.BlockSpec((1, tm, D_MODEL), lambda i, j: (i, j, 0)),
        out_shape=jax.ShapeDtypeStruct((b, t, D_MODEL), F32),
        compiler_params=pltpu.CompilerParams(dimension_semantics=("arbitrary", "arbitrary"),
                                             vmem_limit_bytes=VMEM_LIMIT),
        name="out",
    )(x, mixg, mixs, c3, kvm, wout, lng, lnb)


def _regroup_w_in(w):
    o = 0
    seg = {}
    for name, n in (("gq", GLA_QK_W), ("gk", GLA_QK_W), ("gv", GLA_V_W), ("gg", GLA_V_W), ("ga", GLA_RANK),
                    ("sq", SB_W), ("sk", SB_W), ("sv", SB_W), ("sg", SB_W), ("mq", MEM_W), ("mg", MEM_W)):
        seg[name] = w[:, o:o + n]
        o += n
    pad = jnp.zeros((w.shape[0], GA_COLS - GLA_RANK), w.dtype)
    cols = [seg["gq"] * GLA_DK ** -0.5, seg["gk"], seg["gv"], seg["gg"],
            seg["sq"] * SB_DH ** -0.5, seg["sk"], seg["sv"],
            seg["sg"], seg["mq"] * MEM_DH ** -0.5, seg["mg"],
            seg["ga"], pad]
    return jnp.concatenate(cols, axis=1).astype(BF16)


def kernel(x, mem, w_in, w_alpha2, b_alpha, gla_norm_w, w_mem_kv, w_out, ln_g, ln_b):
    b, t, d = x.shape
    kvm = _memkv(mem.reshape(b * N_MEM, d), w_mem_kv.astype(BF16)).reshape(DEPTH, b, N_MEM, 2 * MEM_W)
    for l in range(DEPTH):
        gla, sb, c3, ga = _proj(x.reshape(b * t, d), _regroup_w_in(w_in[l]))
        wa2 = jnp.zeros((GA_COLS, GLA_QK_W), F32).at[:GLA_RANK].set(w_alpha2[l]).astype(BF16)
        mixg = _gla(gla.reshape(b, t, GLA_COLS), ga.reshape(b, t, GA_COLS), wa2,
                    b_alpha[l].reshape(1, GLA_QK_W), gla_norm_w[l].reshape(1, GLA_DV))
        c3 = c3.reshape(b, t, C3_COLS)
        mixs = _sb(sb.reshape(b, t, SB_COLS), c3)
        x = _out(x, mixg, mixs, c3, kvm, l, w_out[l].astype(BF16),
                 ln_g[l].reshape(1, d), ln_b[l].reshape(1, d))
    return x
```

```python
import functools

import jax
import jax.numpy as jnp
from jax import lax
from jax.experimental import pallas as pl
from jax.experimental.pallas import tpu as pltpu

F32 = jnp.float32
BF16 = jnp.bfloat16

D_MODEL = 1024
DEPTH = 4
N_MEM = 256
GLA_HEADS = 4
GLA_DK = 64
GLA_DV = 128
GLA_RANK = 16
GLA_GATE_NORM = 16.0
GLA_CHUNK = 64
SB_HEADS = 4
SB_DH = 64
SB_BLOCK = 128
MEM_HEADS = 4
MEM_DH = 64
GLA_QK_W = GLA_HEADS * GLA_DK
GLA_V_W = GLA_HEADS * GLA_DV
SB_W = SB_HEADS * SB_DH
MEM_W = MEM_HEADS * MEM_DH
ALPHA = (2.0 * DEPTH) ** 0.25
LN_EPS = 1e-5
RMS_EPS = 1e-6

LANES = 128
GLA_COLS = 2 * GLA_QK_W + 2 * GLA_V_W
SB_COLS = 3 * SB_W
C3_COLS = SB_W + 2 * MEM_W
GA_COLS = LANES
PROJ_COLS = GLA_COLS + SB_COLS + C3_COLS + GA_COLS
VMEM_LIMIT = 56 * 1024 * 1024
SB_SKIP_LOG = -106.0
SB_HALF = 64
SB_LEAD_KEYS = 256
SB_QTILE = 256


def _dot(a, b):
    return jnp.dot(a, b, preferred_element_type=F32)


def _dot_nt(a, b):
    return lax.dot_general(a, b, (((1,), (1,)), ((), ())), preferred_element_type=F32)


def _dot_tn(a, b):
    return lax.dot_general(a, b, (((0,), (0,)), ((), ())), preferred_element_type=F32)


def _split_bf16(x, n):
    parts = []
    for _ in range(n - 1):
        h = x.astype(BF16)
        parts.append(h)
        x = x - h.astype(F32)
    parts.append(x.astype(BF16))
    return parts


def _log_sigmoid(z):
    return jnp.minimum(z, 0.0) - jnp.log(1.0 + jnp.exp(-jnp.abs(z)))


def _silu(g):
    return g / (1.0 + jnp.exp(-g))


def _proj_kernel(x_ref, w_ref, gla_ref, sb_ref, c3_ref, ga_ref):
    xb = x_ref[...].astype(BF16)
    w_ref = w_ref.at[0]
    off = 0
    for ref, width in ((gla_ref, GLA_COLS), (sb_ref, SB_COLS), (c3_ref, C3_COLS), (ga_ref, GA_COLS)):
        step = 512 if width % 512 == 0 else width if width < 512 else 256
        for c in range(0, width, step):
            ref[:, c:c + step] = _dot(xb, w_ref[:, off + c:off + c + step]).astype(BF16)
        off += width


def _proj(x2, w, layer, tm=512):
    n = x2.shape[0]
    outs = (GLA_COLS, SB_COLS, C3_COLS, GA_COLS)
    return pl.pallas_call(
        _proj_kernel,
        grid=(n // tm,),
        in_specs=[pl.BlockSpec((tm, D_MODEL), lambda i: (i, 0)),
                  pl.BlockSpec((1, D_MODEL, PROJ_COLS), lambda i: (layer, 0, 0))],
        out_specs=[pl.BlockSpec((tm, c), lambda i: (i, 0)) for c in outs],
        out_shape=[jax.ShapeDtypeStruct((n, c), BF16) for c in outs],
        compiler_params=pltpu.CompilerParams(dimension_semantics=("arbitrary",),
                                             vmem_limit_bytes=VMEM_LIMIT),
        name="proj",
    )(x2, w)


def _memkv_kernel(m_ref, w_ref, o_ref):
    o_ref[0] = _dot(m_ref[...].astype(BF16), w_ref[0]).astype(BF16)


def _memkv(mem2, w_mkv):
    n = mem2.shape[0]
    return pl.pallas_call(
        _memkv_kernel,
        grid=(DEPTH,),
        in_specs=[pl.BlockSpec((n, D_MODEL), lambda l: (0, 0)),
                  pl.BlockSpec((1, D_MODEL, 2 * MEM_W), lambda l: (l, 0, 0))],
        out_specs=pl.BlockSpec((1, n, 2 * MEM_W), lambda l: (l, 0, 0)),
        out_shape=jax.ShapeDtypeStruct((DEPTH, n, 2 * MEM_W), BF16),
        compiler_params=pltpu.CompilerParams(dimension_semantics=("arbitrary",),
                                             vmem_limit_bytes=VMEM_LIMIT),
        name="memkv",
    )(mem2, w_mkv)


def _gla_kernel(gla_ref, ga_ref, wa2_ref, ba_ref, nw_ref, out_ref, st_ref, *, tt):
    c_sz = GLA_CHUNK

    @pl.when(pl.program_id(1) == 0)
    def _():
        st_ref[...] = jnp.zeros_like(st_ref)

    row = lax.broadcasted_iota(jnp.int32, (c_sz, c_sz), 0)
    col = lax.broadcasted_iota(jnp.int32, (c_sz, c_sz), 1)
    causal = col <= row
    r2 = lax.broadcasted_iota(jnp.int32, (c_sz, 2 * c_sz), 0)
    c2 = lax.broadcasted_iota(jnp.int32, (c_sz, 2 * c_sz), 1) & (c_sz - 1)
    tri_incl2 = jnp.where(c2 <= r2, 1.0, 0.0).astype(BF16)
    low_c = lax.broadcasted_iota(jnp.int32, (c_sz, LANES), 1) < GLA_DK
    low_v = lax.broadcasted_iota(jnp.int32, (GLA_DV, LANES), 1) < GLA_DK
    chunks = range(tt // c_sz)
    pairs = range(GLA_HEADS // 2)
    rows = [slice(c * c_sz, (c + 1) * c_sz) for c in chunks]
    lanes = [slice(p * LANES, (p + 1) * LANES) for p in pairs]
    v_col = 2 * GLA_QK_W
    g_col = v_col + GLA_V_W

    zz = _dot(ga_ref[0], wa2_ref[0]) + ba_ref[0]
    log_a = _log_sigmoid(zz) * (1.0 / GLA_GATE_NORM)
    hi, lo = _split_bf16(log_a, 2)
    g_cum = [_dot(tri_incl2, jnp.concatenate([hi[rows[c]], lo[rows[c]]], axis=0)) for c in chunks]
    qg, kg, kd, decay = [], [], [], []
    for c in chunks:
        g_last = g_cum[c][c_sz - 1:c_sz, :]
        q = gla_ref[0, rows[c], 0:GLA_QK_W].astype(F32)
        k = gla_ref[0, rows[c], GLA_QK_W:2 * GLA_QK_W].astype(F32)
        qg.append((q * jnp.exp(g_cum[c])).astype(BF16))
        kg.append((k * jnp.exp(-g_cum[c])).astype(BF16))
        kd.append((k * jnp.exp(g_last - g_cum[c])).astype(BF16))
        decay.append(jnp.exp(g_last))
    q2, a2, upd = {}, {}, {}
    for c in chunks:
        for p in pairs:
            qgp = qg[c][:, lanes[p]]
            q2[c, p] = jnp.concatenate([jnp.where(low_c, qgp, jnp.zeros_like(qgp)),
                                        jnp.where(low_c, jnp.zeros_like(qgp), qgp)], axis=0)
            a2[c, p] = _dot_nt(q2[c, p], kg[c][:, lanes[p]])
            for hh in range(2):
                h = 2 * p + hh
                v_h = gla_ref[0, rows[c], v_col + h * GLA_DV:v_col + (h + 1) * GLA_DV]
                upd[c, p, hh] = _dot_tn(v_h, kd[c][:, lanes[p]])
    inter2 = {}
    for p in pairs:
        st = st_ref[p]
        for c in chunks:
            inter2[c, p] = _dot_nt(q2[c, p], st.astype(BF16))
            st = st * decay[c][:, lanes[p]] + jnp.where(low_v, upd[c, p, 0], upd[c, p, 1])
        st_ref[p] = st
    for c in chunks:
        for p in pairs:
            for hh in range(2):
                h = 2 * p + hh
                v_h = gla_ref[0, rows[c], v_col + h * GLA_DV:v_col + (h + 1) * GLA_DV]
                a = jnp.where(causal, a2[c, p][hh * c_sz:(hh + 1) * c_sz], 0.0).astype(BF16)
                o = _dot(a, v_h) + inter2[c, p][hh * c_sz:(hh + 1) * c_sz]
                ms = jnp.mean(o * o, axis=-1, keepdims=True)
                gate = gla_ref[0, rows[c], g_col + h * GLA_DV:g_col + (h + 1) * GLA_DV].astype(F32)
                res = o * lax.rsqrt(ms + RMS_EPS) * nw_ref[0] * _silu(gate)
                out_ref[0, rows[c], h * GLA_DV:(h + 1) * GLA_DV] = res.astype(BF16)


def _gla(gla, ga, wa2, ba, nw, layer, tt=512):
    b, t, _ = gla.shape
    return pl.pallas_call(
        functools.partial(_gla_kernel, tt=tt),
        grid=(b, t // tt),
        in_specs=[pl.BlockSpec((1, tt, GLA_COLS), lambda i, j: (i, j, 0)),
                  pl.BlockSpec((1, tt, GA_COLS), lambda i, j: (i, j, 0)),
                  pl.BlockSpec((1, GA_COLS, GLA_QK_W), lambda i, j: (layer, 0, 0)),
                  pl.BlockSpec((1, 1, GLA_QK_W), lambda i, j: (layer, 0, 0)),
                  pl.BlockSpec((1, 1, GLA_DV), lambda i, j: (layer, 0, 0))],
        out_specs=pl.BlockSpec((1, tt, GLA_V_W), lambda i, j: (i, j, 0)),
        out_shape=jax.ShapeDtypeStruct((b, t, GLA_V_W), BF16),
        scratch_shapes=[pltpu.VMEM((GLA_HEADS // 2, GLA_DV, LANES), F32)],
        compiler_params=pltpu.CompilerParams(dimension_semantics=("arbitrary", "arbitrary"),
                                             vmem_limit_bytes=VMEM_LIMIT),
        name="gla",
    )(gla, ga, wa2, ba, nw)


def _sb_kernel(q_ref, k_ref, v_ref, g_ref, o_ref, oacc_ref, lacc_ref):
    hb, kb = SB_HALF, SB_BLOCK
    pairs = SB_HEADS // 2
    halves = SB_QTILE // hb
    q_base = pl.program_id(1) * SB_QTILE
    row = lax.broadcasted_iota(jnp.int32, (hb, kb), 0)
    lane = lax.broadcasted_iota(jnp.int32, (hb, kb), 1)
    low = lane < SB_DH
    rk = lax.broadcasted_iota(jnp.int32, (kb, kb), 0)
    ck = lax.broadcasted_iota(jnp.int32, (kb, kb), 1)
    tri = jnp.where(rk > ck, 1.0, 0.0).astype(BF16)

    def stream_rows(m, p, hh):
        st = (m * pairs + p) * 2 + hh
        return slice(st * hb, (st + 1) * hb)

    qneg = {}
    for m in range(halves):
        for p in range(pairs):
            qn = -q_ref[0, m * hb:(m + 1) * hb, p * LANES:(p + 1) * LANES]
            zero = jnp.zeros_like(qn)
            qneg[m, p, 0] = jnp.where(low, qn, zero)
            qneg[m, p, 1] = jnp.where(low, zero, qn)
    oacc_ref[...] = jnp.zeros_like(oacc_ref)
    lacc_ref[...] = jnp.zeros_like(lacc_ref)

    def group(start, n, valid):
        order = list(reversed(range(n)))
        streams = [(m, p, hh) for m in start for p in range(pairs) for hh in range(2)]
        rows = {m: pl.ds(pl.multiple_of(start[m], hb), n * kb) for m in start}
        s = {st: _dot_nt(qneg[st], k_ref[0, rows[st[0]], st[1] * LANES:(st[1] + 1) * LANES])
             for st in streams}
        log_beta, lf, within = {}, {}, {}
        for g in order:
            for st in streams:
                sg = s[st][:, g * kb:(g + 1) * kb]
                lsn = _log_sigmoid(sg)
                log_beta[st, g] = lsn - sg
                lf[st, g] = jnp.where(valid, lsn, 0.0) if g == n - 1 else lsn
        for g in order:
            for st in streams:
                within[st, g] = _dot(lf[st, g].astype(BF16), tri)
        acc = {st: lacc_ref[stream_rows(*st)] for st in streams}
        ws = {st: [None] * n for st in streams}
        for g in order:
            for st in streams:
                w = jnp.exp(log_beta[st, g] + within[st, g] + acc[st])
                if g == n - 1:
                    w = jnp.where(valid, w, 0.0)
                ws[st][g] = w.astype(BF16)
                acc[st] = acc[st] + jnp.sum(lf[st, g], axis=-1, keepdims=True)
        for st in streams:
            lacc_ref[stream_rows(*st)] = acc[st]
            oacc_ref[stream_rows(*st)] += _dot(jnp.concatenate(ws[st], axis=1),
                                               v_ref[0, rows[st[0]], st[1] * LANES:(st[1] + 1) * LANES])

    def sweep(m, bound):
        per_m = 2 * pairs * hb

        def live():
            return jnp.max(lacc_ref[m * per_m:(m + 1) * per_m])

        def cond(c):
            return jnp.logical_and(c[0] > 0, c[1] > SB_SKIP_LOG)

        def body(c):
            first = jnp.maximum(c[0] - kb, 0)
            group({m: first}, 1, lane < c[0] - first)
            return first, live()

        lax.while_loop(cond, body, (bound, live()))

    @pl.when(q_base >= SB_LEAD_KEYS - hb)
    def _():
        lead = {m: q_base + (m + 1) * hb - SB_LEAD_KEYS for m in range(halves)}
        group(lead, SB_LEAD_KEYS // kb, lane < row + (kb - hb))

        @pl.when(jnp.max(lacc_ref[...]) > SB_SKIP_LOG)
        def _():
            for m in range(halves):
                sweep(m, lead[m])

    @pl.when(q_base < SB_LEAD_KEYS - hb)
    def _():
        for m in range(halves):
            first = max(m * hb + hb - kb, 0)
            group({m: first}, 1, lane < row + (m * hb - first))
            sweep(m, first)

    for m in range(halves):
        for p in range(pairs):
            o = jnp.where(low, oacc_ref[stream_rows(m, p, 0)], oacc_ref[stream_rows(m, p, 1)])
            gate = g_ref[0, m * hb:(m + 1) * hb, p * LANES:(p + 1) * LANES].astype(F32)
            o_ref[0, m * hb:(m + 1) * hb, p * LANES:(p + 1) * LANES] = (o * _silu(gate)).astype(BF16)


def _sb(sb, c3):
    b, t, _ = sb.shape
    n_streams = (SB_QTILE // SB_HALF) * SB_HEADS
    return pl.pallas_call(
        _sb_kernel,
        grid=(b, t // SB_QTILE),
        in_specs=[pl.BlockSpec((1, SB_QTILE, SB_W), lambda bi, i: (bi, i, 0)),
                  pl.BlockSpec((1, t, SB_W), lambda bi, i: (bi, 0, 1)),
                  pl.BlockSpec((1, t, SB_W), lambda bi, i: (bi, 0, 2)),
                  pl.BlockSpec((1, SB_QTILE, SB_W), lambda bi, i: (bi, i, 0))],
        out_specs=pl.BlockSpec((1, SB_QTILE, SB_W), lambda bi, i: (bi, i, 0)),
        out_shape=jax.ShapeDtypeStruct((b, t, SB_W), BF16),
        scratch_shapes=[pltpu.VMEM((n_streams * SB_HALF, LANES), F32),
                        pltpu.VMEM((n_streams * SB_HALF, LANES), F32)],
        compiler_params=pltpu.CompilerParams(dimension_semantics=("arbitrary", "arbitrary"),
                                             vmem_limit_bytes=VMEM_LIMIT),
        name="sb",
    )(sb, sb, sb, c3)


def _out_kernel(x_ref, mixg_ref, mixs_ref, c3_ref, kvm_ref, wout_ref, lng_ref, lnb_ref, o_ref, *, tm):
    wout_ref = wout_ref.at[0]
    low = lax.broadcasted_iota(jnp.int32, (tm, LANES), 1) < MEM_DH
    pairs = range(MEM_HEADS // 2)
    s = {}
    for p in pairs:
        qp = c3_ref[0, :, SB_W + p * LANES:SB_W + (p + 1) * LANES]
        km = kvm_ref[0, 0, :, p * LANES:(p + 1) * LANES]
        zero = jnp.zeros_like(qp)
        s[p, 0] = _dot_nt(jnp.where(low, qp, zero), km)
        s[p, 1] = _dot_nt(jnp.where(low, zero, qp), km)
    y = _dot(mixg_ref[0], wout_ref[0:GLA_V_W, :])
    y += _dot(mixs_ref[0], wout_ref[GLA_V_W:GLA_V_W + SB_W, :])
    e, den = {}, {}
    for key, sc in s.items():
        ex = jnp.exp(sc - jnp.max(sc, axis=-1, keepdims=True))
        den[key] = jnp.sum(ex, axis=-1, keepdims=True)
        e[key] = ex.astype(BF16)
    for p in pairs:
        vm = kvm_ref[0, 0, :, MEM_W + p * LANES:MEM_W + (p + 1) * LANES]
        om = jnp.where(low, _dot(e[p, 0], vm) / den[p, 0], _dot(e[p, 1], vm) / den[p, 1])
        gate = c3_ref[0, :, SB_W + MEM_W + p * LANES:SB_W + MEM_W + (p + 1) * LANES].astype(F32)
        mixm = (om * _silu(gate)).astype(BF16)
        base = GLA_V_W + SB_W + p * LANES
        y += _dot(mixm, wout_ref[base:base + LANES, :])
    r = ALPHA * x_ref[0] + y
    mu = jnp.mean(r, axis=-1, keepdims=True)
    d = r - mu
    var = jnp.mean(d * d, axis=-1, keepdims=True)
    o_ref[0] = d * lax.rsqrt(var + LN_EPS) * lng_ref[0] + lnb_ref[0]


def _out(x, mixg, mixs, c3, kvm, layer, wout, lng, lnb, tm=512):
    b, t, _ = x.shape
    return pl.pallas_call(
        functools.partial(_out_kernel, tm=tm),
        grid=(b, t // tm),
        in_specs=[pl.BlockSpec((1, tm, D_MODEL), lambda i, j: (i, j, 0)),
                  pl.BlockSpec((1, tm, GLA_V_W), lambda i, j: (i, j, 0)),
                  pl.BlockSpec((1, tm, SB_W), lambda i, j: (i, j, 0)),
                  pl.BlockSpec((1, tm, C3_COLS), lambda i, j: (i, j, 0)),
                  pl.BlockSpec((1, 1, N_MEM, 2 * MEM_W), lambda i, j: (layer, i, 0, 0)),
                  pl.BlockSpec((1, D_MODEL, D_MODEL), lambda i, j: (layer, 0, 0)),
                  pl.BlockSpec((1, 1, D_MODEL), lambda i, j: (layer, 0, 0)),
                  pl.BlockSpec((1, 1, D_MODEL), lambda i, j: (layer, 0, 0))],
        out_specs=pl.BlockSpec((1, tm, D_MODEL), lambda i, j: (i, j, 0)),
        out_shape=jax.ShapeDtypeStruct((b, t, D_MODEL), F32),
        compiler_params=pltpu.CompilerParams(dimension_semantics=("arbitrary", "arbitrary"),
                                             vmem_limit_bytes=VMEM_LIMIT),
        name="out",
    )(x, mixg, mixs, c3, kvm, wout, lng, lnb)


def _regroup_w_in(w):
    w = w.astype(BF16)
    o = 0
    seg = {}
    for name, n in (("gq", GLA_QK_W), ("gk", GLA_QK_W), ("gv", GLA_V_W), ("gg", GLA_V_W), ("ga", GLA_RANK),
                    ("sq", SB_W), ("sk", SB_W), ("sv", SB_W), ("sg", SB_W), ("mq", MEM_W), ("mg", MEM_W)):
        seg[name] = w[..., o:o + n]
        o += n
    pad = jnp.zeros(w.shape[:-1] + (GA_COLS - GLA_RANK,), BF16)
    cols = [seg["gq"] * GLA_DK ** -0.5, seg["gk"], seg["gv"], seg["gg"],
            seg["sq"] * SB_DH ** -0.5, seg["sk"], seg["sv"],
            seg["sg"], seg["mq"] * MEM_DH ** -0.5, seg["mg"],
            seg["ga"], pad]
    return jnp.concatenate(cols, axis=-1)


def kernel(x, mem, w_in, w_alpha2, b_alpha, gla_norm_w, w_mem_kv, w_out, ln_g, ln_b):
    b, t, d = x.shape
    kvm = _memkv(mem.reshape(b * N_MEM, d), w_mem_kv.astype(BF16)).reshape(DEPTH, b, N_MEM, 2 * MEM_W)
    w_proj = _regroup_w_in(w_in)
    wa2 = jnp.pad(w_alpha2.astype(BF16), ((0, 0), (0, GA_COLS - GLA_RANK), (0, 0)))
    ba = b_alpha.reshape(DEPTH, 1, GLA_QK_W)
    nw = gla_norm_w.reshape(DEPTH, 1, GLA_DV)
    wout = w_out.astype(BF16)
    lng = ln_g.reshape(DEPTH, 1, d)
    lnb = ln_b.reshape(DEPTH, 1, d)
    for l in range(DEPTH):
        gla, sb, c3, ga = _proj(x.reshape(b * t, d), w_proj, l)
        mixg = _gla(gla.reshape(b, t, GLA_COLS), ga.reshape(b, t, GA_COLS), wa2, ba, nw, l)
        c3 = c3.reshape(b, t, C3_COLS)
        mixs = _sb(sb.reshape(b, t, SB_COLS), c3)
        x = _out(x, mixg, mixs, c3, kvm, l, wout, lng, lnb)
    return x
```

```python
import functools

import jax
import jax.numpy as jnp
from jax import lax
from jax.experimental import pallas as pl
from jax.experimental.pallas import tpu as pltpu

F32 = jnp.float32
BF16 = jnp.bfloat16

D_MODEL = 1024
DEPTH = 4
N_MEM = 256
GLA_HEADS = 4
GLA_DK = 64
GLA_DV = 128
GLA_RANK = 16
GLA_GATE_NORM = 16.0
GLA_CHUNK = 64
SB_HEADS = 4
SB_DH = 64
SB_BLOCK = 128
MEM_HEADS = 4
MEM_DH = 64
GLA_QK_W = GLA_HEADS * GLA_DK
GLA_V_W = GLA_HEADS * GLA_DV
SB_W = SB_HEADS * SB_DH
MEM_W = MEM_HEADS * MEM_DH
ALPHA = (2.0 * DEPTH) ** 0.25
LN_EPS = 1e-5
RMS_EPS = 1e-6

LANES = 128
GLA_COLS = 2 * GLA_QK_W + 2 * GLA_V_W
SB_COLS = 3 * SB_W
C3_COLS = SB_W + 2 * MEM_W
GA_COLS = LANES
PROJ_COLS = GLA_COLS + SB_COLS + C3_COLS + GA_COLS
VMEM_LIMIT = 56 * 1024 * 1024
SB_SKIP_LOG = -106.0
SB_HALF = 64
SB_LEAD_KEYS = 256
SB_QTILE = 512
LOG2E = 1.4426950408889634


def _dot(a, b):
    return jnp.dot(a, b, preferred_element_type=F32)


def _dot_nt(a, b):
    return lax.dot_general(a, b, (((1,), (1,)), ((), ())), preferred_element_type=F32)


def _dot_tn(a, b):
    return lax.dot_general(a, b, (((0,), (0,)), ((), ())), preferred_element_type=F32)


def _split_bf16(x, n):
    parts = []
    for _ in range(n - 1):
        h = x.astype(BF16)
        parts.append(h)
        x = x - h.astype(F32)
    parts.append(x.astype(BF16))
    return parts


def _log_sigmoid(z):
    return jnp.minimum(z, 0.0) - jnp.log(1.0 + jnp.exp2(jnp.abs(z) * -LOG2E))


def _silu(g):
    return g / (1.0 + jnp.exp(-g))


def _emit_pipelined(stages, items):
    for step in range(len(items) + len(stages) - 1):
        for depth, stage in enumerate(stages):
            j = step - depth
            if 0 <= j < len(items):
                stage(items[j])


def _proj_kernel(x_ref, w_ref, gla_ref, sb_ref, c3_ref, ga_ref):
    xb = x_ref[...].astype(BF16)
    w_ref = w_ref.at[0]
    off = 0
    for ref, width in ((gla_ref, GLA_COLS), (sb_ref, SB_COLS), (c3_ref, C3_COLS), (ga_ref, GA_COLS)):
        step = 512 if width % 512 == 0 else width if width < 512 else 256
        for c in range(0, width, step):
            ref[:, c:c + step] = _dot(xb, w_ref[:, off + c:off + c + step]).astype(BF16)
        off += width


def _proj(x2, w, layer, tm=512):
    n = x2.shape[0]
    outs = (GLA_COLS, SB_COLS, C3_COLS, GA_COLS)
    return pl.pallas_call(
        _proj_kernel,
        grid=(n // tm,),
        in_specs=[pl.BlockSpec((tm, D_MODEL), lambda i: (i, 0)),
                  pl.BlockSpec((1, D_MODEL, PROJ_COLS), lambda i: (layer, 0, 0))],
        out_specs=[pl.BlockSpec((tm, c), lambda i: (i, 0)) for c in outs],
        out_shape=[jax.ShapeDtypeStruct((n, c), BF16) for c in outs],
        compiler_params=pltpu.CompilerParams(dimension_semantics=("arbitrary",),
                                             vmem_limit_bytes=VMEM_LIMIT),
        name="proj",
    )(x2, w)


def _memkv_kernel(m_ref, w_ref, o_ref):
    o_ref[0] = _dot(m_ref[...].astype(BF16), w_ref[0]).astype(BF16)


def _memkv(mem2, w_mkv):
    n = mem2.shape[0]
    return pl.pallas_call(
        _memkv_kernel,
        grid=(DEPTH,),
        in_specs=[pl.BlockSpec((n, D_MODEL), lambda l: (0, 0)),
                  pl.BlockSpec((1, D_MODEL, 2 * MEM_W), lambda l: (l, 0, 0))],
        out_specs=pl.BlockSpec((1, n, 2 * MEM_W), lambda l: (l, 0, 0)),
        out_shape=jax.ShapeDtypeStruct((DEPTH, n, 2 * MEM_W), BF16),
        compiler_params=pltpu.CompilerParams(dimension_semantics=("arbitrary",),
                                             vmem_limit_bytes=VMEM_LIMIT),
        name="memkv",
    )(mem2, w_mkv)


def _gla_kernel(gla_ref, ga_ref, wa2_ref, ba_ref, nw_ref, out_ref, st_ref, *, tt):
    c_sz = GLA_CHUNK

    @pl.when(pl.program_id(1) == 0)
    def _():
        st_ref[...] = jnp.zeros_like(st_ref)

    row = lax.broadcasted_iota(jnp.int32, (c_sz, c_sz), 0)
    col = lax.broadcasted_iota(jnp.int32, (c_sz, c_sz), 1)
    causal = col <= row
    r2 = lax.broadcasted_iota(jnp.int32, (c_sz, 2 * c_sz), 0)
    c2 = lax.broadcasted_iota(jnp.int32, (c_sz, 2 * c_sz), 1) & (c_sz - 1)
    tri_incl2 = jnp.where(c2 <= r2, 1.0, 0.0).astype(BF16)
    low_c = lax.broadcasted_iota(jnp.int32, (c_sz, LANES), 1) < GLA_DK
    low_v = lax.broadcasted_iota(jnp.int32, (GLA_DV, LANES), 1) < GLA_DK
    chunks = range(tt // c_sz)
    pairs = range(GLA_HEADS // 2)
    rows = [slice(c * c_sz, (c + 1) * c_sz) for c in chunks]
    lanes = [slice(p * LANES, (p + 1) * LANES) for p in pairs]
    v_col = 2 * GLA_QK_W
    g_col = v_col + GLA_V_W

    zz = _dot(ga_ref[0], wa2_ref[0]) + ba_ref[0]
    log_a = _log_sigmoid(zz) * (1.0 / GLA_GATE_NORM)
    hi, lo = _split_bf16(log_a, 2)
    g_cum, kd, decay, q2, a2, upd, inter2 = {}, {}, {}, {}, {}, {}, {}
    state = {p: st_ref[p] for p in pairs}

    def v_head(c, h):
        return gla_ref[0, rows[c], v_col + h * GLA_DV:v_col + (h + 1) * GLA_DV]

    def cumsum(c):
        g_cum[c] = _dot(tri_incl2, jnp.concatenate([hi[rows[c]], lo[rows[c]]], axis=0))

    def scale(c):
        g_last = g_cum[c][c_sz - 1:c_sz, :]
        q = gla_ref[0, rows[c], 0:GLA_QK_W].astype(F32)
        k = gla_ref[0, rows[c], GLA_QK_W:2 * GLA_QK_W].astype(F32)
        qg = (q * jnp.exp(g_cum[c])).astype(BF16)
        kg = (k * jnp.exp(-g_cum[c])).astype(BF16)
        kd[c] = (k * jnp.exp(g_last - g_cum[c])).astype(BF16)
        decay[c] = jnp.exp(g_last)
        for p in pairs:
            qgp = qg[:, lanes[p]]
            q2[c, p] = jnp.concatenate([jnp.where(low_c, qgp, jnp.zeros_like(qgp)),
                                        jnp.where(low_c, jnp.zeros_like(qgp), qgp)], axis=0)
            a2[c, p] = _dot_nt(q2[c, p], kg[:, lanes[p]])

    def update(c):
        for p in pairs:
            for hh in range(2):
                upd[c, p, hh] = _dot_tn(v_head(c, 2 * p + hh), kd[c][:, lanes[p]])

    def recur(c):
        for p in pairs:
            inter2[c, p] = _dot_nt(q2[c, p], state[p].astype(BF16))
            state[p] = state[p] * decay[c][:, lanes[p]] + jnp.where(low_v, upd[c, p, 0], upd[c, p, 1])

    def finish(c):
        for p in pairs:
            for hh in range(2):
                h = 2 * p + hh
                a = jnp.where(causal, a2[c, p][hh * c_sz:(hh + 1) * c_sz], 0.0).astype(BF16)
                o = _dot(a, v_head(c, h)) + inter2[c, p][hh * c_sz:(hh + 1) * c_sz]
                ms = jnp.mean(o * o, axis=-1, keepdims=True)
                gate = gla_ref[0, rows[c], g_col + h * GLA_DV:g_col + (h + 1) * GLA_DV].astype(F32)
                res = o * lax.rsqrt(ms + RMS_EPS) * nw_ref[0] * _silu(gate)
                out_ref[0, rows[c], h * GLA_DV:(h + 1) * GLA_DV] = res.astype(BF16)

    _emit_pipelined((cumsum, scale, update, recur, finish), list(chunks))
    for p in pairs:
        st_ref[p] = state[p]


def _gla(gla, ga, wa2, ba, nw, layer, tt=1024):
    b, t, _ = gla.shape
    return pl.pallas_call(
        functools.partial(_gla_kernel, tt=tt),
        grid=(b, t // tt),
        in_specs=[pl.BlockSpec((1, tt, GLA_COLS), lambda i, j: (i, j, 0)),
                  pl.BlockSpec((1, tt, GA_COLS), lambda i, j: (i, j, 0)),
                  pl.BlockSpec((1, GA_COLS, GLA_QK_W), lambda i, j: (layer, 0, 0)),
                  pl.BlockSpec((1, 1, GLA_QK_W), lambda i, j: (layer, 0, 0)),
                  pl.BlockSpec((1, 1, GLA_DV), lambda i, j: (layer, 0, 0))],
        out_specs=pl.BlockSpec((1, tt, GLA_V_W), lambda i, j: (i, j, 0)),
        out_shape=jax.ShapeDtypeStruct((b, t, GLA_V_W), BF16),
        scratch_shapes=[pltpu.VMEM((GLA_HEADS // 2, GLA_DV, LANES), F32)],
        compiler_params=pltpu.CompilerParams(dimension_semantics=("arbitrary", "arbitrary"),
                                             vmem_limit_bytes=VMEM_LIMIT),
        name="gla",
    )(gla, ga, wa2, ba, nw)


def _sb_kernel(q_ref, k_ref, v_ref, g_ref, o_ref, oacc_ref, lacc_ref):
    hb, kb = SB_HALF, SB_BLOCK
    pairs = SB_HEADS // 2
    halves = SB_QTILE // hb
    q_base = pl.program_id(1) * SB_QTILE
    row = lax.broadcasted_iota(jnp.int32, (hb, kb), 0)
    lane = lax.broadcasted_iota(jnp.int32, (hb, kb), 1)
    low = lane < SB_DH
    rk = lax.broadcasted_iota(jnp.int32, (kb, 2 * kb), 0)
    ck = lax.broadcasted_iota(jnp.int32, (kb, 2 * kb), 1)
    tri_ones = jnp.where((rk > ck) | (ck >= kb), 1.0, 0.0).astype(BF16)

    def pair_rows(m, p):
        st = m * pairs + p
        return slice(st * 2 * hb, (st + 1) * 2 * hb)

    qneg = {}
    for m in range(halves):
        for p in range(pairs):
            qn = -q_ref[0, m * hb:(m + 1) * hb, p * LANES:(p + 1) * LANES]
            zero = jnp.zeros_like(qn)
            qneg[m, p] = jnp.concatenate([jnp.where(low, qn, zero), jnp.where(low, zero, qn)], axis=0)
    oacc_ref[...] = jnp.zeros_like(oacc_ref)
    lacc_ref[...] = jnp.zeros_like(lacc_ref)

    def group(start, n, valid):
        order = list(reversed(range(n)))
        rows = {m: pl.ds(pl.multiple_of(start[m], hb), n * kb) for m in start}
        log_beta, lf, sums = {}, {}, {}
        keep1 = jnp.where(valid, 1.0, 0.0).astype(BF16)
        keep = jnp.concatenate([keep1, keep1], axis=0)
        parts = [(p, g) for g in order for p in range(pairs)]

        def scores(m):
            s = {p: _dot_nt(qneg[m, p], k_ref[0, rows[m], p * LANES:(p + 1) * LANES])
                 for p in range(pairs)}
            for p, g in parts:
                sg = s[p][:, g * kb:(g + 1) * kb].astype(BF16)
                lsn = _log_sigmoid(sg)
                log_beta[m, p, g] = lsn - sg
                lf[m, p, g] = lsn * keep if g == n - 1 else lsn

        def suffix_sums(m):
            sums[m] = _dot(jnp.concatenate([lf[m, p, g] for p, g in parts], axis=0), tri_ones)

        def weights(m):
            acc = {p: lacc_ref[pair_rows(m, p)] for p in range(pairs)}
            ws = {p: [None] * n for p in range(pairs)}
            for idx, (p, g) in enumerate(parts):
                part = sums[m][idx * 2 * hb:(idx + 1) * 2 * hb]
                w = jnp.exp((part[:, :kb] + acc[p]).astype(BF16) + log_beta[m, p, g])
                ws[p][g] = w * keep if g == n - 1 else w
                acc[p] = acc[p] + part[:, kb:]
            for p in range(pairs):
                lacc_ref[pair_rows(m, p)] = acc[p]
                oacc_ref[pair_rows(m, p)] += _dot(jnp.concatenate(ws[p], axis=1),
                                                  v_ref[0, rows[m], p * LANES:(p + 1) * LANES])

        _emit_pipelined((scores, suffix_sums, weights), list(start))

    def sweep(m, bound):
        per_m = 2 * pairs * hb

        def live():
            return jnp.max(lacc_ref[m * per_m:(m + 1) * per_m])

        def cond(c):
            return jnp.logical_and(c[0] > 0, c[1] > SB_SKIP_LOG)

        def body(c):
            first = jnp.maximum(c[0] - kb, 0)
            group({m: first}, 1, lane < c[0] - first)
            return first, live()

        lax.while_loop(cond, body, (bound, live()))

    @pl.when(q_base >= SB_LEAD_KEYS - hb)
    def _():
        lead = {m: q_base + (m + 1) * hb - SB_LEAD_KEYS for m in range(halves)}
        group(lead, SB_LEAD_KEYS // kb, lane < row + (kb - hb))

        @pl.when(jnp.max(lacc_ref[...]) > SB_SKIP_LOG)
        def _():
            for m in range(halves):
                sweep(m, lead[m])

    @pl.when(q_base < SB_LEAD_KEYS - hb)
    def _():
        for m in range(halves):
            first = max(m * hb + hb - kb, 0)
            group({m: first}, 1, lane < row + (m * hb - first))
            sweep(m, first)

    for m in range(halves):
        for p in range(pairs):
            both = oacc_ref[pair_rows(m, p)]
            o = jnp.where(low, both[:hb], both[hb:])
            gate = g_ref[0, m * hb:(m + 1) * hb, p * LANES:(p + 1) * LANES].astype(F32)
            o_ref[0, m * hb:(m + 1) * hb, p * LANES:(p + 1) * LANES] = (o * _silu(gate)).astype(BF16)


def _sb(sb, c3):
    b, t, _ = sb.shape
    n_streams = (SB_QTILE // SB_HALF) * SB_HEADS
    return pl.pallas_call(
        _sb_kernel,
        grid=(b, t // SB_QTILE),
        in_specs=[pl.BlockSpec((1, SB_QTILE, SB_W), lambda bi, i: (bi, i, 0)),
                  pl.BlockSpec((1, t, SB_W), lambda bi, i: (bi, 0, 1)),
                  pl.BlockSpec((1, t, SB_W), lambda bi, i: (bi, 0, 2)),
                  pl.BlockSpec((1, SB_QTILE, SB_W), lambda bi, i: (bi, i, 0))],
        out_specs=pl.BlockSpec((1, SB_QTILE, SB_W), lambda bi, i: (bi, i, 0)),
        out_shape=jax.ShapeDtypeStruct((b, t, SB_W), BF16),
        scratch_shapes=[pltpu.VMEM((n_streams * SB_HALF, LANES), F32),
                        pltpu.VMEM((n_streams * SB_HALF, LANES), F32)],
        compiler_params=pltpu.CompilerParams(dimension_semantics=("arbitrary", "arbitrary"),
                                             vmem_limit_bytes=VMEM_LIMIT),
        name="sb",
    )(sb, sb, sb, c3)


def _out_kernel(x_ref, mixg_ref, mixs_ref, c3_ref, kvm_ref, wout_ref, lng_ref, lnb_ref, o_ref, *, tm):
    wout_ref = wout_ref.at[0]
    low = lax.broadcasted_iota(jnp.int32, (tm, LANES), 1) < MEM_DH
    pairs = range(MEM_HEADS // 2)
    s = {}
    for p in pairs:
        qp = c3_ref[0, :, SB_W + p * LANES:SB_W + (p + 1) * LANES]
        km = kvm_ref[0, 0, :, p * LANES:(p + 1) * LANES]
        zero = jnp.zeros_like(qp)
        s[p, 0] = _dot_nt(jnp.where(low, qp, zero), km)
        s[p, 1] = _dot_nt(jnp.where(low, zero, qp), km)
    y = _dot(mixg_ref[0], wout_ref[0:GLA_V_W, :])
    y += _dot(mixs_ref[0], wout_ref[GLA_V_W:GLA_V_W + SB_W, :])
    e, den = {}, {}
    for key, sc in s.items():
        ex = jnp.exp(sc - jnp.max(sc, axis=-1, keepdims=True))
        den[key] = jnp.sum(ex, axis=-1, keepdims=True)
        e[key] = ex.astype(BF16)
    for p in pairs:
        vm = kvm_ref[0, 0, :, MEM_W + p * LANES:MEM_W + (p + 1) * LANES]
        om = jnp.where(low, _dot(e[p, 0], vm) / den[p, 0], _dot(e[p, 1], vm) / den[p, 1])
        gate = c3_ref[0, :, SB_W + MEM_W + p * LANES:SB_W + MEM_W + (p + 1) * LANES].astype(F32)
        mixm = (om * _silu(gate)).astype(BF16)
        base = GLA_V_W + SB_W + p * LANES
        y += _dot(mixm, wout_ref[base:base + LANES, :])
    r = ALPHA * x_ref[0] + y
    mu = jnp.mean(r, axis=-1, keepdims=True)
    d = r - mu
    var = jnp.mean(d * d, axis=-1, keepdims=True)
    o_ref[0] = d * lax.rsqrt(var + LN_EPS) * lng_ref[0] + lnb_ref[0]


def _out(x, mixg, mixs, c3, kvm, layer, wout, lng, lnb, tm=512):
    b, t, _ = x.shape
    return pl.pallas_call(
        functools.partial(_out_kernel, tm=tm),
        grid=(b, t // tm),
        in_specs=[pl.BlockSpec((1, tm, D_MODEL), lambda i, j: (i, j, 0)),
                  pl.BlockSpec((1, tm, GLA_V_W), lambda i, j: (i, j, 0)),
                  pl.BlockSpec((1, tm, SB_W), lambda i, j: (i, j, 0)),
                  pl.BlockSpec((1, tm, C3_COLS), lambda i, j: (i, j, 0)),
                  pl.BlockSpec((1, 1, N_MEM, 2 * MEM_W), lambda i, j: (layer, i, 0, 0)),
                  pl.BlockSpec((1, D_MODEL, D_MODEL), lambda i, j: (layer, 0, 0)),
                  pl.BlockSpec((1, 1, D_MODEL), lambda i, j: (layer, 0, 0)),
                  pl.BlockSpec((1, 1, D_MODEL), lambda i, j: (layer, 0, 0))],
        out_specs=pl.BlockSpec((1, tm, D_MODEL), lambda i, j: (i, j, 0)),
        out_shape=jax.ShapeDtypeStruct((b, t, D_MODEL), F32),
        compiler_params=pltpu.CompilerParams(dimension_semantics=("arbitrary", "arbitrary"),
                                             vmem_limit_bytes=VMEM_LIMIT),
        name="out",
    )(x, mixg, mixs, c3, kvm, wout, lng, lnb)


def _regroup_w_in(w):
    w = w.astype(BF16)
    o = 0
    seg = {}
    for name, n in (("gq", GLA_QK_W), ("gk", GLA_QK_W), ("gv", GLA_V_W), ("gg", GLA_V_W), ("ga", GLA_RANK),
                    ("sq", SB_W), ("sk", SB_W), ("sv", SB_W), ("sg", SB_W), ("mq", MEM_W), ("mg", MEM_W)):
        seg[name] = w[..., o:o + n]
        o += n
    pad = jnp.zeros(w.shape[:-1] + (GA_COLS - GLA_RANK,), BF16)
    cols = [seg["gq"] * GLA_DK ** -0.5, seg["gk"], seg["gv"], seg["gg"],
            seg["sq"] * SB_DH ** -0.5, seg["sk"], seg["sv"],
            seg["sg"], seg["mq"] * MEM_DH ** -0.5, seg["mg"],
            seg["ga"], pad]
    return jnp.concatenate(cols, axis=-1)


def kernel(x, mem, w_in, w_alpha2, b_alpha, gla_norm_w, w_mem_kv, w_out, ln_g, ln_b):
    b, t, d = x.shape
    kvm = _memkv(mem.reshape(b * N_MEM, d), w_mem_kv.astype(BF16)).reshape(DEPTH, b, N_MEM, 2 * MEM_W)
    w_proj = _regroup_w_in(w_in)
    wa2 = jnp.pad(w_alpha2.astype(BF16), ((0, 0), (0, GA_COLS - GLA_RANK), (0, 0)))
    ba = b_alpha.reshape(DEPTH, 1, GLA_QK_W)
    nw = gla_norm_w.reshape(DEPTH, 1, GLA_DV)
    wout = w_out.astype(BF16)
    lng = ln_g.reshape(DEPTH, 1, d)
    lnb = ln_b.reshape(DEPTH, 1, d)
    for l in range(DEPTH):
        gla, sb, c3, ga = _proj(x.reshape(b * t, d), w_proj, l)
        mixg = _gla(gla.reshape(b, t, GLA_COLS), ga.reshape(b, t, GA_COLS), wa2, ba, nw, l)
        c3 = c3.reshape(b, t, C3_COLS)
        mixs = _sb(sb.reshape(b, t, SB_COLS), c3)
        x = _out(x, mixg, mixs, c3, kvm, l, wout, lng, lnb)
    return x
```

```python
import functools

import jax
import jax.numpy as jnp
from jax import lax
from jax.experimental import pallas as pl
from jax.experimental.pallas import tpu as pltpu

F32 = jnp.float32
BF16 = jnp.bfloat16

D_MODEL = 1024
DEPTH = 4
N_MEM = 256
GLA_HEADS = 4
GLA_DK = 64
GLA_DV = 128
GLA_RANK = 16
GLA_GATE_NORM = 16.0
GLA_CHUNK = 64
SB_HEADS = 4
SB_DH = 64
SB_BLOCK = 128
MEM_HEADS = 4
MEM_DH = 64
GLA_QK_W = GLA_HEADS * GLA_DK
GLA_V_W = GLA_HEADS * GLA_DV
SB_W = SB_HEADS * SB_DH
MEM_W = MEM_HEADS * MEM_DH
ALPHA = (2.0 * DEPTH) ** 0.25
LN_EPS = 1e-5
RMS_EPS = 1e-6

LANES = 128
GLA_COLS = 2 * GLA_QK_W + 2 * GLA_V_W
SB_COLS = 3 * SB_W
C3_COLS = SB_W + 2 * MEM_W
GA_COLS = LANES
PROJ_COLS = GLA_COLS + SB_COLS + C3_COLS + GA_COLS
VMEM_LIMIT = 56 * 1024 * 1024
SB_SKIP_LOG = -106.0
SB_HALF = 64
SB_LEAD_KEYS = 2 * SB_BLOCK
SB_QTILE = 512
GLA_TILE = 1024
ROW_TILE = 512
OUT_TILE = 1024
LOG2E = 1.4426950408889634


def _dot(a, b):
    return jnp.dot(a, b, preferred_element_type=F32)


def _dot_nt(a, b):
    return lax.dot_general(a, b, (((1,), (1,)), ((), ())), preferred_element_type=F32)


def _dot_tn(a, b):
    return lax.dot_general(a, b, (((0,), (0,)), ((), ())), preferred_element_type=F32)


def _split_bf16(x, n):
    parts = []
    for _ in range(n - 1):
        h = x.astype(BF16)
        parts.append(h)
        x = x - h.astype(F32)
    parts.append(x.astype(BF16))
    return parts


def _log_sigmoid(z):
    return jnp.minimum(z, 0.0) - jnp.log(1.0 + jnp.exp2(jnp.abs(z) * -LOG2E))


def _silu(g):
    return g / (1.0 + jnp.exp(-g))


def _emit_pipelined(stages, items):
    for step in range(len(items) + len(stages) - 1):
        for depth, stage in enumerate(stages):
            j = step - depth
            if 0 <= j < len(items):
                stage(items[j])


def _proj_kernel(x_ref, w_ref, gla_ref, sb_ref, c3_ref, ga_ref):
    xb = x_ref[...].astype(BF16)
    w_ref = w_ref.at[0]
    off = 0
    for ref, width in ((gla_ref, GLA_COLS), (sb_ref, SB_COLS), (c3_ref, C3_COLS), (ga_ref, GA_COLS)):
        step = 512 if width % 512 == 0 else width if width < 512 else 256
        for c in range(0, width, step):
            ref[:, c:c + step] = _dot(xb, w_ref[:, off + c:off + c + step]).astype(BF16)
        off += width


def _proj(x2, w, layer, tm=ROW_TILE):
    n = x2.shape[0]
    outs = (GLA_COLS, SB_COLS, C3_COLS, GA_COLS)
    return pl.pallas_call(
        _proj_kernel,
        grid=(n // tm,),
        in_specs=[pl.BlockSpec((tm, D_MODEL), lambda i: (i, 0)),
                  pl.BlockSpec((1, D_MODEL, PROJ_COLS), lambda i: (layer, 0, 0))],
        out_specs=[pl.BlockSpec((tm, c), lambda i: (i, 0)) for c in outs],
        out_shape=[jax.ShapeDtypeStruct((n, c), BF16) for c in outs],
        compiler_params=pltpu.CompilerParams(dimension_semantics=("arbitrary",),
                                             vmem_limit_bytes=VMEM_LIMIT),
        name="proj",
    )(x2, w)


def _memkv_kernel(m_ref, w_ref, o_ref):
    o_ref[0] = _dot(m_ref[...].astype(BF16), w_ref[0]).astype(BF16)


def _memkv(mem2, w_mkv):
    n = mem2.shape[0]
    return pl.pallas_call(
        _memkv_kernel,
        grid=(DEPTH,),
        in_specs=[pl.BlockSpec((n, D_MODEL), lambda l: (0, 0)),
                  pl.BlockSpec((1, D_MODEL, 2 * MEM_W), lambda l: (l, 0, 0))],
        out_specs=pl.BlockSpec((1, n, 2 * MEM_W), lambda l: (l, 0, 0)),
        out_shape=jax.ShapeDtypeStruct((DEPTH, n, 2 * MEM_W), BF16),
        compiler_params=pltpu.CompilerParams(dimension_semantics=("arbitrary",),
                                             vmem_limit_bytes=VMEM_LIMIT),
        name="memkv",
    )(mem2, w_mkv)


def _gla_kernel(gla_ref, ga_ref, wa2_ref, ba_ref, nw_ref, out_ref, st_ref, *, tt):
    c_sz = GLA_CHUNK

    @pl.when(pl.program_id(1) == 0)
    def _():
        st_ref[...] = jnp.zeros_like(st_ref)

    row = lax.broadcasted_iota(jnp.int32, (c_sz, c_sz), 0)
    col = lax.broadcasted_iota(jnp.int32, (c_sz, c_sz), 1)
    causal = col <= row
    r2 = lax.broadcasted_iota(jnp.int32, (c_sz, 2 * c_sz), 0)
    c2 = lax.broadcasted_iota(jnp.int32, (c_sz, 2 * c_sz), 1) & (c_sz - 1)
    tri_incl2 = jnp.where(c2 <= r2, 1.0, 0.0).astype(BF16)
    low_c = lax.broadcasted_iota(jnp.int32, (c_sz, LANES), 1) < GLA_DK
    low_v = lax.broadcasted_iota(jnp.int32, (GLA_DV, LANES), 1) < GLA_DK
    chunks = range(tt // c_sz)
    pairs = range(GLA_HEADS // 2)
    rows = [slice(c * c_sz, (c + 1) * c_sz) for c in chunks]
    lanes = [slice(p * LANES, (p + 1) * LANES) for p in pairs]
    v_col = 2 * GLA_QK_W
    g_col = v_col + GLA_V_W

    zz = _dot(ga_ref[0], wa2_ref[0]) + ba_ref[0]
    log_a = _log_sigmoid(zz) * (1.0 / GLA_GATE_NORM)
    hi, lo = _split_bf16(log_a, 2)
    g_cum, kd, decay, q2, a2, upd, inter2 = {}, {}, {}, {}, {}, {}, {}
    state = {p: st_ref[p] for p in pairs}

    def v_head(c, h):
        return gla_ref[0, rows[c], v_col + h * GLA_DV:v_col + (h + 1) * GLA_DV]

    def cumsum(c):
        g_cum[c] = _dot(tri_incl2, jnp.concatenate([hi[rows[c]], lo[rows[c]]], axis=0))

    def scale(c):
        g_last = g_cum[c][c_sz - 1:c_sz, :]
        q = gla_ref[0, rows[c], 0:GLA_QK_W].astype(F32)
        k = gla_ref[0, rows[c], GLA_QK_W:2 * GLA_QK_W].astype(F32)
        qg = (q * jnp.exp(g_cum[c])).astype(BF16)
        kg = (k * jnp.exp(-g_cum[c])).astype(BF16)
        kd[c] = (k * jnp.exp(g_last - g_cum[c])).astype(BF16)
        decay[c] = jnp.exp(g_last)
        for p in pairs:
            qgp = qg[:, lanes[p]]
            q2[c, p] = jnp.concatenate([jnp.where(low_c, qgp, jnp.zeros_like(qgp)),
                                        jnp.where(low_c, jnp.zeros_like(qgp), qgp)], axis=0)
            a2[c, p] = _dot_nt(q2[c, p], kg[:, lanes[p]])

    def update(c):
        for p in pairs:
            for hh in range(2):
                upd[c, p, hh] = _dot_tn(v_head(c, 2 * p + hh), kd[c][:, lanes[p]])

    def recur(c):
        for p in pairs:
            inter2[c, p] = _dot_nt(q2[c, p], state[p].astype(BF16))
            state[p] = state[p] * decay[c][:, lanes[p]] + jnp.where(low_v, upd[c, p, 0], upd[c, p, 1])

    def finish(c):
        for p in pairs:
            for hh in range(2):
                h = 2 * p + hh
                a = jnp.where(causal, a2[c, p][hh * c_sz:(hh + 1) * c_sz], 0.0).astype(BF16)
                o = _dot(a, v_head(c, h)) + inter2[c, p][hh * c_sz:(hh + 1) * c_sz]
                ms = jnp.mean(o * o, axis=-1, keepdims=True)
                gate = gla_ref[0, rows[c], g_col + h * GLA_DV:g_col + (h + 1) * GLA_DV].astype(F32)
                res = o * lax.rsqrt(ms + RMS_EPS) * nw_ref[0] * _silu(gate)
                out_ref[0, rows[c], h * GLA_DV:(h + 1) * GLA_DV] = res.astype(BF16)

    _emit_pipelined((cumsum, scale, update, recur, finish), list(chunks))
    for p in pairs:
        st_ref[p] = state[p]


def _gla(gla, ga, wa2, ba, nw, layer, tt=GLA_TILE):
    b, t, _ = gla.shape
    return pl.pallas_call(
        functools.partial(_gla_kernel, tt=tt),
        grid=(b, t // tt),
        in_specs=[pl.BlockSpec((1, tt, GLA_COLS), lambda i, j: (i, j, 0)),
                  pl.BlockSpec((1, tt, GA_COLS), lambda i, j: (i, j, 0)),
                  pl.BlockSpec((1, GA_COLS, GLA_QK_W), lambda i, j: (layer, 0, 0)),
                  pl.BlockSpec((1, 1, GLA_QK_W), lambda i, j: (layer, 0, 0)),
                  pl.BlockSpec((1, 1, GLA_DV), lambda i, j: (layer, 0, 0))],
        out_specs=pl.BlockSpec((1, tt, GLA_V_W), lambda i, j: (i, j, 0)),
        out_shape=jax.ShapeDtypeStruct((b, t, GLA_V_W), BF16),
        scratch_shapes=[pltpu.VMEM((GLA_HEADS // 2, GLA_DV, LANES), F32)],
        compiler_params=pltpu.CompilerParams(dimension_semantics=("arbitrary", "arbitrary"),
                                             vmem_limit_bytes=VMEM_LIMIT),
        name="gla",
    )(gla, ga, wa2, ba, nw)


def _sb_kernel(q_ref, k_ref, v_ref, g_ref, o_ref, oacc_ref, lacc_ref):
    hb, kb = SB_HALF, SB_BLOCK
    pairs = SB_HEADS // 2
    halves = SB_QTILE // hb
    q_base = pl.program_id(1) * SB_QTILE
    row = lax.broadcasted_iota(jnp.int32, (hb, kb), 0)
    lane = lax.broadcasted_iota(jnp.int32, (hb, kb), 1)
    low = lane < SB_DH
    rk = lax.broadcasted_iota(jnp.int32, (kb, 2 * kb), 0)
    ck = lax.broadcasted_iota(jnp.int32, (kb, 2 * kb), 1)
    tri_ones = jnp.where((rk > ck) | (ck >= kb), 1.0, 0.0).astype(BF16)

    def pair_rows(m, p):
        st = m * pairs + p
        return slice(st * 2 * hb, (st + 1) * 2 * hb)

    qneg = {}
    for m in range(halves):
        for p in range(pairs):
            qn = -q_ref[0, m * hb:(m + 1) * hb, p * LANES:(p + 1) * LANES]
            zero = jnp.zeros_like(qn)
            qneg[m, p] = jnp.concatenate([jnp.where(low, qn, zero), jnp.where(low, zero, qn)], axis=0)
    def lead_group(start, valid):
        rows = {m: pl.ds(pl.multiple_of(start[m], hb), 2 * kb) for m in start}
        r4 = lax.broadcasted_iota(jnp.int32, (2 * kb, 2 * kb), 0)
        c4 = lax.broadcasted_iota(jnp.int32, (2 * kb, 2 * kb), 1)
        later = ((r4 > c4) & ((r4 >= kb) == (c4 >= kb))) | ((r4 < kb) & (c4 >= kb))
        tri_pair = jnp.where(later, 1.0, 0.0).astype(BF16)
        keep1 = jnp.where(valid, 1.0, 0.0).astype(BF16)
        keep = jnp.concatenate([keep1, keep1], axis=0)
        log_beta, lf, sums = {}, {}, {}

        def scores(m):
            for p in range(pairs):
                s = _dot_nt(qneg[m, p], k_ref[0, rows[m], p * LANES:(p + 1) * LANES])
                for g in (1, 0):
                    sg = s[:, g * kb:(g + 1) * kb].astype(BF16)
                    lsn = _log_sigmoid(sg)
                    log_beta[m, p, g] = lsn - sg
                    lf[m, p, g] = lsn * keep if g == 1 else lsn

        def suffix_sums(m):
            lhs = [jnp.concatenate([lf[m, p, 1], lf[m, p, 0]], axis=1) for p in range(pairs)]
            sums[m] = _dot(jnp.concatenate(lhs, axis=0), tri_pair)

        def weights(m):
            for p in range(pairs):
                part = sums[m][p * 2 * hb:(p + 1) * 2 * hb]
                w1 = jnp.exp(part[:, :kb].astype(BF16) + log_beta[m, p, 1]) * keep
                w0 = jnp.exp(part[:, kb:].astype(BF16) + log_beta[m, p, 0])
                total = part[:, kb:kb + 1] + lf[m, p, 0][:, 0:1].astype(F32)
                lacc_ref[pair_rows(m, p)] = jnp.broadcast_to(total, (2 * hb, kb))
                oacc_ref[pair_rows(m, p)] = _dot(jnp.concatenate([w0, w1], axis=1),
                                                 v_ref[0, rows[m], p * LANES:(p + 1) * LANES])

        _emit_pipelined((scores, suffix_sums, weights), list(start))

    def group(start, n, valid):
        order = list(reversed(range(n)))
        rows = {m: pl.ds(pl.multiple_of(start[m], hb), n * kb) for m in start}
        log_beta, lf, sums = {}, {}, {}
        keep1 = jnp.where(valid, 1.0, 0.0).astype(BF16)
        keep = jnp.concatenate([keep1, keep1], axis=0)
        parts = [(p, g) for g in order for p in range(pairs)]

        def scores(m):
            s = {p: _dot_nt(qneg[m, p], k_ref[0, rows[m], p * LANES:(p + 1) * LANES])
                 for p in range(pairs)}
            for p, g in parts:
                sg = s[p][:, g * kb:(g + 1) * kb].astype(BF16)
                lsn = _log_sigmoid(sg)
                log_beta[m, p, g] = lsn - sg
                lf[m, p, g] = lsn * keep if g == n - 1 else lsn

        def suffix_sums(m):
            sums[m] = _dot(jnp.concatenate([lf[m, p, g] for p, g in parts], axis=0), tri_ones)

        def weights(m):
            acc = {p: lacc_ref[pair_rows(m, p)] for p in range(pairs)}
            ws = {p: [None] * n for p in range(pairs)}
            for idx, (p, g) in enumerate(parts):
                part = sums[m][idx * 2 * hb:(idx + 1) * 2 * hb]
                w = jnp.exp((part[:, :kb] + acc[p]).astype(BF16) + log_beta[m, p, g])
                ws[p][g] = w * keep if g == n - 1 else w
                acc[p] = acc[p] + part[:, kb:]
            for p in range(pairs):
                lacc_ref[pair_rows(m, p)] = acc[p]
                oacc_ref[pair_rows(m, p)] += _dot(jnp.concatenate(ws[p], axis=1),
                                                  v_ref[0, rows[m], p * LANES:(p + 1) * LANES])

        _emit_pipelined((scores, suffix_sums, weights), list(start))

    def sweep(m, bound):
        per_m = 2 * pairs * hb

        def live():
            return jnp.max(lacc_ref[m * per_m:(m + 1) * per_m])

        def cond(c):
            return jnp.logical_and(c[0] > 0, c[1] > SB_SKIP_LOG)

        def body(c):
            first = jnp.maximum(c[0] - kb, 0)
            group({m: first}, 1, lane < c[0] - first)
            return first, live()

        lax.while_loop(cond, body, (bound, live()))

    @pl.when(q_base >= SB_LEAD_KEYS - hb)
    def _():
        lead = {m: q_base + (m + 1) * hb - SB_LEAD_KEYS for m in range(halves)}
        lead_group(lead, lane < row + (kb - hb))

        @pl.when(jnp.max(lacc_ref[...]) > SB_SKIP_LOG)
        def _():
            for m in range(halves):
                sweep(m, lead[m])

    @pl.when(q_base < SB_LEAD_KEYS - hb)
    def _():
        oacc_ref[...] = jnp.zeros_like(oacc_ref)
        lacc_ref[...] = jnp.zeros_like(lacc_ref)
        for m in range(halves):
            first = max(m * hb + hb - kb, 0)
            group({m: first}, 1, lane < row + (m * hb - first))
            sweep(m, first)

    for m in range(halves):
        for p in range(pairs):
            both = oacc_ref[pair_rows(m, p)]
            o = jnp.where(low, both[:hb], both[hb:])
            gate = g_ref[0, m * hb:(m + 1) * hb, p * LANES:(p + 1) * LANES].astype(F32)
            o_ref[0, m * hb:(m + 1) * hb, p * LANES:(p + 1) * LANES] = (o * _silu(gate)).astype(BF16)


def _sb(sb, c3):
    b, t, _ = sb.shape
    n_streams = (SB_QTILE // SB_HALF) * SB_HEADS
    return pl.pallas_call(
        _sb_kernel,
        grid=(b, t // SB_QTILE),
        in_specs=[pl.BlockSpec((1, SB_QTILE, SB_W), lambda bi, i: (bi, i, 0)),
                  pl.BlockSpec((1, t, SB_W), lambda bi, i: (bi, 0, 1)),
                  pl.BlockSpec((1, t, SB_W), lambda bi, i: (bi, 0, 2)),
                  pl.BlockSpec((1, SB_QTILE, SB_W), lambda bi, i: (bi, i, 0))],
        out_specs=pl.BlockSpec((1, SB_QTILE, SB_W), lambda bi, i: (bi, i, 0)),
        out_shape=jax.ShapeDtypeStruct((b, t, SB_W), BF16),
        scratch_shapes=[pltpu.VMEM((n_streams * SB_HALF, LANES), F32),
                        pltpu.VMEM((n_streams * SB_HALF, LANES), F32)],
        compiler_params=pltpu.CompilerParams(dimension_semantics=("arbitrary", "arbitrary"),
                                             vmem_limit_bytes=VMEM_LIMIT),
        name="sb",
    )(sb, sb, sb, c3)


def _out_kernel(x_ref, mixg_ref, mixs_ref, c3_ref, kvm_ref, wout_ref, lng_ref, lnb_ref, o_ref, *, tm, sub):
    wout_ref = wout_ref.at[0]
    low = lax.broadcasted_iota(jnp.int32, (sub, LANES), 1) < MEM_DH
    pairs = range(MEM_HEADS // 2)
    s, e, den, y = {}, {}, {}, {}

    def rows(j):
        return slice(j * sub, (j + 1) * sub)

    def scores(j):
        for p in pairs:
            qp = c3_ref[0, rows(j), SB_W + p * LANES:SB_W + (p + 1) * LANES]
            km = kvm_ref[0, 0, :, p * LANES:(p + 1) * LANES]
            zero = jnp.zeros_like(qp)
            s[j, p, 0] = _dot_nt(jnp.where(low, qp, zero), km)
            s[j, p, 1] = _dot_nt(jnp.where(low, zero, qp), km)

    def softmax(j):
        y[j] = (_dot(mixg_ref[0, rows(j), :], wout_ref[0:GLA_V_W, :])
                + _dot(mixs_ref[0, rows(j), :], wout_ref[GLA_V_W:GLA_V_W + SB_W, :]))
        for p in pairs:
            for hh in range(2):
                sc = s[j, p, hh]
                ex = jnp.exp(sc - jnp.max(sc, axis=-1, keepdims=True))
                den[j, p, hh] = jnp.sum(ex, axis=-1, keepdims=True)
                e[j, p, hh] = ex.astype(BF16)

    def mix(j):
        for p in pairs:
            vm = kvm_ref[0, 0, :, MEM_W + p * LANES:MEM_W + (p + 1) * LANES]
            om = jnp.where(low, _dot(e[j, p, 0], vm) / den[j, p, 0], _dot(e[j, p, 1], vm) / den[j, p, 1])
            gate = c3_ref[0, rows(j), SB_W + MEM_W + p * LANES:SB_W + MEM_W + (p + 1) * LANES].astype(F32)
            mixm = (om * _silu(gate)).astype(BF16)
            base = GLA_V_W + SB_W + p * LANES
            y[j] = y[j] + _dot(mixm, wout_ref[base:base + LANES, :])

    def norm(j):
        r = ALPHA * x_ref[0, rows(j), :] + y[j]
        mu = jnp.mean(r, axis=-1, keepdims=True)
        d = r - mu
        var = jnp.mean(d * d, axis=-1, keepdims=True)
        o_ref[0, rows(j), :] = d * lax.rsqrt(var + LN_EPS) * lng_ref[0] + lnb_ref[0]

    _emit_pipelined((scores, softmax, mix, norm), list(range(tm // sub)))


def _out(x, mixg, mixs, c3, kvm, layer, wout, lng, lnb, tm=OUT_TILE, sub=ROW_TILE):
    b, t, _ = x.shape
    return pl.pallas_call(
        functools.partial(_out_kernel, tm=tm, sub=sub),
        grid=(b, t // tm),
        in_specs=[pl.BlockSpec((1, tm, D_MODEL), lambda i, j: (i, j, 0)),
                  pl.BlockSpec((1, tm, GLA_V_W), lambda i, j: (i, j, 0)),
                  pl.BlockSpec((1, tm, SB_W), lambda i, j: (i, j, 0)),
                  pl.BlockSpec((1, tm, C3_COLS), lambda i, j: (i, j, 0)),
                  pl.BlockSpec((1, 1, N_MEM, 2 * MEM_W), lambda i, j: (layer, i, 0, 0)),
                  pl.BlockSpec((1, D_MODEL, D_MODEL), lambda i, j: (layer, 0, 0)),
                  pl.BlockSpec((1, 1, D_MODEL), lambda i, j: (layer, 0, 0)),
                  pl.BlockSpec((1, 1, D_MODEL), lambda i, j: (layer, 0, 0))],
        out_specs=pl.BlockSpec((1, tm, D_MODEL), lambda i, j: (i, j, 0)),
        out_shape=jax.ShapeDtypeStruct((b, t, D_MODEL), F32),
        compiler_params=pltpu.CompilerParams(dimension_semantics=("arbitrary", "arbitrary"),
                                             vmem_limit_bytes=VMEM_LIMIT),
        name="out",
    )(x, mixg, mixs, c3, kvm, wout, lng, lnb)


_W_IN_SEGMENTS = (("gq", GLA_QK_W), ("gk", GLA_QK_W), ("gv", GLA_V_W), ("gg", GLA_V_W), ("ga", GLA_RANK),
                  ("sq", SB_W), ("sk", SB_W), ("sv", SB_W), ("sg", SB_W), ("mq", MEM_W), ("mg", MEM_W))
_W_PROJ_ORDER = ("gq", "gk", "gv", "gg", "sq", "sk", "sv", "sg", "mq", "mg", "ga")
_Q_SCALED = {"gq": GLA_DK ** -0.5, "sq": SB_DH ** -0.5, "mq": MEM_DH ** -0.5}
D_IN = sum(n for _, n in _W_IN_SEGMENTS)


def _wprep_kernel(w_ref, o_ref):
    src, off = {}, 0
    for name, n in _W_IN_SEGMENTS:
        src[name] = (off, n)
        off += n
    dst = 0
    for name in _W_PROJ_ORDER:
        lo, n = src[name]
        o_ref[0, :, dst:dst + n] = (w_ref[0, :, lo:lo + n] * _Q_SCALED.get(name, 1.0)).astype(BF16)
        dst += n
    o_ref[0, :, dst:PROJ_COLS] = jnp.zeros((o_ref.shape[1], PROJ_COLS - dst), BF16)


def _regroup_w_in(w, tr=256):
    depth, d, _ = w.shape
    return pl.pallas_call(
        _wprep_kernel,
        grid=(depth, d // tr),
        in_specs=[pl.BlockSpec((1, tr, D_IN), lambda l, i: (l, i, 0))],
        out_specs=pl.BlockSpec((1, tr, PROJ_COLS), lambda l, i: (l, i, 0)),
        out_shape=jax.ShapeDtypeStruct((depth, d, PROJ_COLS), BF16),
        compiler_params=pltpu.CompilerParams(dimension_semantics=("arbitrary", "arbitrary"),
                                             vmem_limit_bytes=VMEM_LIMIT),
        name="wprep",
    )(w)


def kernel(x, mem, w_in, w_alpha2, b_alpha, gla_norm_w, w_mem_kv, w_out, ln_g, ln_b):
    b, t, d = x.shape
    assert d == D_MODEL and mem.shape == (b, N_MEM, d) and w_in.shape == (DEPTH, d, D_IN)
    assert t % GLA_TILE == 0 and t % SB_QTILE == 0 and t % OUT_TILE == 0
    kvm = _memkv(mem.reshape(b * N_MEM, d), w_mem_kv.astype(BF16)).reshape(DEPTH, b, N_MEM, 2 * MEM_W)
    w_proj = _regroup_w_in(w_in)
    wa2 = jnp.pad(w_alpha2.astype(BF16), ((0, 0), (0, GA_COLS - GLA_RANK), (0, 0)))
    ba = b_alpha.reshape(DEPTH, 1, GLA_QK_W)
    nw = gla_norm_w.reshape(DEPTH, 1, GLA_DV)
    wout = w_out.astype(BF16)
    lng = ln_g.reshape(DEPTH, 1, d)
    lnb = ln_b.reshape(DEPTH, 1, d)
    for l in range(DEPTH):
        gla, sb, c3, ga = _proj(x.reshape(b * t, d), w_proj, l)
        mixg = _gla(gla.reshape(b, t, GLA_COLS), ga.reshape(b, t, GA_COLS), wa2, ba, nw, l)
        c3 = c3.reshape(b, t, C3_COLS)
        mixs = _sb(sb.reshape(b, t, SB_COLS), c3)
        x = _out(x, mixg, mixs, c3, kvm, l, wout, lng, lnb)
    return x
```

```python
import functools

import jax
import jax.numpy as jnp
from jax import lax
from jax.experimental import pallas as pl
from jax.experimental.pallas import tpu as pltpu

F32 = jnp.float32
BF16 = jnp.bfloat16

D_MODEL = 1024
DEPTH = 4
N_MEM = 256
GLA_HEADS = 4
GLA_DK = 64
GLA_DV = 128
GLA_RANK = 16
GLA_GATE_NORM = 16.0
GLA_CHUNK = 64
SB_HEADS = 4
SB_DH = 64
SB_BLOCK = 128
MEM_HEADS = 4
MEM_DH = 64
GLA_QK_W = GLA_HEADS * GLA_DK
GLA_V_W = GLA_HEADS * GLA_DV
SB_W = SB_HEADS * SB_DH
MEM_W = MEM_HEADS * MEM_DH
ALPHA = (2.0 * DEPTH) ** 0.25
LN_EPS = 1e-5
RMS_EPS = 1e-6

LANES = 128
GLA_COLS = 2 * GLA_QK_W + 2 * GLA_V_W
SB_COLS = 3 * SB_W
C3_COLS = SB_W + 2 * MEM_W
GA_COLS = LANES
PROJ_COLS = GLA_COLS + SB_COLS + C3_COLS + GA_COLS
VMEM_LIMIT = 56 * 1024 * 1024
SB_SKIP_LOG = -106.0
SB_HALF = 64
SB_LEAD_KEYS = 2 * SB_BLOCK
SB_QTILE = 512
GLA_TILE = 1024
ROW_TILE = 512
OUT_TILE = 1024
LOG2E = 1.4426950408889634


def _dot(a, b):
    return jnp.dot(a, b, preferred_element_type=F32)


def _dot_nt(a, b):
    return lax.dot_general(a, b, (((1,), (1,)), ((), ())), preferred_element_type=F32)


def _dot_tn(a, b):
    return lax.dot_general(a, b, (((0,), (0,)), ((), ())), preferred_element_type=F32)


def _split_bf16(x, n):
    parts = []
    for _ in range(n - 1):
        h = x.astype(BF16)
        parts.append(h)
        x = x - h.astype(F32)
    parts.append(x.astype(BF16))
    return parts


def _log_sigmoid(z):
    return jnp.minimum(z, 0.0) - jnp.log(1.0 + jnp.exp2(jnp.abs(z) * -LOG2E))


def _silu(g):
    return g / (1.0 + jnp.exp(-g))


def _emit_pipelined(stages, items):
    for step in range(len(items) + len(stages) - 1):
        for depth, stage in enumerate(stages):
            j = step - depth
            if 0 <= j < len(items):
                stage(items[j])


def _proj_kernel(x_ref, w_ref, gla_ref, sb_ref, c3_ref, ga_ref):
    xb = x_ref[...].astype(BF16)
    w_ref = w_ref.at[0]
    off = 0
    for ref, width in ((gla_ref, GLA_COLS), (sb_ref, SB_COLS), (c3_ref, C3_COLS), (ga_ref, GA_COLS)):
        step = 512 if width % 512 == 0 else width if width < 512 else 256
        for c in range(0, width, step):
            ref[:, c:c + step] = _dot(xb, w_ref[:, off + c:off + c + step]).astype(BF16)
        off += width


def _proj(x2, w, layer, tm=ROW_TILE):
    n = x2.shape[0]
    outs = (GLA_COLS, SB_COLS, C3_COLS, GA_COLS)
    return pl.pallas_call(
        _proj_kernel,
        grid=(n // tm,),
        in_specs=[pl.BlockSpec((tm, D_MODEL), lambda i: (i, 0)),
                  pl.BlockSpec((1, D_MODEL, PROJ_COLS), lambda i: (layer, 0, 0))],
        out_specs=[pl.BlockSpec((tm, c), lambda i: (i, 0)) for c in outs],
        out_shape=[jax.ShapeDtypeStruct((n, c), BF16) for c in outs],
        compiler_params=pltpu.CompilerParams(dimension_semantics=("arbitrary",),
                                             vmem_limit_bytes=VMEM_LIMIT),
        name="proj",
    )(x2, w)


def _memkv_kernel(m_ref, w_ref, o_ref):
    o_ref[0] = _dot(m_ref[...].astype(BF16), w_ref[0]).astype(BF16)


def _memkv(mem2, w_mkv):
    n = mem2.shape[0]
    return pl.pallas_call(
        _memkv_kernel,
        grid=(DEPTH,),
        in_specs=[pl.BlockSpec((n, D_MODEL), lambda l: (0, 0)),
                  pl.BlockSpec((1, D_MODEL, 2 * MEM_W), lambda l: (l, 0, 0))],
        out_specs=pl.BlockSpec((1, n, 2 * MEM_W), lambda l: (l, 0, 0)),
        out_shape=jax.ShapeDtypeStruct((DEPTH, n, 2 * MEM_W), BF16),
        compiler_params=pltpu.CompilerParams(dimension_semantics=("arbitrary",),
                                             vmem_limit_bytes=VMEM_LIMIT),
        name="memkv",
    )(mem2, w_mkv)


def _gla_kernel(gla_ref, ga_ref, wa2_ref, ba_ref, nw_ref, out_ref, st_ref, *, tt):
    c_sz = GLA_CHUNK

    @pl.when(pl.program_id(1) == 0)
    def _():
        st_ref[...] = jnp.zeros_like(st_ref)

    row = lax.broadcasted_iota(jnp.int32, (c_sz, c_sz), 0)
    col = lax.broadcasted_iota(jnp.int32, (c_sz, c_sz), 1)
    causal = col <= row
    r2 = lax.broadcasted_iota(jnp.int32, (c_sz, 2 * c_sz), 0)
    c2 = lax.broadcasted_iota(jnp.int32, (c_sz, 2 * c_sz), 1) & (c_sz - 1)
    tri_incl2 = jnp.where(c2 <= r2, 1.0, 0.0).astype(BF16)
    low_c = lax.broadcasted_iota(jnp.int32, (c_sz, LANES), 1) < GLA_DK
    low_v = lax.broadcasted_iota(jnp.int32, (GLA_DV, LANES), 1) < GLA_DK
    chunks = range(tt // c_sz)
    pairs = range(GLA_HEADS // 2)
    rows = [slice(c * c_sz, (c + 1) * c_sz) for c in chunks]
    lanes = [slice(p * LANES, (p + 1) * LANES) for p in pairs]
    v_col = 2 * GLA_QK_W
    g_col = v_col + GLA_V_W

    zz = _dot(ga_ref[0], wa2_ref[0]) + ba_ref[0]
    log_a = _log_sigmoid(zz) * (1.0 / GLA_GATE_NORM)
    hi, lo = _split_bf16(log_a, 2)
    g_cum, kd, decay, q2, a2, upd, inter2 = {}, {}, {}, {}, {}, {}, {}
    state = {p: st_ref[p] for p in pairs}

    def v_head(c, h):
        return gla_ref[0, rows[c], v_col + h * GLA_DV:v_col + (h + 1) * GLA_DV]

    def cumsum(c):
        g_cum[c] = _dot(tri_incl2, jnp.concatenate([hi[rows[c]], lo[rows[c]]], axis=0))

    def scale(c):
        g_last = g_cum[c][c_sz - 1:c_sz, :]
        q = gla_ref[0, rows[c], 0:GLA_QK_W].astype(F32)
        k = gla_ref[0, rows[c], GLA_QK_W:2 * GLA_QK_W].astype(F32)
        qg = (q * jnp.exp(g_cum[c])).astype(BF16)
        kg = (k * jnp.exp(-g_cum[c])).astype(BF16)
        kd[c] = (k * jnp.exp(g_last - g_cum[c])).astype(BF16)
        decay[c] = jnp.exp(g_last)
        for p in pairs:
            qgp = qg[:, lanes[p]]
            q2[c, p] = jnp.concatenate([jnp.where(low_c, qgp, jnp.zeros_like(qgp)),
                                        jnp.where(low_c, jnp.zeros_like(qgp), qgp)], axis=0)
            a2[c, p] = _dot_nt(q2[c, p], kg[:, lanes[p]])

    def update(c):
        for p in pairs:
            for hh in range(2):
                upd[c, p, hh] = _dot_tn(v_head(c, 2 * p + hh), kd[c][:, lanes[p]])

    def recur(c):
        for p in pairs:
            inter2[c, p] = _dot_nt(q2[c, p], state[p].astype(BF16))
            state[p] = state[p] * decay[c][:, lanes[p]] + jnp.where(low_v, upd[c, p, 0], upd[c, p, 1])

    def finish(c):
        for p in pairs:
            for hh in range(2):
                h = 2 * p + hh
                a = jnp.where(causal, a2[c, p][hh * c_sz:(hh + 1) * c_sz], 0.0).astype(BF16)
                o = _dot(a, v_head(c, h)) + inter2[c, p][hh * c_sz:(hh + 1) * c_sz]
                ms = jnp.mean(o * o, axis=-1, keepdims=True)
                gate = gla_ref[0, rows[c], g_col + h * GLA_DV:g_col + (h + 1) * GLA_DV].astype(F32)
                res = o * lax.rsqrt(ms + RMS_EPS) * nw_ref[0] * _silu(gate)
                out_ref[0, rows[c], h * GLA_DV:(h + 1) * GLA_DV] = res.astype(BF16)

    _emit_pipelined((cumsum, scale, update, recur, finish), list(chunks))
    for p in pairs:
        st_ref[p] = state[p]


def _gla(gla, ga, wa2, ba, nw, layer, tt=GLA_TILE):
    b, t, _ = gla.shape
    return pl.pallas_call(
        functools.partial(_gla_kernel, tt=tt),
        grid=(b, t // tt),
        in_specs=[pl.BlockSpec((1, tt, GLA_COLS), lambda i, j: (i, j, 0)),
                  pl.BlockSpec((1, tt, GA_COLS), lambda i, j: (i, j, 0)),
                  pl.BlockSpec((1, GA_COLS, GLA_QK_W), lambda i, j: (layer, 0, 0)),
                  pl.BlockSpec((1, 1, GLA_QK_W), lambda i, j: (layer, 0, 0)),
                  pl.BlockSpec((1, 1, GLA_DV), lambda i, j: (layer, 0, 0))],
        out_specs=pl.BlockSpec((1, tt, GLA_V_W), lambda i, j: (i, j, 0)),
        out_shape=jax.ShapeDtypeStruct((b, t, GLA_V_W), BF16),
        scratch_shapes=[pltpu.VMEM((GLA_HEADS // 2, GLA_DV, LANES), F32)],
        compiler_params=pltpu.CompilerParams(dimension_semantics=("arbitrary", "arbitrary"),
                                             vmem_limit_bytes=VMEM_LIMIT),
        name="gla",
    )(gla, ga, wa2, ba, nw)


def _sb_kernel(q_ref, k_ref, v_ref, g_ref, o_ref, oacc_ref, lacc_ref):
    hb, kb = SB_HALF, SB_BLOCK
    pairs = SB_HEADS // 2
    halves = SB_QTILE // hb
    q_base = pl.program_id(1) * SB_QTILE
    row = lax.broadcasted_iota(jnp.int32, (hb, kb), 0)
    lane = lax.broadcasted_iota(jnp.int32, (hb, kb), 1)
    low = lane < SB_DH
    rk = lax.broadcasted_iota(jnp.int32, (kb, 2 * kb), 0)
    ck = lax.broadcasted_iota(jnp.int32, (kb, 2 * kb), 1)
    tri_ones = jnp.where((rk > ck) | (ck >= kb), 1.0, 0.0).astype(BF16)

    def pair_rows(m, p):
        st = m * pairs + p
        return slice(st * 2 * hb, (st + 1) * 2 * hb)

    qneg = {}
    for m in range(halves):
        for p in range(pairs):
            qn = -q_ref[0, m * hb:(m + 1) * hb, p * LANES:(p + 1) * LANES]
            zero = jnp.zeros_like(qn)
            qneg[m, p] = jnp.concatenate([jnp.where(low, qn, zero), jnp.where(low, zero, qn)], axis=0)
    def lead_group(start, valid):
        rows = {m: pl.ds(pl.multiple_of(start[m], hb), 2 * kb) for m in start}
        r4 = lax.broadcasted_iota(jnp.int32, (2 * kb, 2 * kb), 0)
        c4 = lax.broadcasted_iota(jnp.int32, (2 * kb, 2 * kb), 1)
        later = ((r4 > c4) & ((r4 >= kb) == (c4 >= kb))) | ((r4 < kb) & (c4 >= kb))
        tri_pair = jnp.where(later, 1.0, 0.0).astype(BF16)
        keep1 = jnp.where(valid, 1.0, 0.0).astype(BF16)
        keep = jnp.concatenate([keep1, keep1], axis=0)
        log_beta, lf, sums = {}, {}, {}

        def scores(m):
            for p in range(pairs):
                s = _dot_nt(qneg[m, p], k_ref[0, rows[m], p * LANES:(p + 1) * LANES])
                for g in (1, 0):
                    sg = s[:, g * kb:(g + 1) * kb].astype(BF16)
                    lsn = _log_sigmoid(sg)
                    log_beta[m, p, g] = lsn - sg
                    lf[m, p, g] = lsn * keep if g == 1 else lsn

        def suffix_sums(m):
            lhs = [jnp.concatenate([lf[m, p, 1], lf[m, p, 0]], axis=1) for p in range(pairs)]
            sums[m] = _dot(jnp.concatenate(lhs, axis=0), tri_pair)

        def weights(m):
            for p in range(pairs):
                part = sums[m][p * 2 * hb:(p + 1) * 2 * hb]
                w1 = jnp.exp(part[:, :kb].astype(BF16) + log_beta[m, p, 1]) * keep
                w0 = jnp.exp(part[:, kb:].astype(BF16) + log_beta[m, p, 0])
                total = part[:, kb:kb + 1] + lf[m, p, 0][:, 0:1].astype(F32)
                lacc_ref[pair_rows(m, p)] = jnp.broadcast_to(total, (2 * hb, kb))
                oacc_ref[pair_rows(m, p)] = _dot(jnp.concatenate([w0, w1], axis=1),
                                                 v_ref[0, rows[m], p * LANES:(p + 1) * LANES])

        _emit_pipelined((scores, suffix_sums, weights), list(start))

    def group(start, n, valid):
        order = list(reversed(range(n)))
        rows = {m: pl.ds(pl.multiple_of(start[m], hb), n * kb) for m in start}
        log_beta, lf, sums = {}, {}, {}
        keep1 = jnp.where(valid, 1.0, 0.0).astype(BF16)
        keep = jnp.concatenate([keep1, keep1], axis=0)
        parts = [(p, g) for g in order for p in range(pairs)]

        def scores(m):
            s = {p: _dot_nt(qneg[m, p], k_ref[0, rows[m], p * LANES:(p + 1) * LANES])
                 for p in range(pairs)}
            for p, g in parts:
                sg = s[p][:, g * kb:(g + 1) * kb].astype(BF16)
                lsn = _log_sigmoid(sg)
                log_beta[m, p, g] = lsn - sg
                lf[m, p, g] = lsn * keep if g == n - 1 else lsn

        def suffix_sums(m):
            sums[m] = _dot(jnp.concatenate([lf[m, p, g] for p, g in parts], axis=0), tri_ones)

        def weights(m):
            acc = {p: lacc_ref[pair_rows(m, p)] for p in range(pairs)}
            ws = {p: [None] * n for p in range(pairs)}
            for idx, (p, g) in enumerate(parts):
                part = sums[m][idx * 2 * hb:(idx + 1) * 2 * hb]
                w = jnp.exp((part[:, :kb] + acc[p]).astype(BF16) + log_beta[m, p, g])
                ws[p][g] = w * keep if g == n - 1 else w
                acc[p] = acc[p] + part[:, kb:]
            for p in range(pairs):
                lacc_ref[pair_rows(m, p)] = acc[p]
                oacc_ref[pair_rows(m, p)] += _dot(jnp.concatenate(ws[p], axis=1),
                                                  v_ref[0, rows[m], p * LANES:(p + 1) * LANES])

        _emit_pipelined((scores, suffix_sums, weights), list(start))

    def sweep(m, bound):
        per_m = 2 * pairs * hb

        def live():
            return jnp.max(lacc_ref[m * per_m:(m + 1) * per_m])

        def cond(c):
            return jnp.logical_and(c[0] > 0, c[1] > SB_SKIP_LOG)

        def body(c):
            first = jnp.maximum(c[0] - kb, 0)
            group({m: first}, 1, lane < c[0] - first)
            return first, live()

        lax.while_loop(cond, body, (bound, live()))

    @pl.when(q_base >= SB_LEAD_KEYS - hb)
    def _():
        lead = {m: q_base + (m + 1) * hb - SB_LEAD_KEYS for m in range(halves)}
        lead_group(lead, lane < row + (kb - hb))

        @pl.when(jnp.max(lacc_ref[...]) > SB_SKIP_LOG)
        def _():
            for m in range(halves):
                sweep(m, lead[m])

    @pl.when(q_base < SB_LEAD_KEYS - hb)
    def _():
        oacc_ref[...] = jnp.zeros_like(oacc_ref)
        lacc_ref[...] = jnp.zeros_like(lacc_ref)
        for m in range(halves):
            first = max(m * hb + hb - kb, 0)
            group({m: first}, 1, lane < row + (m * hb - first))
            sweep(m, first)

    for m in range(halves):
        for p in range(pairs):
            both = oacc_ref[pair_rows(m, p)]
            o = jnp.where(low, both[:hb], both[hb:])
            gate = g_ref[0, m * hb:(m + 1) * hb, p * LANES:(p + 1) * LANES].astype(F32)
            o_ref[0, m * hb:(m + 1) * hb, p * LANES:(p + 1) * LANES] = (o * _silu(gate)).astype(BF16)


def _sb(sb, c3):
    b, t, _ = sb.shape
    n_streams = (SB_QTILE // SB_HALF) * SB_HEADS
    return pl.pallas_call(
        _sb_kernel,
        grid=(b, t // SB_QTILE),
        in_specs=[pl.BlockSpec((1, SB_QTILE, SB_W), lambda bi, i: (bi, i, 0)),
                  pl.BlockSpec((1, t, SB_W), lambda bi, i: (bi, 0, 1)),
                  pl.BlockSpec((1, t, SB_W), lambda bi, i: (bi, 0, 2)),
                  pl.BlockSpec((1, SB_QTILE, SB_W), lambda bi, i: (bi, i, 0))],
        out_specs=pl.BlockSpec((1, SB_QTILE, SB_W), lambda bi, i: (bi, i, 0)),
        out_shape=jax.ShapeDtypeStruct((b, t, SB_W), BF16),
        scratch_shapes=[pltpu.VMEM((n_streams * SB_HALF, LANES), F32),
                        pltpu.VMEM((n_streams * SB_HALF, LANES), F32)],
        compiler_params=pltpu.CompilerParams(dimension_semantics=("arbitrary", "arbitrary"),
                                             vmem_limit_bytes=VMEM_LIMIT),
        name="sb",
    )(sb, sb, sb, c3)


def _out_kernel(x_ref, mixg_ref, mixs_ref, c3_ref, kvm_ref, wout_ref, lng_ref, lnb_ref, o_ref, *, tm, sub):
    wout_ref = wout_ref.at[0]
    low = lax.broadcasted_iota(jnp.int32, (sub, LANES), 1) < MEM_DH
    pairs = range(MEM_HEADS // 2)
    s, e, den, y = {}, {}, {}, {}

    def rows(j):
        return slice(j * sub, (j + 1) * sub)

    def scores(j):
        for p in pairs:
            qp = c3_ref[0, rows(j), SB_W + p * LANES:SB_W + (p + 1) * LANES]
            km = kvm_ref[0, 0, :, p * LANES:(p + 1) * LANES]
            zero = jnp.zeros_like(qp)
            s[j, p, 0] = _dot_nt(jnp.where(low, qp, zero), km)
            s[j, p, 1] = _dot_nt(jnp.where(low, zero, qp), km)

    def softmax(j):
        for p in pairs:
            for hh in range(2):
                sc = s[j, p, hh]
                ex = jnp.exp(sc - jnp.max(sc, axis=-1, keepdims=True))
                den[j, p, hh] = jnp.sum(ex, axis=-1, keepdims=True)
                e[j, p, hh] = ex.astype(BF16)

    def mix(j):
        mixm = []
        for p in pairs:
            vm = kvm_ref[0, 0, :, MEM_W + p * LANES:MEM_W + (p + 1) * LANES]
            om = jnp.where(low, _dot(e[j, p, 0], vm) / den[j, p, 0], _dot(e[j, p, 1], vm) / den[j, p, 1])
            gate = c3_ref[0, rows(j), SB_W + MEM_W + p * LANES:SB_W + MEM_W + (p + 1) * LANES].astype(F32)
            mixm.append((om * _silu(gate)).astype(BF16))
        y[j] = _dot(jnp.concatenate([mixg_ref[0, rows(j), :], mixs_ref[0, rows(j), :]] + mixm, axis=1), wout_ref[...])

    def norm(j):
        r = ALPHA * x_ref[0, rows(j), :] + y[j]
        mu = jnp.mean(r, axis=-1, keepdims=True)
        d = r - mu
        var = jnp.mean(d * d, axis=-1, keepdims=True)
        o_ref[0, rows(j), :] = d * lax.rsqrt(var + LN_EPS) * lng_ref[0] + lnb_ref[0]

    _emit_pipelined((scores, softmax, mix, norm), list(range(tm // sub)))


def _out(x, mixg, mixs, c3, kvm, layer, wout, lng, lnb, tm=OUT_TILE, sub=ROW_TILE):
    b, t, _ = x.shape
    return pl.pallas_call(
        functools.partial(_out_kernel, tm=tm, sub=sub),
        grid=(b, t // tm),
        in_specs=[pl.BlockSpec((1, tm, D_MODEL), lambda i, j: (i, j, 0)),
                  pl.BlockSpec((1, tm, GLA_V_W), lambda i, j: (i, j, 0)),
                  pl.BlockSpec((1, tm, SB_W), lambda i, j: (i, j, 0)),
                  pl.BlockSpec((1, tm, C3_COLS), lambda i, j: (i, j, 0)),
                  pl.BlockSpec((1, 1, N_MEM, 2 * MEM_W), lambda i, j: (layer, i, 0, 0)),
                  pl.BlockSpec((1, D_MODEL, D_MODEL), lambda i, j: (layer, 0, 0)),
                  pl.BlockSpec((1, 1, D_MODEL), lambda i, j: (layer, 0, 0)),
                  pl.BlockSpec((1, 1, D_MODEL), lambda i, j: (layer, 0, 0))],
        out_specs=pl.BlockSpec((1, tm, D_MODEL), lambda i, j: (i, j, 0)),
        out_shape=jax.ShapeDtypeStruct((b, t, D_MODEL), F32),
        compiler_params=pltpu.CompilerParams(dimension_semantics=("arbitrary", "arbitrary"),
                                             vmem_limit_bytes=VMEM_LIMIT),
        name="out",
    )(x, mixg, mixs, c3, kvm, wout, lng, lnb)


_W_IN_SEGMENTS = (("gq", GLA_QK_W), ("gk", GLA_QK_W), ("gv", GLA_V_W), ("gg", GLA_V_W), ("ga", GLA_RANK),
                  ("sq", SB_W), ("sk", SB_W), ("sv", SB_W), ("sg", SB_W), ("mq", MEM_W), ("mg", MEM_W))
_W_PROJ_ORDER = ("gq", "gk", "gv", "gg", "sq", "sk", "sv", "sg", "mq", "mg", "ga")
_Q_SCALED = {"gq": GLA_DK ** -0.5, "sq": SB_DH ** -0.5, "mq": MEM_DH ** -0.5}
D_IN = sum(n for _, n in _W_IN_SEGMENTS)


def _wprep_kernel(w_ref, o_ref):
    src, off = {}, 0
    for name, n in _W_IN_SEGMENTS:
        src[name] = (off, n)
        off += n
    dst = 0
    for name in _W_PROJ_ORDER:
        lo, n = src[name]
        cols = w_ref[0, :, lo:lo + n]
        o_ref[0, :, dst:dst + n] = cols * _Q_SCALED[name] if name in _Q_SCALED else cols
        dst += n
    o_ref[0, :, dst:PROJ_COLS] = jnp.zeros((o_ref.shape[1], PROJ_COLS - dst), BF16)


def _regroup_w_in(w, tr=256):
    depth, d, _ = w.shape
    return pl.pallas_call(
        _wprep_kernel,
        grid=(depth, d // tr),
        in_specs=[pl.BlockSpec((1, tr, D_IN), lambda l, i: (l, i, 0))],
        out_specs=pl.BlockSpec((1, tr, PROJ_COLS), lambda l, i: (l, i, 0)),
        out_shape=jax.ShapeDtypeStruct((depth, d, PROJ_COLS), BF16),
        compiler_params=pltpu.CompilerParams(dimension_semantics=("arbitrary", "arbitrary"),
                                             vmem_limit_bytes=VMEM_LIMIT),
        name="wprep",
    )(w)


def kernel(x, mem, w_in, w_alpha2, b_alpha, gla_norm_w, w_mem_kv, w_out, ln_g, ln_b):
    b, t, d = x.shape
    assert d == D_MODEL and mem.shape == (b, N_MEM, d) and w_in.shape == (DEPTH, d, D_IN)
    assert t % GLA_TILE == 0 and t % SB_QTILE == 0 and t % OUT_TILE == 0
    kvm = _memkv(mem.reshape(b * N_MEM, d), w_mem_kv.astype(BF16)).reshape(DEPTH, b, N_MEM, 2 * MEM_W)
    w_proj = _regroup_w_in(w_in.astype(BF16))
    wa2 = jnp.pad(w_alpha2.astype(BF16), ((0, 0), (0, GA_COLS - GLA_RANK), (0, 0)))
    ba = b_alpha.reshape(DEPTH, 1, GLA_QK_W)
    nw = gla_norm_w.reshape(DEPTH, 1, GLA_DV)
    wout = w_out.astype(BF16)
    lng = ln_g.reshape(DEPTH, 1, d)
    lnb = ln_b.reshape(DEPTH, 1, d)
    for l in range(DEPTH):
        gla, sb, c3, ga = _proj(x.reshape(b * t, d), w_proj, l)
        mixg = _gla(gla.reshape(b, t, GLA_COLS), ga.reshape(b, t, GA_COLS), wa2, ba, nw, l)
        c3 = c3.reshape(b, t, C3_COLS)
        mixs = _sb(sb.reshape(b, t, SB_COLS), c3)
        x = _out(x, mixg, mixs, c3, kvm, l, wout, lng, lnb)
    return x
```

```python
import functools

import jax
import jax.numpy as jnp
from jax import lax
from jax.experimental import pallas as pl
from jax.experimental.pallas import tpu as pltpu

F32 = jnp.float32
BF16 = jnp.bfloat16

D_MODEL = 1024
DEPTH = 4
N_MEM = 256
GLA_HEADS = 4
GLA_DK = 64
GLA_DV = 128
GLA_RANK = 16
GLA_GATE_NORM = 16.0
GLA_CHUNK = 64
SB_HEADS = 4
SB_DH = 64
SB_BLOCK = 128
MEM_HEADS = 4
MEM_DH = 64
GLA_QK_W = GLA_HEADS * GLA_DK
GLA_V_W = GLA_HEADS * GLA_DV
SB_W = SB_HEADS * SB_DH
MEM_W = MEM_HEADS * MEM_DH
ALPHA = (2.0 * DEPTH) ** 0.25
LN_EPS = 1e-5
RMS_EPS = 1e-6

LANES = 128
GLA_COLS = 2 * GLA_QK_W + 2 * GLA_V_W
SB_COLS = 3 * SB_W
C3_COLS = SB_W + 2 * MEM_W
GA_COLS = LANES
PROJ_COLS = GLA_COLS + SB_COLS + C3_COLS + GA_COLS
VMEM_LIMIT = 56 * 1024 * 1024
SB_SKIP_LOG = -106.0
SB_HALF = 64
SB_LEAD_KEYS = 2 * SB_BLOCK
SB_QTILE = 1024
GLA_TILE = 1024
ROW_TILE = 512
OUT_TILE = 1024
LOG2E = 1.4426950408889634


def _dot(a, b):
    return jnp.dot(a, b, preferred_element_type=F32)


def _dot_nt(a, b):
    return lax.dot_general(a, b, (((1,), (1,)), ((), ())), preferred_element_type=F32)


def _dot_tn(a, b):
    return lax.dot_general(a, b, (((0,), (0,)), ((), ())), preferred_element_type=F32)


def _split_bf16(x, n):
    parts = []
    for _ in range(n - 1):
        h = x.astype(BF16)
        parts.append(h)
        x = x - h.astype(F32)
    parts.append(x.astype(BF16))
    return parts


def _log_sigmoid(z):
    return jnp.minimum(z, 0.0) - jnp.log(1.0 + jnp.exp2(jnp.abs(z) * -LOG2E))


def _silu(g):
    return g / (1.0 + jnp.exp(-g))


def _emit_pipelined(stages, items):
    for step in range(len(items) + len(stages) - 1):
        for depth, stage in enumerate(stages):
            j = step - depth
            if 0 <= j < len(items):
                stage(items[j])


def _proj_kernel(x_ref, w_ref, gla_ref, sb_ref, c3_ref, ga_ref):
    xb = x_ref[...].astype(BF16)
    w_ref = w_ref.at[0]
    off = 0
    for ref, width in ((gla_ref, GLA_COLS), (sb_ref, SB_COLS), (c3_ref, C3_COLS), (ga_ref, GA_COLS)):
        step = 512 if width % 512 == 0 else width if width < 512 else 256
        for c in range(0, width, step):
            ref[:, c:c + step] = _dot(xb, w_ref[:, off + c:off + c + step]).astype(BF16)
        off += width


def _proj(x2, w, layer, tm=ROW_TILE):
    n = x2.shape[0]
    outs = (GLA_COLS, SB_COLS, C3_COLS, GA_COLS)
    return pl.pallas_call(
        _proj_kernel,
        grid=(n // tm,),
        in_specs=[pl.BlockSpec((tm, D_MODEL), lambda i: (i, 0)),
                  pl.BlockSpec((1, D_MODEL, PROJ_COLS), lambda i: (layer, 0, 0))],
        out_specs=[pl.BlockSpec((tm, c), lambda i: (i, 0)) for c in outs],
        out_shape=[jax.ShapeDtypeStruct((n, c), BF16) for c in outs],
        compiler_params=pltpu.CompilerParams(dimension_semantics=("arbitrary",),
                                             vmem_limit_bytes=VMEM_LIMIT),
        name="proj",
    )(x2, w)


def _memkv_kernel(m_ref, w_ref, o_ref):
    o_ref[0] = _dot(m_ref[...].astype(BF16), w_ref[0]).astype(BF16)


def _memkv(mem2, w_mkv):
    n = mem2.shape[0]
    return pl.pallas_call(
        _memkv_kernel,
        grid=(DEPTH,),
        in_specs=[pl.BlockSpec((n, D_MODEL), lambda l: (0, 0)),
                  pl.BlockSpec((1, D_MODEL, 2 * MEM_W), lambda l: (l, 0, 0))],
        out_specs=pl.BlockSpec((1, n, 2 * MEM_W), lambda l: (l, 0, 0)),
        out_shape=jax.ShapeDtypeStruct((DEPTH, n, 2 * MEM_W), BF16),
        compiler_params=pltpu.CompilerParams(dimension_semantics=("arbitrary",),
                                             vmem_limit_bytes=VMEM_LIMIT),
        name="memkv",
    )(mem2, w_mkv)


def _gla_kernel(gla_ref, ga_ref, wa2_ref, ba_ref, nw_ref, out_ref, st_ref, *, tt):
    c_sz = GLA_CHUNK

    @pl.when(pl.program_id(1) == 0)
    def _():
        st_ref[...] = jnp.zeros_like(st_ref)

    row = lax.broadcasted_iota(jnp.int32, (c_sz, c_sz), 0)
    col = lax.broadcasted_iota(jnp.int32, (c_sz, c_sz), 1)
    causal = col <= row
    r2 = lax.broadcasted_iota(jnp.int32, (c_sz, 2 * c_sz), 0)
    c2 = lax.broadcasted_iota(jnp.int32, (c_sz, 2 * c_sz), 1) & (c_sz - 1)
    tri_incl2 = jnp.where(c2 <= r2, 1.0, 0.0).astype(BF16)
    low_c = lax.broadcasted_iota(jnp.int32, (c_sz, LANES), 1) < GLA_DK
    low_v = lax.broadcasted_iota(jnp.int32, (GLA_DV, LANES), 1) < GLA_DK
    chunks = range(tt // c_sz)
    pairs = range(GLA_HEADS // 2)
    rows = [slice(c * c_sz, (c + 1) * c_sz) for c in chunks]
    lanes = [slice(p * LANES, (p + 1) * LANES) for p in pairs]
    v_col = 2 * GLA_QK_W
    g_col = v_col + GLA_V_W

    zz = _dot(ga_ref[0], wa2_ref[0]) + ba_ref[0]
    log_a = _log_sigmoid(zz) * (1.0 / GLA_GATE_NORM)
    hi, lo = _split_bf16(log_a, 2)
    g_cum, kd, decay, q2, a2, upd, inter2 = {}, {}, {}, {}, {}, {}, {}
    state = {p: st_ref[p] for p in pairs}

    def v_head(c, h):
        return gla_ref[0, rows[c], v_col + h * GLA_DV:v_col + (h + 1) * GLA_DV]

    def cumsum(c):
        g_cum[c] = _dot(tri_incl2, jnp.concatenate([hi[rows[c]], lo[rows[c]]], axis=0))

    def scale(c):
        g_last = g_cum[c][c_sz - 1:c_sz, :]
        q = gla_ref[0, rows[c], 0:GLA_QK_W].astype(F32)
        k = gla_ref[0, rows[c], GLA_QK_W:2 * GLA_QK_W].astype(F32)
        qg = (q * jnp.exp(g_cum[c])).astype(BF16)
        kg = (k * jnp.exp(-g_cum[c])).astype(BF16)
        kd[c] = (k * jnp.exp(g_last - g_cum[c])).astype(BF16)
        decay[c] = jnp.exp(g_last)
        for p in pairs:
            qgp = qg[:, lanes[p]]
            q2[c, p] = jnp.concatenate([jnp.where(low_c, qgp, jnp.zeros_like(qgp)),
                                        jnp.where(low_c, jnp.zeros_like(qgp), qgp)], axis=0)
            a2[c, p] = _dot_nt(q2[c, p], kg[:, lanes[p]])

    def update(c):
        for p in pairs:
            for hh in range(2):
                upd[c, p, hh] = _dot_tn(v_head(c, 2 * p + hh), kd[c][:, lanes[p]])

    def recur(c):
        for p in pairs:
            inter2[c, p] = _dot_nt(q2[c, p], state[p].astype(BF16))
            state[p] = state[p] * decay[c][:, lanes[p]] + jnp.where(low_v, upd[c, p, 0], upd[c, p, 1])

    def finish(c):
        for p in pairs:
            for hh in range(2):
                h = 2 * p + hh
                a = jnp.where(causal, a2[c, p][hh * c_sz:(hh + 1) * c_sz], 0.0).astype(BF16)
                o = _dot(a, v_head(c, h)) + inter2[c, p][hh * c_sz:(hh + 1) * c_sz]
                ms = jnp.mean(o * o, axis=-1, keepdims=True)
                gate = gla_ref[0, rows[c], g_col + h * GLA_DV:g_col + (h + 1) * GLA_DV].astype(F32)
                res = o * lax.rsqrt(ms + RMS_EPS) * nw_ref[0] * _silu(gate)
                out_ref[0, rows[c], h * GLA_DV:(h + 1) * GLA_DV] = res.astype(BF16)

    _emit_pipelined((cumsum, scale, update, recur, finish), list(chunks))
    for p in pairs:
        st_ref[p] = state[p]


def _gla(gla, ga, wa2, ba, nw, layer, tt=GLA_TILE):
    b, t, _ = gla.shape
    return pl.pallas_call(
        functools.partial(_gla_kernel, tt=tt),
        grid=(b, t // tt),
        in_specs=[pl.BlockSpec((1, tt, GLA_COLS), lambda i, j: (i, j, 0)),
                  pl.BlockSpec((1, tt, GA_COLS), lambda i, j: (i, j, 0)),
                  pl.BlockSpec((1, GA_COLS, GLA_QK_W), lambda i, j: (layer, 0, 0)),
                  pl.BlockSpec((1, 1, GLA_QK_W), lambda i, j: (layer, 0, 0)),
                  pl.BlockSpec((1, 1, GLA_DV), lambda i, j: (layer, 0, 0))],
        out_specs=pl.BlockSpec((1, tt, GLA_V_W), lambda i, j: (i, j, 0)),
        out_shape=jax.ShapeDtypeStruct((b, t, GLA_V_W), BF16),
        scratch_shapes=[pltpu.VMEM((GLA_HEADS // 2, GLA_DV, LANES), F32)],
        compiler_params=pltpu.CompilerParams(dimension_semantics=("arbitrary", "arbitrary"),
                                             vmem_limit_bytes=VMEM_LIMIT),
        name="gla",
    )(gla, ga, wa2, ba, nw)


def _sb_kernel(q_ref, k_ref, v_ref, g_ref, o_ref, oacc_ref, lacc_ref):
    hb, kb = SB_HALF, SB_BLOCK
    pairs = SB_HEADS // 2
    halves = SB_QTILE // hb
    q_base = pl.program_id(1) * SB_QTILE
    row = lax.broadcasted_iota(jnp.int32, (hb, kb), 0)
    lane = lax.broadcasted_iota(jnp.int32, (hb, kb), 1)
    low = lane < SB_DH
    rk = lax.broadcasted_iota(jnp.int32, (kb, 2 * kb), 0)
    ck = lax.broadcasted_iota(jnp.int32, (kb, 2 * kb), 1)
    tri_ones = jnp.where((rk > ck) | (ck >= kb), 1.0, 0.0).astype(BF16)

    def pair_rows(m, p):
        st = m * pairs + p
        return slice(st * 2 * hb, (st + 1) * 2 * hb)

    qneg = {}
    for m in range(halves):
        for p in range(pairs):
            qn = -q_ref[0, m * hb:(m + 1) * hb, p * LANES:(p + 1) * LANES]
            zero = jnp.zeros_like(qn)
            qneg[m, p] = jnp.concatenate([jnp.where(low, qn, zero), jnp.where(low, zero, qn)], axis=0)
    def lead_group(start, valid):
        rows = {m: pl.ds(start[m] if isinstance(start[m], int) else pl.multiple_of(start[m], hb), 2 * kb)
                for m in start}
        r4 = lax.broadcasted_iota(jnp.int32, (2 * kb, 2 * kb), 0)
        c4 = lax.broadcasted_iota(jnp.int32, (2 * kb, 2 * kb), 1)
        later = ((r4 > c4) & ((r4 >= kb) == (c4 >= kb))) | ((r4 < kb) & (c4 >= kb))
        tri_pair = jnp.where(later, 1.0, 0.0).astype(BF16)
        keep1 = jnp.where(valid, 1.0, 0.0).astype(BF16)
        keep = jnp.concatenate([keep1, keep1], axis=0)
        log_beta, lf, sums = {}, {}, {}

        def scores(m):
            for p in range(pairs):
                s = _dot_nt(qneg[m, p], k_ref[0, rows[m], p * LANES:(p + 1) * LANES])
                for g in (1, 0):
                    sg = s[:, g * kb:(g + 1) * kb].astype(BF16)
                    lsn = _log_sigmoid(sg)
                    log_beta[m, p, g] = lsn - sg
                    lf[m, p, g] = lsn * keep if g == 1 else lsn

        def suffix_sums(m):
            lhs = [jnp.concatenate([lf[m, p, 1], lf[m, p, 0]], axis=1) for p in range(pairs)]
            sums[m] = _dot(jnp.concatenate(lhs, axis=0), tri_pair)

        def weights(m):
            for p in range(pairs):
                part = sums[m][p * 2 * hb:(p + 1) * 2 * hb]
                w1 = jnp.exp(part[:, :kb].astype(BF16) + log_beta[m, p, 1]) * keep
                w0 = jnp.exp(part[:, kb:].astype(BF16) + log_beta[m, p, 0])
                total = part[:, kb:kb + 1] + lf[m, p, 0][:, 0:1].astype(F32)
                lacc_ref[pair_rows(m, p)] = jnp.broadcast_to(total, (2 * hb, kb))
                oacc_ref[pair_rows(m, p)] = _dot(jnp.concatenate([w0, w1], axis=1),
                                                 v_ref[0, rows[m], p * LANES:(p + 1) * LANES])

        _emit_pipelined((scores, suffix_sums, weights), list(start))

    def group(start, n, valid):
        order = list(reversed(range(n)))
        rows = {m: pl.ds(start[m] if isinstance(start[m], int) else pl.multiple_of(start[m], hb), n * kb)
                for m in start}
        log_beta, lf, sums = {}, {}, {}
        keep1 = jnp.where(valid, 1.0, 0.0).astype(BF16)
        keep = jnp.concatenate([keep1, keep1], axis=0)
        parts = [(p, g) for g in order for p in range(pairs)]

        def scores(m):
            s = {p: _dot_nt(qneg[m, p], k_ref[0, rows[m], p * LANES:(p + 1) * LANES])
                 for p in range(pairs)}
            for p, g in parts:
                sg = s[p][:, g * kb:(g + 1) * kb].astype(BF16)
                lsn = _log_sigmoid(sg)
                log_beta[m, p, g] = lsn - sg
                lf[m, p, g] = lsn * keep if g == n - 1 else lsn

        def suffix_sums(m):
            sums[m] = _dot(jnp.concatenate([lf[m, p, g] for p, g in parts], axis=0), tri_ones)

        def weights(m):
            acc = {p: lacc_ref[pair_rows(m, p)] for p in range(pairs)}
            ws = {p: [None] * n for p in range(pairs)}
            for idx, (p, g) in enumerate(parts):
                part = sums[m][idx * 2 * hb:(idx + 1) * 2 * hb]
                w = jnp.exp((part[:, :kb] + acc[p]).astype(BF16) + log_beta[m, p, g])
                ws[p][g] = w * keep if g == n - 1 else w
                acc[p] = acc[p] + part[:, kb:]
            for p in range(pairs):
                lacc_ref[pair_rows(m, p)] = acc[p]
                oacc_ref[pair_rows(m, p)] += _dot(jnp.concatenate(ws[p], axis=1),
                                                  v_ref[0, rows[m], p * LANES:(p + 1) * LANES])

        _emit_pipelined((scores, suffix_sums, weights), list(start))

    def sweep(m, bound):
        per_m = 2 * pairs * hb

        def live():
            return jnp.max(lacc_ref[m * per_m:(m + 1) * per_m])

        def cond(c):
            return jnp.logical_and(c[0] > 0, c[1] > SB_SKIP_LOG)

        def body(c):
            first = jnp.maximum(c[0] - kb, 0)
            group({m: first}, 1, lane < c[0] - first)
            return first, live()

        lax.while_loop(cond, body, (bound, live()))

    def lead_and_sweep(base, ms):
        lead = {m: base + (m + 1) * hb - SB_LEAD_KEYS for m in ms}
        lead_group(lead, lane < row + (kb - hb))
        rows_all = slice(ms[0] * 2 * pairs * hb, (ms[-1] + 1) * 2 * pairs * hb)

        @pl.when(jnp.max(lacc_ref[rows_all]) > SB_SKIP_LOG)
        def _():
            for m in ms:
                sweep(m, lead[m])

    short = (SB_LEAD_KEYS - hb) // hb

    @pl.when(q_base > 0)
    def _():
        lead_and_sweep(q_base, list(range(halves)))

    @pl.when(q_base == 0)
    def _():
        for m in range(short):
            oacc_ref[m * 2 * pairs * hb:(m + 1) * 2 * pairs * hb] = jnp.zeros((2 * pairs * hb, LANES), F32)
            lacc_ref[m * 2 * pairs * hb:(m + 1) * 2 * pairs * hb] = jnp.zeros((2 * pairs * hb, LANES), F32)
            first = max(m * hb + hb - kb, 0)
            group({m: first}, 1, lane < row + (m * hb - first))
            sweep(m, first)
        lead_and_sweep(0, list(range(short, halves)))

    for m in range(halves):
        for p in range(pairs):
            both = oacc_ref[pair_rows(m, p)]
            o = jnp.where(low, both[:hb], both[hb:])
            gate = g_ref[0, m * hb:(m + 1) * hb, p * LANES:(p + 1) * LANES].astype(F32)
            o_ref[0, m * hb:(m + 1) * hb, p * LANES:(p + 1) * LANES] = (o * _silu(gate)).astype(BF16)


def _sb(sb, c3):
    b, t, _ = sb.shape
    n_streams = (SB_QTILE // SB_HALF) * SB_HEADS
    return pl.pallas_call(
        _sb_kernel,
        grid=(b, t // SB_QTILE),
        in_specs=[pl.BlockSpec((1, SB_QTILE, SB_W), lambda bi, i: (bi, i, 0)),
                  pl.BlockSpec((1, t, SB_W), lambda bi, i: (bi, 0, 1)),
                  pl.BlockSpec((1, t, SB_W), lambda bi, i: (bi, 0, 2)),
                  pl.BlockSpec((1, SB_QTILE, SB_W), lambda bi, i: (bi, i, 0))],
        out_specs=pl.BlockSpec((1, SB_QTILE, SB_W), lambda bi, i: (bi, i, 0)),
        out_shape=jax.ShapeDtypeStruct((b, t, SB_W), BF16),
        scratch_shapes=[pltpu.VMEM((n_streams * SB_HALF, LANES), F32),
                        pltpu.VMEM((n_streams * SB_HALF, LANES), F32)],
        compiler_params=pltpu.CompilerParams(dimension_semantics=("arbitrary", "arbitrary"),
                                             vmem_limit_bytes=VMEM_LIMIT),
        name="sb",
    )(sb, sb, sb, c3)


def _out_kernel(x_ref, mixg_ref, mixs_ref, c3_ref, kvm_ref, wout_ref, lng_ref, lnb_ref, o_ref, *, tm, sub):
    wout_ref = wout_ref.at[0]
    low = lax.broadcasted_iota(jnp.int32, (sub, LANES), 1) < MEM_DH
    pairs = range(MEM_HEADS // 2)
    s, e, den, y = {}, {}, {}, {}

    def rows(j):
        return slice(j * sub, (j + 1) * sub)

    def scores(j):
        for p in pairs:
            qp = c3_ref[0, rows(j), SB_W + p * LANES:SB_W + (p + 1) * LANES]
            km = kvm_ref[0, 0, :, p * LANES:(p + 1) * LANES]
            zero = jnp.zeros_like(qp)
            s[j, p, 0] = _dot_nt(jnp.where(low, qp, zero), km)
            s[j, p, 1] = _dot_nt(jnp.where(low, zero, qp), km)

    def softmax(j):
        for p in pairs:
            for hh in range(2):
                sc = s[j, p, hh]
                ex = jnp.exp(sc - jnp.max(sc, axis=-1, keepdims=True))
                den[j, p, hh] = jnp.sum(ex, axis=-1, keepdims=True)
                e[j, p, hh] = ex.astype(BF16)

    def mix(j):
        mixm = []
        for p in pairs:
            vm = kvm_ref[0, 0, :, MEM_W + p * LANES:MEM_W + (p + 1) * LANES]
            om = jnp.where(low, _dot(e[j, p, 0], vm) / den[j, p, 0], _dot(e[j, p, 1], vm) / den[j, p, 1])
            gate = c3_ref[0, rows(j), SB_W + MEM_W + p * LANES:SB_W + MEM_W + (p + 1) * LANES].astype(F32)
            mixm.append((om * _silu(gate)).astype(BF16))
        y[j] = _dot(jnp.concatenate([mixg_ref[0, rows(j), :], mixs_ref[0, rows(j), :]] + mixm, axis=1), wout_ref[...])

    def norm(j):
        r = ALPHA * x_ref[0, rows(j), :] + y[j]
        mu = jnp.mean(r, axis=-1, keepdims=True)
        d = r - mu
        var = jnp.mean(d * d, axis=-1, keepdims=True)
        o_ref[0, rows(j), :] = d * lax.rsqrt(var + LN_EPS) * lng_ref[0] + lnb_ref[0]

    _emit_pipelined((scores, softmax, mix, norm), list(range(tm // sub)))


def _out(x, mixg, mixs, c3, kvm, layer, wout, lng, lnb, tm=OUT_TILE, sub=ROW_TILE):
    b, t, _ = x.shape
    return pl.pallas_call(
        functools.partial(_out_kernel, tm=tm, sub=sub),
        grid=(b, t // tm),
        in_specs=[pl.BlockSpec((1, tm, D_MODEL), lambda i, j: (i, j, 0)),
                  pl.BlockSpec((1, tm, GLA_V_W), lambda i, j: (i, j, 0)),
                  pl.BlockSpec((1, tm, SB_W), lambda i, j: (i, j, 0)),
                  pl.BlockSpec((1, tm, C3_COLS), lambda i, j: (i, j, 0)),
                  pl.BlockSpec((1, 1, N_MEM, 2 * MEM_W), lambda i, j: (layer, i, 0, 0)),
                  pl.BlockSpec((1, D_MODEL, D_MODEL), lambda i, j: (layer, 0, 0)),
                  pl.BlockSpec((1, 1, D_MODEL), lambda i, j: (layer, 0, 0)),
                  pl.BlockSpec((1, 1, D_MODEL), lambda i, j: (layer, 0, 0))],
        out_specs=pl.BlockSpec((1, tm, D_MODEL), lambda i, j: (i, j, 0)),
        out_shape=jax.ShapeDtypeStruct((b, t, D_MODEL), F32),
        compiler_params=pltpu.CompilerParams(dimension_semantics=("arbitrary", "arbitrary"),
                                             vmem_limit_bytes=VMEM_LIMIT),
        name="out",
    )(x, mixg, mixs, c3, kvm, wout, lng, lnb)


_W_IN_SEGMENTS = (("gq", GLA_QK_W), ("gk", GLA_QK_W), ("gv", GLA_V_W), ("gg", GLA_V_W), ("ga", GLA_RANK),
                  ("sq", SB_W), ("sk", SB_W), ("sv", SB_W), ("sg", SB_W), ("mq", MEM_W), ("mg", MEM_W))
_W_PROJ_ORDER = ("gq", "gk", "gv", "gg", "sq", "sk", "sv", "sg", "mq", "mg", "ga")
_Q_SCALED = {"gq": GLA_DK ** -0.5, "sq": SB_DH ** -0.5, "mq": MEM_DH ** -0.5}
D_IN = sum(n for _, n in _W_IN_SEGMENTS)


def _wprep_kernel(wt_ref, o_ref):
    src, off = {}, 0
    for name, n in _W_IN_SEGMENTS:
        src[name] = (off, n)
        off += n
    dst = 0
    for name in _W_PROJ_ORDER:
        lo, n = src[name]
        rows = max(n, LANES)
        cols = wt_ref[0, lo:lo + rows, :].T
        if name in _Q_SCALED:
            cols = cols * _Q_SCALED[name]
        if rows > n:
            cols = jnp.where(lax.broadcasted_iota(jnp.int32, cols.shape, 1) < n, cols, 0.0)
        o_ref[0, :, dst:dst + rows] = cols.astype(BF16)
        dst += rows
    assert dst == PROJ_COLS


def _regroup_w_in(wt, tk=256):
    depth, _, d = wt.shape
    return pl.pallas_call(
        _wprep_kernel,
        grid=(depth, d // tk),
        in_specs=[pl.BlockSpec((1, D_IN, tk), lambda l, i: (l, 0, i))],
        out_specs=pl.BlockSpec((1, tk, PROJ_COLS), lambda l, i: (l, i, 0)),
        out_shape=jax.ShapeDtypeStruct((depth, d, PROJ_COLS), BF16),
        compiler_params=pltpu.CompilerParams(dimension_semantics=("arbitrary", "arbitrary"),
                                             vmem_limit_bytes=VMEM_LIMIT),
        name="wprep",
    )(wt)


def kernel(x, mem, w_in, w_alpha2, b_alpha, gla_norm_w, w_mem_kv, w_out, ln_g, ln_b):
    b, t, d = x.shape
    assert d == D_MODEL and mem.shape == (b, N_MEM, d) and w_in.shape == (DEPTH, d, D_IN)
    assert t % GLA_TILE == 0 and t % SB_QTILE == 0 and t % OUT_TILE == 0
    kvm = _memkv(mem.reshape(b * N_MEM, d), w_mem_kv.astype(BF16)).reshape(DEPTH, b, N_MEM, 2 * MEM_W)
    w_proj = _regroup_w_in(jnp.swapaxes(w_in, 1, 2))
    wa2 = jnp.pad(w_alpha2.astype(BF16), ((0, 0), (0, GA_COLS - GLA_RANK), (0, 0)))
    ba = b_alpha.reshape(DEPTH, 1, GLA_QK_W)
    nw = gla_norm_w.reshape(DEPTH, 1, GLA_DV)
    wout = w_out.astype(BF16)
    lng = ln_g.reshape(DEPTH, 1, d)
    lnb = ln_b.reshape(DEPTH, 1, d)
    for l in range(DEPTH):
        gla, sb, c3, ga = _proj(x.reshape(b * t, d), w_proj, l)
        mixg = _gla(gla.reshape(b, t, GLA_COLS), ga.reshape(b, t, GA_COLS), wa2, ba, nw, l)
        c3 = c3.reshape(b, t, C3_COLS)
        mixs = _sb(sb.reshape(b, t, SB_COLS), c3)
        x = _out(x, mixg, mixs, c3, kvm, l, wout, lng, lnb)
    return x
```

```python
import functools

import jax
import jax.numpy as jnp
from jax import lax
from jax.experimental import pallas as pl
from jax.experimental.pallas import tpu as pltpu

F32 = jnp.float32
BF16 = jnp.bfloat16

D_MODEL = 1024
DEPTH = 4
N_MEM = 256
GLA_HEADS = 4
GLA_DK = 64
GLA_DV = 128
GLA_RANK = 16
GLA_GATE_NORM = 16.0
GLA_CHUNK = 64
SB_HEADS = 4
SB_DH = 64
SB_BLOCK = 128
MEM_HEADS = 4
MEM_DH = 64
GLA_QK_W = GLA_HEADS * GLA_DK
GLA_V_W = GLA_HEADS * GLA_DV
SB_W = SB_HEADS * SB_DH
MEM_W = MEM_HEADS * MEM_DH
ALPHA = (2.0 * DEPTH) ** 0.25
LN_EPS = 1e-5
RMS_EPS = 1e-6

LANES = 128
GLA_COLS = 2 * GLA_QK_W + 2 * GLA_V_W
SB_COLS = 3 * SB_W
C3_COLS = SB_W + 2 * MEM_W
GA_COLS = LANES
PROJ_COLS = GLA_COLS + SB_COLS + C3_COLS + GA_COLS
VMEM_LIMIT = 56 * 1024 * 1024
SB_SKIP_LOG = -106.0
SB_HALF = 64
SB_LEAD_KEYS = 2 * SB_BLOCK
SB_QTILE = 1024
GLA_TILE = 1024
ROW_TILE = 512
OUT_TILE = 1024
LOG2E = 1.4426950408889634


def _dot(a, b):
    return jnp.dot(a, b, preferred_element_type=F32)


def _dot_nt(a, b):
    return lax.dot_general(a, b, (((1,), (1,)), ((), ())), preferred_element_type=F32)


def _dot_tn(a, b):
    return lax.dot_general(a, b, (((0,), (0,)), ((), ())), preferred_element_type=F32)


def _split_bf16(x, n):
    parts = []
    for _ in range(n - 1):
        h = x.astype(BF16)
        parts.append(h)
        x = x - h.astype(F32)
    parts.append(x.astype(BF16))
    return parts


def _exp_neg(x):
    return jnp.exp2(x * -LOG2E)


def _log_sigmoid(z):
    return jnp.minimum(z, 0.0) - jnp.log(1.0 + _exp_neg(jnp.abs(z)))


def _silu(g):
    return g / (1.0 + _exp_neg(g))


def _emit_pipelined(stages, items):
    for step in range(len(items) + len(stages) - 1):
        for depth, stage in enumerate(stages):
            j = step - depth
            if 0 <= j < len(items):
                stage(items[j])


def _proj_kernel(x_ref, w_ref, gla_ref, sb_ref, c3_ref, ga_ref):
    xb = x_ref[...].astype(BF16)
    w_ref = w_ref.at[0]
    off = 0
    for ref, width in ((gla_ref, GLA_COLS), (sb_ref, SB_COLS), (c3_ref, C3_COLS), (ga_ref, GA_COLS)):
        step = 512 if width % 512 == 0 else width if width < 512 else 256
        for c in range(0, width, step):
            ref[:, c:c + step] = _dot(xb, w_ref[:, off + c:off + c + step]).astype(BF16)
        off += width


def _proj(x2, w, layer, tm=ROW_TILE):
    n = x2.shape[0]
    outs = (GLA_COLS, SB_COLS, C3_COLS, GA_COLS)
    return pl.pallas_call(
        _proj_kernel,
        grid=(n // tm,),
        in_specs=[pl.BlockSpec((tm, D_MODEL), lambda i: (i, 0)),
                  pl.BlockSpec((1, D_MODEL, PROJ_COLS), lambda i: (layer, 0, 0))],
        out_specs=[pl.BlockSpec((tm, c), lambda i: (i, 0)) for c in outs],
        out_shape=[jax.ShapeDtypeStruct((n, c), BF16) for c in outs],
        compiler_params=pltpu.CompilerParams(dimension_semantics=("arbitrary",),
                                             vmem_limit_bytes=VMEM_LIMIT),
        name="proj",
    )(x2, w)


def _memkv_kernel(m_ref, w_ref, o_ref):
    o_ref[0] = _dot(m_ref[...].astype(BF16), w_ref[0]).astype(BF16)


def _memkv(mem2, w_mkv):
    n = mem2.shape[0]
    return pl.pallas_call(
        _memkv_kernel,
        grid=(DEPTH,),
        in_specs=[pl.BlockSpec((n, D_MODEL), lambda l: (0, 0)),
                  pl.BlockSpec((1, D_MODEL, 2 * MEM_W), lambda l: (l, 0, 0))],
        out_specs=pl.BlockSpec((1, n, 2 * MEM_W), lambda l: (l, 0, 0)),
        out_shape=jax.ShapeDtypeStruct((DEPTH, n, 2 * MEM_W), BF16),
        compiler_params=pltpu.CompilerParams(dimension_semantics=("arbitrary",),
                                             vmem_limit_bytes=VMEM_LIMIT),
        name="memkv",
    )(mem2, w_mkv)


def _gla_kernel(gla_ref, ga_ref, wa2_ref, ba_ref, nw_ref, out_ref, st_ref, *, tt):
    c_sz = GLA_CHUNK

    @pl.when(pl.program_id(1) == 0)
    def _():
        st_ref[...] = jnp.zeros_like(st_ref)

    row = lax.broadcasted_iota(jnp.int32, (c_sz, c_sz), 0)
    col = lax.broadcasted_iota(jnp.int32, (c_sz, c_sz), 1)
    causal = col <= row
    r2 = lax.broadcasted_iota(jnp.int32, (c_sz, 2 * c_sz), 0)
    c2 = lax.broadcasted_iota(jnp.int32, (c_sz, 2 * c_sz), 1) & (c_sz - 1)
    tri_incl2 = jnp.where(c2 <= r2, 1.0, 0.0).astype(BF16)
    low_c = lax.broadcasted_iota(jnp.int32, (c_sz, LANES), 1) < GLA_DK
    chunks = range(tt // c_sz)
    pairs = range(GLA_HEADS // 2)
    rows = [slice(c * c_sz, (c + 1) * c_sz) for c in chunks]
    lanes = [slice(p * LANES, (p + 1) * LANES) for p in pairs]
    v_col = 2 * GLA_QK_W
    g_col = v_col + GLA_V_W

    zz = _dot(ga_ref[0], wa2_ref[0]) + ba_ref[0]
    log_a = _log_sigmoid(zz) * (1.0 / GLA_GATE_NORM)
    hi, lo = _split_bf16(log_a, 2)
    g_cum, kd, decay, q2, a2, upd, inter2 = {}, {}, {}, {}, {}, {}, {}
    state = {p: st_ref[p] for p in pairs}
    norm_w = nw_ref[0] * GLA_DV ** 0.5

    def v_head(c, h):
        return gla_ref[0, rows[c], v_col + h * GLA_DV:v_col + (h + 1) * GLA_DV]

    def cumsum(c):
        g_cum[c] = _dot(tri_incl2, jnp.concatenate([hi[rows[c]], lo[rows[c]]], axis=0))

    def scale(c):
        g_last = g_cum[c][c_sz - 1:c_sz, :]
        q = gla_ref[0, rows[c], 0:GLA_QK_W].astype(F32)
        k = gla_ref[0, rows[c], GLA_QK_W:2 * GLA_QK_W].astype(F32)
        decay[c] = jnp.exp(g_last)
        qg = (q * jnp.exp(g_cum[c])).astype(BF16)
        kgf = k * _exp_neg(g_cum[c])
        kg = kgf.astype(BF16)
        kd[c] = (kgf * decay[c]).astype(BF16)
        for p in pairs:
            qgp = qg[:, lanes[p]]
            q2[c, p] = jnp.concatenate([jnp.where(low_c, qgp, jnp.zeros_like(qgp)),
                                        jnp.where(low_c, jnp.zeros_like(qgp), qgp)], axis=0)
            a2[c, p] = _dot_nt(q2[c, p], kg[:, lanes[p]])

    def update(c):
        for p in pairs:
            kdp = kd[c][:, lanes[p]]
            zero = jnp.zeros_like(kdp)
            upd[c, p] = _dot_tn(jnp.concatenate([v_head(c, 2 * p), v_head(c, 2 * p + 1)], axis=0),
                                jnp.concatenate([jnp.where(low_c, kdp, zero), jnp.where(low_c, zero, kdp)], axis=0))

    def recur(c):
        for p in pairs:
            inter2[c, p] = _dot_nt(q2[c, p], state[p].astype(BF16))
            state[p] = state[p] * decay[c][:, lanes[p]] + upd[c, p]

    def finish(c):
        for p in pairs:
            for hh in range(2):
                h = 2 * p + hh
                a = jnp.where(causal, a2[c, p][hh * c_sz:(hh + 1) * c_sz], 0.0).astype(BF16)
                o = _dot(a, v_head(c, h)) + inter2[c, p][hh * c_sz:(hh + 1) * c_sz]
                ssq = jnp.sum(o * o, axis=-1, keepdims=True)
                gate = gla_ref[0, rows[c], g_col + h * GLA_DV:g_col + (h + 1) * GLA_DV].astype(F32)
                res = o * lax.rsqrt(ssq + GLA_DV * RMS_EPS) * norm_w * _silu(gate)
                out_ref[0, rows[c], h * GLA_DV:(h + 1) * GLA_DV] = res.astype(BF16)

    _emit_pipelined((cumsum, scale, update, recur, finish), list(chunks))
    for p in pairs:
        st_ref[p] = state[p]


def _gla(gla, ga, wa2, ba, nw, layer, tt=GLA_TILE):
    b, t, _ = gla.shape
    return pl.pallas_call(
        functools.partial(_gla_kernel, tt=tt),
        grid=(b, t // tt),
        in_specs=[pl.BlockSpec((1, tt, GLA_COLS), lambda i, j: (i, j, 0)),
                  pl.BlockSpec((1, tt, GA_COLS), lambda i, j: (i, j, 0)),
                  pl.BlockSpec((1, GA_COLS, GLA_QK_W), lambda i, j: (layer, 0, 0)),
                  pl.BlockSpec((1, 1, GLA_QK_W), lambda i, j: (layer, 0, 0)),
                  pl.BlockSpec((1, 1, GLA_DV), lambda i, j: (layer, 0, 0))],
        out_specs=pl.BlockSpec((1, tt, GLA_V_W), lambda i, j: (i, j, 0)),
        out_shape=jax.ShapeDtypeStruct((b, t, GLA_V_W), BF16),
        scratch_shapes=[pltpu.VMEM((GLA_HEADS // 2, GLA_DV, LANES), F32)],
        compiler_params=pltpu.CompilerParams(dimension_semantics=("arbitrary", "arbitrary"),
                                             vmem_limit_bytes=VMEM_LIMIT),
        name="gla",
    )(gla, ga, wa2, ba, nw)


def _sb_kernel(q_ref, k_ref, v_ref, g_ref, o_ref, oacc_ref, lacc_ref):
    hb, kb = SB_HALF, SB_BLOCK
    pairs = SB_HEADS // 2
    halves = SB_QTILE // hb
    q_base = pl.program_id(1) * SB_QTILE
    row = lax.broadcasted_iota(jnp.int32, (hb, kb), 0)
    lane = lax.broadcasted_iota(jnp.int32, (hb, kb), 1)
    low = lane < SB_DH
    rk = lax.broadcasted_iota(jnp.int32, (kb, 2 * kb), 0)
    ck = lax.broadcasted_iota(jnp.int32, (kb, 2 * kb), 1)
    tri_ones = jnp.where((rk > ck) | (ck >= kb), 1.0, 0.0).astype(BF16)

    def pair_rows(m, p):
        st = m * pairs + p
        return slice(st * 2 * hb, (st + 1) * 2 * hb)

    qneg = {}
    for m in range(halves):
        for p in range(pairs):
            qn = -q_ref[0, m * hb:(m + 1) * hb, p * LANES:(p + 1) * LANES]
            zero = jnp.zeros_like(qn)
            qneg[m, p] = jnp.concatenate([jnp.where(low, qn, zero), jnp.where(low, zero, qn)], axis=0)
    def lead_group(start, valid):
        rows = {m: pl.ds(start[m] if isinstance(start[m], int) else pl.multiple_of(start[m], hb), 2 * kb)
                for m in start}
        r4 = lax.broadcasted_iota(jnp.int32, (2 * kb, 2 * kb), 0)
        c4 = lax.broadcasted_iota(jnp.int32, (2 * kb, 2 * kb), 1)
        later = ((r4 > c4) & ((r4 >= kb) == (c4 >= kb))) | ((r4 < kb) & (c4 >= kb))
        tri_pair = jnp.where(later, 1.0, 0.0).astype(BF16)
        keep1 = jnp.where(valid, 1.0, 0.0).astype(BF16)
        keep = jnp.concatenate([keep1, keep1], axis=0)
        log_beta, lf, sums = {}, {}, {}

        def scores(m):
            for p in range(pairs):
                s = _dot_nt(qneg[m, p], k_ref[0, rows[m], p * LANES:(p + 1) * LANES])
                for g in (1, 0):
                    sg = s[:, g * kb:(g + 1) * kb].astype(BF16)
                    lsn = _log_sigmoid(sg)
                    log_beta[m, p, g] = lsn - sg
                    lf[m, p, g] = lsn * keep if g == 1 else lsn

        def suffix_sums(m):
            lhs = [jnp.concatenate([lf[m, p, 1], lf[m, p, 0]], axis=1) for p in range(pairs)]
            sums[m] = _dot(jnp.concatenate(lhs, axis=0), tri_pair)

        def weights(m):
            for p in range(pairs):
                part = sums[m][p * 2 * hb:(p + 1) * 2 * hb]
                w1 = jnp.exp(part[:, :kb].astype(BF16) + log_beta[m, p, 1]) * keep
                w0 = jnp.exp(part[:, kb:].astype(BF16) + log_beta[m, p, 0])
                total = part[:, kb:kb + 1] + lf[m, p, 0][:, 0:1].astype(F32)
                lacc_ref[pair_rows(m, p)] = jnp.broadcast_to(total, (2 * hb, kb))
                oacc_ref[pair_rows(m, p)] = _dot(jnp.concatenate([w0, w1], axis=1),
                                                 v_ref[0, rows[m], p * LANES:(p + 1) * LANES])

        _emit_pipelined((scores, suffix_sums, weights), list(start))

    def group(start, n, valid):
        order = list(reversed(range(n)))
        rows = {m: pl.ds(start[m] if isinstance(start[m], int) else pl.multiple_of(start[m], hb), n * kb)
                for m in start}
        log_beta, lf, sums = {}, {}, {}
        keep1 = jnp.where(valid, 1.0, 0.0).astype(BF16)
        keep = jnp.concatenate([keep1, keep1], axis=0)
        parts = [(p, g) for g in order for p in range(pairs)]

        def scores(m):
            s = {p: _dot_nt(qneg[m, p], k_ref[0, rows[m], p * LANES:(p + 1) * LANES])
                 for p in range(pairs)}
            for p, g in parts:
                sg = s[p][:, g * kb:(g + 1) * kb].astype(BF16)
                lsn = _log_sigmoid(sg)
                log_beta[m, p, g] = lsn - sg
                lf[m, p, g] = lsn * keep if g == n - 1 else lsn

        def suffix_sums(m):
            sums[m] = _dot(jnp.concatenate([lf[m, p, g] for p, g in parts], axis=0), tri_ones)

        def weights(m):
            acc = {p: lacc_ref[pair_rows(m, p)] for p in range(pairs)}
            ws = {p: [None] * n for p in range(pairs)}
            for idx, (p, g) in enumerate(parts):
                part = sums[m][idx * 2 * hb:(idx + 1) * 2 * hb]
                w = jnp.exp((part[:, :kb] + acc[p]).astype(BF16) + log_beta[m, p, g])
                ws[p][g] = w * keep if g == n - 1 else w
                acc[p] = acc[p] + part[:, kb:]
            for p in range(pairs):
                lacc_ref[pair_rows(m, p)] = acc[p]
                oacc_ref[pair_rows(m, p)] += _dot(jnp.concatenate(ws[p], axis=1),
                                                  v_ref[0, rows[m], p * LANES:(p + 1) * LANES])

        _emit_pipelined((scores, suffix_sums, weights), list(start))

    def sweep(m, bound):
        per_m = 2 * pairs * hb

        def live():
            return jnp.max(lacc_ref[m * per_m:(m + 1) * per_m])

        def cond(c):
            return jnp.logical_and(c[0] > 0, c[1] > SB_SKIP_LOG)

        def body(c):
            first = jnp.maximum(c[0] - kb, 0)
            group({m: first}, 1, lane < c[0] - first)
            return first, live()

        lax.while_loop(cond, body, (bound, live()))

    def lead_and_sweep(base, ms):
        lead = {m: base + (m + 1) * hb - SB_LEAD_KEYS for m in ms}
        lead_group(lead, lane < row + (kb - hb))
        rows_all = slice(ms[0] * 2 * pairs * hb, (ms[-1] + 1) * 2 * pairs * hb)

        @pl.when(jnp.max(lacc_ref[rows_all]) > SB_SKIP_LOG)
        def _():
            for m in ms:
                sweep(m, lead[m])

    short = (SB_LEAD_KEYS - hb) // hb

    @pl.when(q_base > 0)
    def _():
        lead_and_sweep(q_base, list(range(halves)))

    @pl.when(q_base == 0)
    def _():
        for m in range(short):
            oacc_ref[m * 2 * pairs * hb:(m + 1) * 2 * pairs * hb] = jnp.zeros((2 * pairs * hb, LANES), F32)
            lacc_ref[m * 2 * pairs * hb:(m + 1) * 2 * pairs * hb] = jnp.zeros((2 * pairs * hb, LANES), F32)
            first = max(m * hb + hb - kb, 0)
            group({m: first}, 1, lane < row + (m * hb - first))
            sweep(m, first)
        lead_and_sweep(0, list(range(short, halves)))

    for m in range(halves):
        for p in range(pairs):
            both = oacc_ref[pair_rows(m, p)]
            o = jnp.where(low, both[:hb], both[hb:])
            gate = g_ref[0, m * hb:(m + 1) * hb, p * LANES:(p + 1) * LANES].astype(F32)
            o_ref[0, m * hb:(m + 1) * hb, p * LANES:(p + 1) * LANES] = (o * _silu(gate)).astype(BF16)


def _sb(sb, c3):
    b, t, _ = sb.shape
    n_streams = (SB_QTILE // SB_HALF) * SB_HEADS
    return pl.pallas_call(
        _sb_kernel,
        grid=(b, t // SB_QTILE),
        in_specs=[pl.BlockSpec((1, SB_QTILE, SB_W), lambda bi, i: (bi, i, 0)),
                  pl.BlockSpec((1, t, SB_W), lambda bi, i: (bi, 0, 1)),
                  pl.BlockSpec((1, t, SB_W), lambda bi, i: (bi, 0, 2)),
                  pl.BlockSpec((1, SB_QTILE, SB_W), lambda bi, i: (bi, i, 0))],
        out_specs=pl.BlockSpec((1, SB_QTILE, SB_W), lambda bi, i: (bi, i, 0)),
        out_shape=jax.ShapeDtypeStruct((b, t, SB_W), BF16),
        scratch_shapes=[pltpu.VMEM((n_streams * SB_HALF, LANES), F32),
                        pltpu.VMEM((n_streams * SB_HALF, LANES), F32)],
        compiler_params=pltpu.CompilerParams(dimension_semantics=("arbitrary", "arbitrary"),
                                             vmem_limit_bytes=VMEM_LIMIT),
        name="sb",
    )(sb, sb, sb, c3)


def _out_kernel(x_ref, mixg_ref, mixs_ref, c3_ref, kvm_ref, wout_ref, lng_ref, lnb_ref, o_ref, *, tm, sub):
    wout_ref = wout_ref.at[0]
    low = lax.broadcasted_iota(jnp.int32, (sub, LANES), 1) < MEM_DH
    pairs = range(MEM_HEADS // 2)
    s, e, den, y = {}, {}, {}, {}

    def rows(j):
        return slice(j * sub, (j + 1) * sub)

    def scores(j):
        for p in pairs:
            qp = c3_ref[0, rows(j), SB_W + p * LANES:SB_W + (p + 1) * LANES]
            km = kvm_ref[0, 0, :, p * LANES:(p + 1) * LANES]
            zero = jnp.zeros_like(qp)
            s[j, p, 0] = _dot_nt(jnp.where(low, qp, zero), km)
            s[j, p, 1] = _dot_nt(jnp.where(low, zero, qp), km)

    def softmax(j):
        for p in pairs:
            for hh in range(2):
                sc = s[j, p, hh]
                ex = jnp.exp(sc - jnp.max(sc, axis=-1, keepdims=True))
                den[j, p, hh] = jnp.sum(ex, axis=-1, keepdims=True)
                e[j, p, hh] = ex.astype(BF16)

    def mix(j):
        mixm = []
        for p in pairs:
            vm = kvm_ref[0, 0, :, MEM_W + p * LANES:MEM_W + (p + 1) * LANES]
            om = jnp.where(low, _dot(e[j, p, 0], vm) / den[j, p, 0], _dot(e[j, p, 1], vm) / den[j, p, 1])
            gate = c3_ref[0, rows(j), SB_W + MEM_W + p * LANES:SB_W + MEM_W + (p + 1) * LANES].astype(F32)
            mixm.append((om * _silu(gate)).astype(BF16))
        y[j] = _dot(jnp.concatenate([mixg_ref[0, rows(j), :], mixs_ref[0, rows(j), :]] + mixm, axis=1), wout_ref[...])

    def norm(j):
        r = ALPHA * x_ref[0, rows(j), :] + y[j]
        mu = jnp.mean(r, axis=-1, keepdims=True)
        d = r - mu
        var = jnp.mean(d * d, axis=-1, keepdims=True)
        o_ref[0, rows(j), :] = d * lax.rsqrt(var + LN_EPS) * lng_ref[0] + lnb_ref[0]

    _emit_pipelined((scores, softmax, mix, norm), list(range(tm // sub)))


def _out(x, mixg, mixs, c3, kvm, layer, wout, lng, lnb, tm=OUT_TILE, sub=ROW_TILE):
    b, t, _ = x.shape
    return pl.pallas_call(
        functools.partial(_out_kernel, tm=tm, sub=sub),
        grid=(b, t // tm),
        in_specs=[pl.BlockSpec((1, tm, D_MODEL), lambda i, j: (i, j, 0)),
                  pl.BlockSpec((1, tm, GLA_V_W), lambda i, j: (i, j, 0)),
                  pl.BlockSpec((1, tm, SB_W), lambda i, j: (i, j, 0)),
                  pl.BlockSpec((1, tm, C3_COLS), lambda i, j: (i, j, 0)),
                  pl.BlockSpec((1, 1, N_MEM, 2 * MEM_W), lambda i, j: (layer, i, 0, 0)),
                  pl.BlockSpec((1, D_MODEL, D_MODEL), lambda i, j: (layer, 0, 0)),
                  pl.BlockSpec((1, 1, D_MODEL), lambda i, j: (layer, 0, 0)),
                  pl.BlockSpec((1, 1, D_MODEL), lambda i, j: (layer, 0, 0))],
        out_specs=pl.BlockSpec((1, tm, D_MODEL), lambda i, j: (i, j, 0)),
        out_shape=jax.ShapeDtypeStruct((b, t, D_MODEL), F32),
        compiler_params=pltpu.CompilerParams(dimension_semantics=("arbitrary", "arbitrary"),
                                             vmem_limit_bytes=VMEM_LIMIT),
        name="out",
    )(x, mixg, mixs, c3, kvm, wout, lng, lnb)


_W_IN_SEGMENTS = (("gq", GLA_QK_W), ("gk", GLA_QK_W), ("gv", GLA_V_W), ("gg", GLA_V_W), ("ga", GLA_RANK),
                  ("sq", SB_W), ("sk", SB_W), ("sv", SB_W), ("sg", SB_W), ("mq", MEM_W), ("mg", MEM_W))
_W_PROJ_ORDER = ("gq", "gk", "gv", "gg", "sq", "sk", "sv", "sg", "mq", "mg", "ga")
_Q_SCALED = {"gq": GLA_DK ** -0.5, "sq": SB_DH ** -0.5, "mq": MEM_DH ** -0.5}
D_IN = sum(n for _, n in _W_IN_SEGMENTS)


def _wprep_kernel(wt_ref, o_ref):
    src, off = {}, 0
    for name, n in _W_IN_SEGMENTS:
        src[name] = (off, n)
        off += n
    dst = 0
    for name in _W_PROJ_ORDER:
        lo, n = src[name]
        rows = max(n, LANES)
        cols = wt_ref[0, lo:lo + rows, :].T
        if name in _Q_SCALED:
            cols = cols * _Q_SCALED[name]
        if rows > n:
            cols = jnp.where(lax.broadcasted_iota(jnp.int32, cols.shape, 1) < n, cols, 0.0)
        o_ref[0, :, dst:dst + rows] = cols.astype(BF16)
        dst += rows
    assert dst == PROJ_COLS


def _regroup_w_in(wt, tk=256):
    depth, _, d = wt.shape
    return pl.pallas_call(
        _wprep_kernel,
        grid=(depth, d // tk),
        in_specs=[pl.BlockSpec((1, D_IN, tk), lambda l, i: (l, 0, i))],
        out_specs=pl.BlockSpec((1, tk, PROJ_COLS), lambda l, i: (l, i, 0)),
        out_shape=jax.ShapeDtypeStruct((depth, d, PROJ_COLS), BF16),
        compiler_params=pltpu.CompilerParams(dimension_semantics=("arbitrary", "arbitrary"),
                                             vmem_limit_bytes=VMEM_LIMIT),
        name="wprep",
    )(wt)


def kernel(x, mem, w_in, w_alpha2, b_alpha, gla_norm_w, w_mem_kv, w_out, ln_g, ln_b):
    b, t, d = x.shape
    assert d == D_MODEL and mem.shape == (b, N_MEM, d) and w_in.shape == (DEPTH, d, D_IN)
    assert t % GLA_TILE == 0 and t % SB_QTILE == 0 and t % OUT_TILE == 0
    kvm = _memkv(mem.reshape(b * N_MEM, d), w_mem_kv.astype(BF16)).reshape(DEPTH, b, N_MEM, 2 * MEM_W)
    w_proj = _regroup_w_in(jnp.swapaxes(w_in, 1, 2))
    wa2 = jnp.pad(w_alpha2.astype(BF16), ((0, 0), (0, GA_COLS - GLA_RANK), (0, 0)))
    ba = b_alpha.reshape(DEPTH, 1, GLA_QK_W)
    nw = gla_norm_w.reshape(DEPTH, 1, GLA_DV)
    wout = w_out.astype(BF16)
    lng = ln_g.reshape(DEPTH, 1, d)
    lnb = ln_b.reshape(DEPTH, 1, d)
    for l in range(DEPTH):
        gla, sb, c3, ga = _proj(x.reshape(b * t, d), w_proj, l)
        mixg = _gla(gla.reshape(b, t, GLA_COLS), ga.reshape(b, t, GA_COLS), wa2, ba, nw, l)
        c3 = c3.reshape(b, t, C3_COLS)
        mixs = _sb(sb.reshape(b, t, SB_COLS), c3)
        x = _out(x, mixg, mixs, c3, kvm, l, wout, lng, lnb)
    return x
```

```python
import functools

import jax
import jax.numpy as jnp
from jax import lax
from jax.experimental import pallas as pl
from jax.experimental.pallas import tpu as pltpu

F32 = jnp.float32
BF16 = jnp.bfloat16

D_MODEL = 1024
DEPTH = 4
N_MEM = 256
GLA_HEADS = 4
GLA_DK = 64
GLA_DV = 128
GLA_RANK = 16
GLA_GATE_NORM = 16.0
GLA_CHUNK = 64
SB_HEADS = 4
SB_DH = 64
SB_BLOCK = 128
MEM_HEADS = 4
MEM_DH = 64
GLA_QK_W = GLA_HEADS * GLA_DK
GLA_V_W = GLA_HEADS * GLA_DV
SB_W = SB_HEADS * SB_DH
MEM_W = MEM_HEADS * MEM_DH
ALPHA = (2.0 * DEPTH) ** 0.25
LN_EPS = 1e-5
RMS_EPS = 1e-6

LANES = 128
GLA_COLS = 2 * GLA_QK_W + 2 * GLA_V_W
SB_COLS = 3 * SB_W
C3_COLS = SB_W + 2 * MEM_W
GA_COLS = LANES
PROJ_COLS = GLA_COLS + SB_COLS + C3_COLS + GA_COLS
VMEM_LIMIT = 56 * 1024 * 1024
SB_SKIP_LOG = -106.0
SB_HALF = 64
SB_LEAD_KEYS = 2 * SB_BLOCK
SB_QTILE = 2048
GLA_TILE = 2048
ROW_TILE = 512
OUT_TILE = 1024
LOG2E = 1.4426950408889634


def _dot(a, b):
    return jnp.dot(a, b, preferred_element_type=F32)


def _dot_nt(a, b):
    return lax.dot_general(a, b, (((1,), (1,)), ((), ())), preferred_element_type=F32)


def _dot_tn(a, b):
    return lax.dot_general(a, b, (((0,), (0,)), ((), ())), preferred_element_type=F32)


def _split_bf16(x, n):
    parts = []
    for _ in range(n - 1):
        h = x.astype(BF16)
        parts.append(h)
        x = x - h.astype(F32)
    parts.append(x.astype(BF16))
    return parts


def _exp_neg(x):
    return jnp.exp2(x * -LOG2E)


def _log_sigmoid(z):
    return jnp.minimum(z, 0.0) - jnp.log(1.0 + _exp_neg(jnp.abs(z)))


def _silu(g):
    return g / (1.0 + _exp_neg(g))


def _emit_pipelined(stages, items):
    for step in range(len(items) + len(stages) - 1):
        for depth, stage in enumerate(stages):
            j = step - depth
            if 0 <= j < len(items):
                stage(items[j])


def _proj_kernel(x_ref, w_ref, gla_ref, sb_ref, c3_ref, ga_ref):
    xb = x_ref[...].astype(BF16)
    w_ref = w_ref.at[0]
    off = 0
    for ref, width in ((gla_ref, GLA_COLS), (sb_ref, SB_COLS), (c3_ref, C3_COLS), (ga_ref, GA_COLS)):
        step = 512 if width % 512 == 0 else width if width < 512 else 256
        for c in range(0, width, step):
            ref[:, c:c + step] = _dot(xb, w_ref[:, off + c:off + c + step]).astype(BF16)
        off += width


def _proj(x2, w, layer, tm=ROW_TILE):
    n = x2.shape[0]
    outs = (GLA_COLS, SB_COLS, C3_COLS, GA_COLS)
    return pl.pallas_call(
        _proj_kernel,
        grid=(n // tm,),
        in_specs=[pl.BlockSpec((tm, D_MODEL), lambda i: (i, 0)),
                  pl.BlockSpec((1, D_MODEL, PROJ_COLS), lambda i: (layer, 0, 0))],
        out_specs=[pl.BlockSpec((tm, c), lambda i: (i, 0)) for c in outs],
        out_shape=[jax.ShapeDtypeStruct((n, c), BF16) for c in outs],
        compiler_params=pltpu.CompilerParams(dimension_semantics=("arbitrary",),
                                             vmem_limit_bytes=VMEM_LIMIT),
        name="proj",
    )(x2, w)


def _memkv_kernel(m_ref, w_ref, o_ref):
    o_ref[0] = _dot(m_ref[...].astype(BF16), w_ref[0]).astype(BF16)


def _memkv(mem2, w_mkv):
    n = mem2.shape[0]
    return pl.pallas_call(
        _memkv_kernel,
        grid=(DEPTH,),
        in_specs=[pl.BlockSpec((n, D_MODEL), lambda l: (0, 0)),
                  pl.BlockSpec((1, D_MODEL, 2 * MEM_W), lambda l: (l, 0, 0))],
        out_specs=pl.BlockSpec((1, n, 2 * MEM_W), lambda l: (l, 0, 0)),
        out_shape=jax.ShapeDtypeStruct((DEPTH, n, 2 * MEM_W), BF16),
        compiler_params=pltpu.CompilerParams(dimension_semantics=("arbitrary",),
                                             vmem_limit_bytes=VMEM_LIMIT),
        name="memkv",
    )(mem2, w_mkv)


def _gla_kernel(gla_ref, ga_ref, wa2_ref, ba_ref, nw_ref, out_ref, st_ref, *, tt):
    c_sz = GLA_CHUNK

    @pl.when(pl.program_id(1) == 0)
    def _():
        st_ref[...] = jnp.zeros_like(st_ref)

    row = lax.broadcasted_iota(jnp.int32, (c_sz, c_sz), 0)
    col = lax.broadcasted_iota(jnp.int32, (c_sz, c_sz), 1)
    causal = col <= row
    r2 = lax.broadcasted_iota(jnp.int32, (c_sz, 2 * c_sz), 0)
    c2 = lax.broadcasted_iota(jnp.int32, (c_sz, 2 * c_sz), 1) & (c_sz - 1)
    tri_incl2 = jnp.where(c2 <= r2, 1.0, 0.0).astype(BF16)
    low_c = lax.broadcasted_iota(jnp.int32, (c_sz, LANES), 1) < GLA_DK
    chunks = range(tt // c_sz)
    pairs = range(GLA_HEADS // 2)
    rows = [slice(c * c_sz, (c + 1) * c_sz) for c in chunks]
    lanes = [slice(p * LANES, (p + 1) * LANES) for p in pairs]
    v_col = 2 * GLA_QK_W
    g_col = v_col + GLA_V_W

    zz = _dot(ga_ref[0], wa2_ref[0]) + ba_ref[0]
    log_a = _log_sigmoid(zz) * (1.0 / GLA_GATE_NORM)
    hi, lo = _split_bf16(log_a, 2)
    g_cum, kd, decay, q2, a2, upd, inter2 = {}, {}, {}, {}, {}, {}, {}
    state = {p: st_ref[p] for p in pairs}
    norm_w = nw_ref[0] * GLA_DV ** 0.5

    def v_head(c, h):
        return gla_ref[0, rows[c], v_col + h * GLA_DV:v_col + (h + 1) * GLA_DV]

    def cumsum(c):
        g_cum[c] = _dot(tri_incl2, jnp.concatenate([hi[rows[c]], lo[rows[c]]], axis=0))

    def scale(c):
        g_last = g_cum[c][c_sz - 1:c_sz, :]
        q = gla_ref[0, rows[c], 0:GLA_QK_W].astype(F32)
        k = gla_ref[0, rows[c], GLA_QK_W:2 * GLA_QK_W].astype(F32)
        decay[c] = jnp.exp(g_last)
        qg = (q * jnp.exp(g_cum[c])).astype(BF16)
        kgf = k * _exp_neg(g_cum[c])
        kg = kgf.astype(BF16)
        kd[c] = (kgf * decay[c]).astype(BF16)
        for p in pairs:
            qgp = qg[:, lanes[p]]
            q2[c, p] = jnp.concatenate([jnp.where(low_c, qgp, jnp.zeros_like(qgp)),
                                        jnp.where(low_c, jnp.zeros_like(qgp), qgp)], axis=0)
            a2[c, p] = _dot_nt(q2[c, p], kg[:, lanes[p]])

    def update(c):
        for p in pairs:
            kdp = kd[c][:, lanes[p]]
            zero = jnp.zeros_like(kdp)
            upd[c, p] = _dot_tn(jnp.concatenate([v_head(c, 2 * p), v_head(c, 2 * p + 1)], axis=0),
                                jnp.concatenate([jnp.where(low_c, kdp, zero), jnp.where(low_c, zero, kdp)], axis=0))

    def recur(c):
        for p in pairs:
            inter2[c, p] = _dot_nt(q2[c, p], state[p].astype(BF16))
            state[p] = state[p] * decay[c][:, lanes[p]] + upd[c, p]

    def finish(c):
        for p in pairs:
            for hh in range(2):
                h = 2 * p + hh
                a = jnp.where(causal, a2[c, p][hh * c_sz:(hh + 1) * c_sz], 0.0).astype(BF16)
                o = _dot(a, v_head(c, h)) + inter2[c, p][hh * c_sz:(hh + 1) * c_sz]
                ssq = jnp.sum(o * o, axis=-1, keepdims=True)
                gate = gla_ref[0, rows[c], g_col + h * GLA_DV:g_col + (h + 1) * GLA_DV].astype(F32)
                res = o * lax.rsqrt(ssq + GLA_DV * RMS_EPS) * norm_w * _silu(gate)
                out_ref[0, rows[c], h * GLA_DV:(h + 1) * GLA_DV] = res.astype(BF16)

    _emit_pipelined((cumsum, scale, update, recur, finish), list(chunks))
    for p in pairs:
        st_ref[p] = state[p]


def _gla(gla, ga, wa2, ba, nw, layer, tt=GLA_TILE):
    b, t, _ = gla.shape
    return pl.pallas_call(
        functools.partial(_gla_kernel, tt=tt),
        grid=(b, t // tt),
        in_specs=[pl.BlockSpec((1, tt, GLA_COLS), lambda i, j: (i, j, 0)),
                  pl.BlockSpec((1, tt, GA_COLS), lambda i, j: (i, j, 0)),
                  pl.BlockSpec((1, GA_COLS, GLA_QK_W), lambda i, j: (layer, 0, 0)),
                  pl.BlockSpec((1, 1, GLA_QK_W), lambda i, j: (layer, 0, 0)),
                  pl.BlockSpec((1, 1, GLA_DV), lambda i, j: (layer, 0, 0))],
        out_specs=pl.BlockSpec((1, tt, GLA_V_W), lambda i, j: (i, j, 0)),
        out_shape=jax.ShapeDtypeStruct((b, t, GLA_V_W), BF16),
        scratch_shapes=[pltpu.VMEM((GLA_HEADS // 2, GLA_DV, LANES), F32)],
        compiler_params=pltpu.CompilerParams(dimension_semantics=("arbitrary", "arbitrary"),
                                             vmem_limit_bytes=VMEM_LIMIT),
        name="gla",
    )(gla, ga, wa2, ba, nw)


def _sb_kernel(q_ref, k_ref, v_ref, g_ref, o_ref, oacc_ref, lacc_ref):
    hb, kb = SB_HALF, SB_BLOCK
    pairs = SB_HEADS // 2
    halves = SB_QTILE // hb
    q_base = pl.program_id(1) * SB_QTILE
    row = lax.broadcasted_iota(jnp.int32, (hb, kb), 0)
    lane = lax.broadcasted_iota(jnp.int32, (hb, kb), 1)
    low = lane < SB_DH
    rk = lax.broadcasted_iota(jnp.int32, (kb, 2 * kb), 0)
    ck = lax.broadcasted_iota(jnp.int32, (kb, 2 * kb), 1)
    tri_ones = jnp.where((rk > ck) | (ck >= kb), 1.0, 0.0).astype(BF16)

    def pair_rows(m, p):
        st = m * pairs + p
        return slice(st * 2 * hb, (st + 1) * 2 * hb)

    qneg = {}
    for m in range(halves):
        for p in range(pairs):
            qn = -q_ref[0, m * hb:(m + 1) * hb, p * LANES:(p + 1) * LANES]
            zero = jnp.zeros_like(qn)
            qneg[m, p] = jnp.concatenate([jnp.where(low, qn, zero), jnp.where(low, zero, qn)], axis=0)
    def lead_group(start, valid):
        rows = {m: pl.ds(start[m] if isinstance(start[m], int) else pl.multiple_of(start[m], hb), 2 * kb)
                for m in start}
        r4 = lax.broadcasted_iota(jnp.int32, (2 * kb, 2 * kb), 0)
        c4 = lax.broadcasted_iota(jnp.int32, (2 * kb, 2 * kb), 1)
        later = ((r4 > c4) & ((r4 >= kb) == (c4 >= kb))) | ((r4 < kb) & (c4 >= kb))
        tri_pair = jnp.where(later, 1.0, 0.0).astype(BF16)
        keep1 = jnp.where(valid, 1.0, 0.0).astype(BF16)
        keep = jnp.concatenate([keep1, keep1], axis=0)
        log_beta, lf, sums = {}, {}, {}

        def scores(m):
            for p in range(pairs):
                s = _dot_nt(qneg[m, p], k_ref[0, rows[m], p * LANES:(p + 1) * LANES])
                for g in (1, 0):
                    sg = s[:, g * kb:(g + 1) * kb].astype(BF16)
                    lsn = _log_sigmoid(sg)
                    log_beta[m, p, g] = lsn - sg
                    lf[m, p, g] = lsn * keep if g == 1 else lsn

        def suffix_sums(m):
            lhs = [jnp.concatenate([lf[m, p, 1], lf[m, p, 0]], axis=1) for p in range(pairs)]
            sums[m] = _dot(jnp.concatenate(lhs, axis=0), tri_pair)

        def weights(m):
            for p in range(pairs):
                part = sums[m][p * 2 * hb:(p + 1) * 2 * hb]
                w1 = jnp.exp(part[:, :kb].astype(BF16) + log_beta[m, p, 1]) * keep
                w0 = jnp.exp(part[:, kb:].astype(BF16) + log_beta[m, p, 0])
                total = part[:, kb:kb + 1] + lf[m, p, 0][:, 0:1].astype(F32)
                lacc_ref[pair_rows(m, p)] = jnp.broadcast_to(total, (2 * hb, kb))
                oacc_ref[pair_rows(m, p)] = _dot(jnp.concatenate([w0, w1], axis=1),
                                                 v_ref[0, rows[m], p * LANES:(p + 1) * LANES])

        _emit_pipelined((scores, suffix_sums, weights), list(start))

    def group(start, n, valid):
        order = list(reversed(range(n)))
        rows = {m: pl.ds(start[m] if isinstance(start[m], int) else pl.multiple_of(start[m], hb), n * kb)
                for m in start}
        log_beta, lf, sums = {}, {}, {}
        keep1 = jnp.where(valid, 1.0, 0.0).astype(BF16)
        keep = jnp.concatenate([keep1, keep1], axis=0)
        parts = [(p, g) for g in order for p in range(pairs)]

        def scores(m):
            s = {p: _dot_nt(qneg[m, p], k_ref[0, rows[m], p * LANES:(p + 1) * LANES])
                 for p in range(pairs)}
            for p, g in parts:
                sg = s[p][:, g * kb:(g + 1) * kb].astype(BF16)
                lsn = _log_sigmoid(sg)
                log_beta[m, p, g] = lsn - sg
                lf[m, p, g] = lsn * keep if g == n - 1 else lsn

        def suffix_sums(m):
            sums[m] = _dot(jnp.concatenate([lf[m, p, g] for p, g in parts], axis=0), tri_ones)

        def weights(m):
            acc = {p: lacc_ref[pair_rows(m, p)] for p in range(pairs)}
            ws = {p: [None] * n for p in range(pairs)}
            for idx, (p, g) in enumerate(parts):
                part = sums[m][idx * 2 * hb:(idx + 1) * 2 * hb]
                w = jnp.exp((part[:, :kb] + acc[p]).astype(BF16) + log_beta[m, p, g])
                ws[p][g] = w * keep if g == n - 1 else w
                acc[p] = acc[p] + part[:, kb:]
            for p in range(pairs):
                lacc_ref[pair_rows(m, p)] = acc[p]
                oacc_ref[pair_rows(m, p)] += _dot(jnp.concatenate(ws[p], axis=1),
                                                  v_ref[0, rows[m], p * LANES:(p + 1) * LANES])

        _emit_pipelined((scores, suffix_sums, weights), list(start))

    def sweep(m, bound):
        per_m = 2 * pairs * hb

        def live():
            return jnp.max(lacc_ref[m * per_m:(m + 1) * per_m])

        def cond(c):
            return jnp.logical_and(c[0] > 0, c[1] > SB_SKIP_LOG)

        def body(c):
            first = jnp.maximum(c[0] - kb, 0)
            group({m: first}, 1, lane < c[0] - first)
            return first, live()

        lax.while_loop(cond, body, (bound, live()))

    def lead_and_sweep(base, ms):
        lead = {m: base + (m + 1) * hb - SB_LEAD_KEYS for m in ms}
        lead_group(lead, lane < row + (kb - hb))
        rows_all = slice(ms[0] * 2 * pairs * hb, (ms[-1] + 1) * 2 * pairs * hb)

        @pl.when(jnp.max(lacc_ref[rows_all]) > SB_SKIP_LOG)
        def _():
            for m in ms:
                sweep(m, lead[m])

    short = (SB_LEAD_KEYS - hb) // hb

    @pl.when(q_base > 0)
    def _():
        lead_and_sweep(q_base, list(range(halves)))

    @pl.when(q_base == 0)
    def _():
        for m in range(short):
            oacc_ref[m * 2 * pairs * hb:(m + 1) * 2 * pairs * hb] = jnp.zeros((2 * pairs * hb, LANES), F32)
            lacc_ref[m * 2 * pairs * hb:(m + 1) * 2 * pairs * hb] = jnp.zeros((2 * pairs * hb, LANES), F32)
            first = max(m * hb + hb - kb, 0)
            group({m: first}, 1, lane < row + (m * hb - first))
            sweep(m, first)
        lead_and_sweep(0, list(range(short, halves)))

    for m in range(halves):
        for p in range(pairs):
            both = oacc_ref[pair_rows(m, p)]
            o = jnp.where(low, both[:hb], both[hb:])
            gate = g_ref[0, m * hb:(m + 1) * hb, p * LANES:(p + 1) * LANES].astype(F32)
            o_ref[0, m * hb:(m + 1) * hb, p * LANES:(p + 1) * LANES] = (o * _silu(gate)).astype(BF16)


def _sb(sb, c3):
    b, t, _ = sb.shape
    n_streams = (SB_QTILE // SB_HALF) * SB_HEADS
    return pl.pallas_call(
        _sb_kernel,
        grid=(b, t // SB_QTILE),
        in_specs=[pl.BlockSpec((1, SB_QTILE, SB_W), lambda bi, i: (bi, i, 0)),
                  pl.BlockSpec((1, t, SB_W), lambda bi, i: (bi, 0, 1)),
                  pl.BlockSpec((1, t, SB_W), lambda bi, i: (bi, 0, 2)),
                  pl.BlockSpec((1, SB_QTILE, SB_W), lambda bi, i: (bi, i, 0))],
        out_specs=pl.BlockSpec((1, SB_QTILE, SB_W), lambda bi, i: (bi, i, 0)),
        out_shape=jax.ShapeDtypeStruct((b, t, SB_W), BF16),
        scratch_shapes=[pltpu.VMEM((n_streams * SB_HALF, LANES), F32),
                        pltpu.VMEM((n_streams * SB_HALF, LANES), F32)],
        compiler_params=pltpu.CompilerParams(dimension_semantics=("arbitrary", "arbitrary"),
                                             vmem_limit_bytes=VMEM_LIMIT),
        name="sb",
    )(sb, sb, sb, c3)


def _out_kernel(x_ref, mixg_ref, mixs_ref, c3_ref, kvm_ref, wout_ref, lng_ref, lnb_ref, o_ref, *, tm, sub):
    wout_ref = wout_ref.at[0]
    low = lax.broadcasted_iota(jnp.int32, (sub, LANES), 1) < MEM_DH
    pairs = range(MEM_HEADS // 2)
    s, e, den, y = {}, {}, {}, {}

    def rows(j):
        return slice(j * sub, (j + 1) * sub)

    def scores(j):
        for p in pairs:
            qp = c3_ref[0, rows(j), SB_W + p * LANES:SB_W + (p + 1) * LANES]
            km = kvm_ref[0, 0, :, p * LANES:(p + 1) * LANES]
            zero = jnp.zeros_like(qp)
            s[j, p, 0] = _dot_nt(jnp.where(low, qp, zero), km)
            s[j, p, 1] = _dot_nt(jnp.where(low, zero, qp), km)

    def softmax(j):
        for p in pairs:
            for hh in range(2):
                sc = s[j, p, hh]
                ex = jnp.exp(sc - jnp.max(sc, axis=-1, keepdims=True))
                den[j, p, hh] = jnp.sum(ex, axis=-1, keepdims=True)
                e[j, p, hh] = ex.astype(BF16)

    def mix(j):
        mixm = []
        for p in pairs:
            vm = kvm_ref[0, 0, :, MEM_W + p * LANES:MEM_W + (p + 1) * LANES]
            om = jnp.where(low, _dot(e[j, p, 0], vm) / den[j, p, 0], _dot(e[j, p, 1], vm) / den[j, p, 1])
            gate = c3_ref[0, rows(j), SB_W + MEM_W + p * LANES:SB_W + MEM_W + (p + 1) * LANES].astype(F32)
            mixm.append((om * _silu(gate)).astype(BF16))
        y[j] = _dot(jnp.concatenate([mixg_ref[0, rows(j), :], mixs_ref[0, rows(j), :]] + mixm, axis=1), wout_ref[...])

    def norm(j):
        r = ALPHA * x_ref[0, rows(j), :] + y[j]
        mu = jnp.mean(r, axis=-1, keepdims=True)
        d = r - mu
        var = jnp.mean(d * d, axis=-1, keepdims=True)
        o_ref[0, rows(j), :] = d * lax.rsqrt(var + LN_EPS) * lng_ref[0] + lnb_ref[0]

    _emit_pipelined((scores, softmax, mix, norm), list(range(tm // sub)))


def _out(x, mixg, mixs, c3, kvm, layer, wout, lng, lnb, tm=OUT_TILE, sub=ROW_TILE):
    b, t, _ = x.shape
    return pl.pallas_call(
        functools.partial(_out_kernel, tm=tm, sub=sub),
        grid=(b, t // tm),
        in_specs=[pl.BlockSpec((1, tm, D_MODEL), lambda i, j: (i, j, 0)),
                  pl.BlockSpec((1, tm, GLA_V_W), lambda i, j: (i, j, 0)),
                  pl.BlockSpec((1, tm, SB_W), lambda i, j: (i, j, 0)),
                  pl.BlockSpec((1, tm, C3_COLS), lambda i, j: (i, j, 0)),
                  pl.BlockSpec((1, 1, N_MEM, 2 * MEM_W), lambda i, j: (layer, i, 0, 0)),
                  pl.BlockSpec((1, D_MODEL, D_MODEL), lambda i, j: (layer, 0, 0)),
                  pl.BlockSpec((1, 1, D_MODEL), lambda i, j: (layer, 0, 0)),
                  pl.BlockSpec((1, 1, D_MODEL), lambda i, j: (layer, 0, 0))],
        out_specs=pl.BlockSpec((1, tm, D_MODEL), lambda i, j: (i, j, 0)),
        out_shape=jax.ShapeDtypeStruct((b, t, D_MODEL), F32),
        compiler_params=pltpu.CompilerParams(dimension_semantics=("arbitrary", "arbitrary"),
                                             vmem_limit_bytes=VMEM_LIMIT),
        name="out",
    )(x, mixg, mixs, c3, kvm, wout, lng, lnb)


_W_IN_SEGMENTS = (("gq", GLA_QK_W), ("gk", GLA_QK_W), ("gv", GLA_V_W), ("gg", GLA_V_W), ("ga", GLA_RANK),
                  ("sq", SB_W), ("sk", SB_W), ("sv", SB_W), ("sg", SB_W), ("mq", MEM_W), ("mg", MEM_W))
_W_PROJ_ORDER = ("gq", "gk", "gv", "gg", "sq", "sk", "sv", "sg", "mq", "mg", "ga")
_Q_SCALED = {"gq": GLA_DK ** -0.5, "sq": SB_DH ** -0.5, "mq": MEM_DH ** -0.5}
D_IN = sum(n for _, n in _W_IN_SEGMENTS)


def _wprep_kernel(wt_ref, o_ref):
    src, off = {}, 0
    for name, n in _W_IN_SEGMENTS:
        src[name] = (off, n)
        off += n
    dst = 0
    for name in _W_PROJ_ORDER:
        lo, n = src[name]
        rows = max(n, LANES)
        cols = wt_ref[0, lo:lo + rows, :].T
        if name in _Q_SCALED:
            cols = cols * _Q_SCALED[name]
        if rows > n:
            cols = jnp.where(lax.broadcasted_iota(jnp.int32, cols.shape, 1) < n, cols, 0.0)
        o_ref[0, :, dst:dst + rows] = cols.astype(BF16)
        dst += rows
    assert dst == PROJ_COLS


def _regroup_w_in(wt, tk=256):
    depth, _, d = wt.shape
    return pl.pallas_call(
        _wprep_kernel,
        grid=(depth, d // tk),
        in_specs=[pl.BlockSpec((1, D_IN, tk), lambda l, i: (l, 0, i))],
        out_specs=pl.BlockSpec((1, tk, PROJ_COLS), lambda l, i: (l, i, 0)),
        out_shape=jax.ShapeDtypeStruct((depth, d, PROJ_COLS), BF16),
        compiler_params=pltpu.CompilerParams(dimension_semantics=("arbitrary", "arbitrary"),
                                             vmem_limit_bytes=VMEM_LIMIT),
        name="wprep",
    )(wt)


def kernel(x, mem, w_in, w_alpha2, b_alpha, gla_norm_w, w_mem_kv, w_out, ln_g, ln_b):
    b, t, d = x.shape
    assert d == D_MODEL and mem.shape == (b, N_MEM, d) and w_in.shape == (DEPTH, d, D_IN)
    assert t % GLA_TILE == 0 and t % SB_QTILE == 0 and t % OUT_TILE == 0
    kvm = _memkv(mem.reshape(b * N_MEM, d), w_mem_kv.astype(BF16)).reshape(DEPTH, b, N_MEM, 2 * MEM_W)
    w_proj = _regroup_w_in(jnp.swapaxes(w_in, 1, 2))
    wa2 = jnp.pad(w_alpha2.astype(BF16), ((0, 0), (0, GA_COLS - GLA_RANK), (0, 0)))
    ba = b_alpha.reshape(DEPTH, 1, GLA_QK_W)
    nw = gla_norm_w.reshape(DEPTH, 1, GLA_DV)
    wout = w_out.astype(BF16)
    lng = ln_g.reshape(DEPTH, 1, d)
    lnb = ln_b.reshape(DEPTH, 1, d)
    for l in range(DEPTH):
        gla, sb, c3, ga = _proj(x.reshape(b * t, d), w_proj, l)
        mixg = _gla(gla.reshape(b, t, GLA_COLS), ga.reshape(b, t, GA_COLS), wa2, ba, nw, l)
        c3 = c3.reshape(b, t, C3_COLS)
        mixs = _sb(sb.reshape(b, t, SB_COLS), c3)
        x = _out(x, mixg, mixs, c3, kvm, l, wout, lng, lnb)
    return x
```

```python
import functools

import jax
import jax.numpy as jnp
from jax import lax
from jax.experimental import pallas as pl
from jax.experimental.pallas import tpu as pltpu

F32 = jnp.float32
BF16 = jnp.bfloat16

D_MODEL = 1024
DEPTH = 4
N_MEM = 256
GLA_HEADS = 4
GLA_DK = 64
GLA_DV = 128
GLA_RANK = 16
GLA_GATE_NORM = 16.0
GLA_CHUNK = 64
SB_HEADS = 4
SB_DH = 64
SB_BLOCK = 128
MEM_HEADS = 4
MEM_DH = 64
GLA_QK_W = GLA_HEADS * GLA_DK
GLA_V_W = GLA_HEADS * GLA_DV
SB_W = SB_HEADS * SB_DH
MEM_W = MEM_HEADS * MEM_DH
ALPHA = (2.0 * DEPTH) ** 0.25
LN_EPS = 1e-5
RMS_EPS = 1e-6

LANES = 128
GLA_COLS = 2 * GLA_QK_W + 2 * GLA_V_W
SB_COLS = 3 * SB_W
C3_COLS = SB_W + 2 * MEM_W
GA_COLS = LANES
PROJ_COLS = GLA_COLS + SB_COLS + C3_COLS + GA_COLS
VMEM_LIMIT = 56 * 1024 * 1024
OUT_VMEM_LIMIT = 60 * 1024 * 1024
SB_SKIP_LOG = -106.0
SB_HALF = 64
SB_LEAD_KEYS = 2 * SB_BLOCK
SB_QTILE = 2048
GLA_TILE = 2048
ROW_TILE = 512
OUT_TILE = 2048
PROJ_TILE = 1024
LOG2E = 1.4426950408889634


def _dot(a, b):
    return jnp.dot(a, b, preferred_element_type=F32)


def _dot_nt(a, b):
    return lax.dot_general(a, b, (((1,), (1,)), ((), ())), preferred_element_type=F32)


def _dot_tn(a, b):
    return lax.dot_general(a, b, (((0,), (0,)), ((), ())), preferred_element_type=F32)


def _split_bf16(x, n):
    parts = []
    for _ in range(n - 1):
        h = x.astype(BF16)
        parts.append(h)
        x = x - h.astype(F32)
    parts.append(x.astype(BF16))
    return parts


def _exp_neg(x):
    return jnp.exp2(x * -LOG2E)


def _log_sigmoid(z):
    return jnp.minimum(z, 0.0) - jnp.log(1.0 + _exp_neg(jnp.abs(z)))


def _silu(g):
    return g / (1.0 + _exp_neg(g))


def _emit_pipelined(stages, items):
    for step in range(len(items) + len(stages) - 1):
        for depth, stage in enumerate(stages):
            j = step - depth
            if 0 <= j < len(items):
                stage(items[j])


def _proj_kernel(x_ref, w_ref, gla_ref, sb_ref, c3_ref, ga_ref):
    xb = x_ref[...].astype(BF16)
    w_ref = w_ref.at[0]
    off = 0
    for ref, width in ((gla_ref, GLA_COLS), (sb_ref, SB_COLS), (c3_ref, C3_COLS), (ga_ref, GA_COLS)):
        step = 512 if width % 512 == 0 else width if width < 512 else 256
        for c in range(0, width, step):
            ref[:, c:c + step] = _dot(xb, w_ref[:, off + c:off + c + step]).astype(BF16)
        off += width


def _proj(x2, w, layer, tm=PROJ_TILE):
    n = x2.shape[0]
    outs = (GLA_COLS, SB_COLS, C3_COLS, GA_COLS)
    return pl.pallas_call(
        _proj_kernel,
        grid=(n // tm,),
        in_specs=[pl.BlockSpec((tm, D_MODEL), lambda i: (i, 0)),
                  pl.BlockSpec((1, D_MODEL, PROJ_COLS), lambda i: (layer, 0, 0))],
        out_specs=[pl.BlockSpec((tm, c), lambda i: (i, 0)) for c in outs],
        out_shape=[jax.ShapeDtypeStruct((n, c), BF16) for c in outs],
        compiler_params=pltpu.CompilerParams(dimension_semantics=("arbitrary",),
                                             vmem_limit_bytes=VMEM_LIMIT),
        name="proj",
    )(x2, w)


def _memkv_kernel(m_ref, w_ref, o_ref):
    o_ref[0] = _dot(m_ref[...].astype(BF16), w_ref[0]).astype(BF16)


def _memkv(mem2, w_mkv):
    n = mem2.shape[0]
    return pl.pallas_call(
        _memkv_kernel,
        grid=(DEPTH,),
        in_specs=[pl.BlockSpec((n, D_MODEL), lambda l: (0, 0)),
                  pl.BlockSpec((1, D_MODEL, 2 * MEM_W), lambda l: (l, 0, 0))],
        out_specs=pl.BlockSpec((1, n, 2 * MEM_W), lambda l: (l, 0, 0)),
        out_shape=jax.ShapeDtypeStruct((DEPTH, n, 2 * MEM_W), BF16),
        compiler_params=pltpu.CompilerParams(dimension_semantics=("arbitrary",),
                                             vmem_limit_bytes=VMEM_LIMIT),
        name="memkv",
    )(mem2, w_mkv)


def _gla_kernel(gla_ref, ga_ref, wa2_ref, ba_ref, nw_ref, out_ref, st_ref, *, tt):
    c_sz = GLA_CHUNK

    @pl.when(pl.program_id(1) == 0)
    def _():
        st_ref[...] = jnp.zeros_like(st_ref)

    row = lax.broadcasted_iota(jnp.int32, (c_sz, c_sz), 0)
    col = lax.broadcasted_iota(jnp.int32, (c_sz, c_sz), 1)
    causal = col <= row
    r2 = lax.broadcasted_iota(jnp.int32, (c_sz, 2 * c_sz), 0)
    c2 = lax.broadcasted_iota(jnp.int32, (c_sz, 2 * c_sz), 1) & (c_sz - 1)
    tri_incl2 = jnp.where(c2 <= r2, 1.0, 0.0).astype(BF16)
    low_c = lax.broadcasted_iota(jnp.int32, (c_sz, LANES), 1) < GLA_DK
    chunks = range(tt // c_sz)
    pairs = range(GLA_HEADS // 2)
    rows = [slice(c * c_sz, (c + 1) * c_sz) for c in chunks]
    lanes = [slice(p * LANES, (p + 1) * LANES) for p in pairs]
    v_col = 2 * GLA_QK_W
    g_col = v_col + GLA_V_W

    zz = _dot(ga_ref[0], wa2_ref[0]) + ba_ref[0]
    log_a = _log_sigmoid(zz) * (1.0 / GLA_GATE_NORM)
    hi, lo = _split_bf16(log_a, 2)
    g_cum, kd, decay, q2, a2, upd, inter2 = {}, {}, {}, {}, {}, {}, {}
    state = {p: st_ref[p] for p in pairs}
    norm_w = nw_ref[0] * GLA_DV ** 0.5

    def v_head(c, h):
        return gla_ref[0, rows[c], v_col + h * GLA_DV:v_col + (h + 1) * GLA_DV]

    def cumsum(c):
        g_cum[c] = _dot(tri_incl2, jnp.concatenate([hi[rows[c]], lo[rows[c]]], axis=0))

    def scale(c):
        g_last = g_cum[c][c_sz - 1:c_sz, :]
        q = gla_ref[0, rows[c], 0:GLA_QK_W].astype(F32)
        k = gla_ref[0, rows[c], GLA_QK_W:2 * GLA_QK_W].astype(F32)
        decay[c] = jnp.exp(g_last)
        qg = (q * jnp.exp(g_cum[c])).astype(BF16)
        kgf = k * _exp_neg(g_cum[c])
        kg = kgf.astype(BF16)
        kd[c] = (kgf * decay[c]).astype(BF16)
        for p in pairs:
            qgp = qg[:, lanes[p]]
            q2[c, p] = jnp.concatenate([jnp.where(low_c, qgp, jnp.zeros_like(qgp)),
                                        jnp.where(low_c, jnp.zeros_like(qgp), qgp)], axis=0)
            a2[c, p] = _dot_nt(q2[c, p], kg[:, lanes[p]])

    def update(c):
        for p in pairs:
            kdp = kd[c][:, lanes[p]]
            zero = jnp.zeros_like(kdp)
            upd[c, p] = _dot_tn(jnp.concatenate([v_head(c, 2 * p), v_head(c, 2 * p + 1)], axis=0),
                                jnp.concatenate([jnp.where(low_c, kdp, zero), jnp.where(low_c, zero, kdp)], axis=0))

    def recur(c):
        for p in pairs:
            inter2[c, p] = _dot_nt(q2[c, p], state[p].astype(BF16))
            state[p] = state[p] * decay[c][:, lanes[p]] + upd[c, p]

    def finish(c):
        for p in pairs:
            for hh in range(2):
                h = 2 * p + hh
                a = jnp.where(causal, a2[c, p][hh * c_sz:(hh + 1) * c_sz], 0.0).astype(BF16)
                o = _dot(a, v_head(c, h)) + inter2[c, p][hh * c_sz:(hh + 1) * c_sz]
                ssq = jnp.sum(o * o, axis=-1, keepdims=True)
                gate = gla_ref[0, rows[c], g_col + h * GLA_DV:g_col + (h + 1) * GLA_DV].astype(F32)
                res = o * lax.rsqrt(ssq + GLA_DV * RMS_EPS) * norm_w * _silu(gate)
                out_ref[0, rows[c], h * GLA_DV:(h + 1) * GLA_DV] = res.astype(BF16)

    _emit_pipelined((cumsum, scale, update, recur, finish), list(chunks))
    for p in pairs:
        st_ref[p] = state[p]


def _gla(gla, ga, wa2, ba, nw, layer, tt=GLA_TILE):
    b, t, _ = gla.shape
    return pl.pallas_call(
        functools.partial(_gla_kernel, tt=tt),
        grid=(b, t // tt),
        in_specs=[pl.BlockSpec((1, tt, GLA_COLS), lambda i, j: (i, j, 0)),
                  pl.BlockSpec((1, tt, GA_COLS), lambda i, j: (i, j, 0)),
                  pl.BlockSpec((1, GA_COLS, GLA_QK_W), lambda i, j: (layer, 0, 0)),
                  pl.BlockSpec((1, 1, GLA_QK_W), lambda i, j: (layer, 0, 0)),
                  pl.BlockSpec((1, 1, GLA_DV), lambda i, j: (layer, 0, 0))],
        out_specs=pl.BlockSpec((1, tt, GLA_V_W), lambda i, j: (i, j, 0)),
        out_shape=jax.ShapeDtypeStruct((b, t, GLA_V_W), BF16),
        scratch_shapes=[pltpu.VMEM((GLA_HEADS // 2, GLA_DV, LANES), F32)],
        compiler_params=pltpu.CompilerParams(dimension_semantics=("arbitrary", "arbitrary"),
                                             vmem_limit_bytes=VMEM_LIMIT),
        name="gla",
    )(gla, ga, wa2, ba, nw)


def _sb_kernel(q_ref, k_ref, v_ref, g_ref, o_ref, oacc_ref, lacc_ref):
    hb, kb = SB_HALF, SB_BLOCK
    pairs = SB_HEADS // 2
    halves = SB_QTILE // hb
    q_base = pl.program_id(1) * SB_QTILE
    row = lax.broadcasted_iota(jnp.int32, (hb, kb), 0)
    lane = lax.broadcasted_iota(jnp.int32, (hb, kb), 1)
    low = lane < SB_DH
    rk = lax.broadcasted_iota(jnp.int32, (kb, 2 * kb), 0)
    ck = lax.broadcasted_iota(jnp.int32, (kb, 2 * kb), 1)
    tri_ones = jnp.where((rk > ck) | (ck >= kb), 1.0, 0.0).astype(BF16)

    def pair_rows(m, p):
        st = m * pairs + p
        return slice(st * 2 * hb, (st + 1) * 2 * hb)

    qneg = {}
    for m in range(halves):
        for p in range(pairs):
            qn = -q_ref[0, m * hb:(m + 1) * hb, p * LANES:(p + 1) * LANES]
            zero = jnp.zeros_like(qn)
            qneg[m, p] = jnp.concatenate([jnp.where(low, qn, zero), jnp.where(low, zero, qn)], axis=0)
    def lead_group(start, valid):
        rows = {m: pl.ds(start[m] if isinstance(start[m], int) else pl.multiple_of(start[m], hb), 2 * kb)
                for m in start}
        r4 = lax.broadcasted_iota(jnp.int32, (2 * kb, 2 * kb), 0)
        c4 = lax.broadcasted_iota(jnp.int32, (2 * kb, 2 * kb), 1)
        later = ((r4 > c4) & ((r4 >= kb) == (c4 >= kb))) | ((r4 < kb) & (c4 >= kb))
        tri_pair = jnp.where(later, 1.0, 0.0).astype(BF16)
        keep1 = jnp.where(valid, 1.0, 0.0).astype(BF16)
        keep = jnp.concatenate([keep1, keep1], axis=0)
        log_beta, lf, sums = {}, {}, {}

        def scores(m):
            for p in range(pairs):
                s = _dot_nt(qneg[m, p], k_ref[0, rows[m], p * LANES:(p + 1) * LANES])
                for g in (1, 0):
                    sg = s[:, g * kb:(g + 1) * kb].astype(BF16)
                    lsn = _log_sigmoid(sg)
                    log_beta[m, p, g] = lsn - sg
                    lf[m, p, g] = lsn * keep if g == 1 else lsn

        def suffix_sums(m):
            lhs = [jnp.concatenate([lf[m, p, 1], lf[m, p, 0]], axis=1) for p in range(pairs)]
            sums[m] = _dot(jnp.concatenate(lhs, axis=0), tri_pair)

        def weights(m):
            for p in range(pairs):
                part = sums[m][p * 2 * hb:(p + 1) * 2 * hb]
                w1 = jnp.exp(part[:, :kb].astype(BF16) + log_beta[m, p, 1]) * keep
                w0 = jnp.exp(part[:, kb:].astype(BF16) + log_beta[m, p, 0])
                total = part[:, kb:kb + 1] + lf[m, p, 0][:, 0:1].astype(F32)
                lacc_ref[pair_rows(m, p)] = jnp.broadcast_to(total, (2 * hb, kb))
                oacc_ref[pair_rows(m, p)] = _dot(jnp.concatenate([w0, w1], axis=1),
                                                 v_ref[0, rows[m], p * LANES:(p + 1) * LANES])

        _emit_pipelined((scores, suffix_sums, weights), list(start))

    def group(start, n, valid):
        order = list(reversed(range(n)))
        rows = {m: pl.ds(start[m] if isinstance(start[m], int) else pl.multiple_of(start[m], hb), n * kb)
                for m in start}
        log_beta, lf, sums = {}, {}, {}
        keep1 = jnp.where(valid, 1.0, 0.0).astype(BF16)
        keep = jnp.concatenate([keep1, keep1], axis=0)
        parts = [(p, g) for g in order for p in range(pairs)]

        def scores(m):
            s = {p: _dot_nt(qneg[m, p], k_ref[0, rows[m], p * LANES:(p + 1) * LANES])
                 for p in range(pairs)}
            for p, g in parts:
                sg = s[p][:, g * kb:(g + 1) * kb].astype(BF16)
                lsn = _log_sigmoid(sg)
                log_beta[m, p, g] = lsn - sg
                lf[m, p, g] = lsn * keep if g == n - 1 else lsn

        def suffix_sums(m):
            sums[m] = _dot(jnp.concatenate([lf[m, p, g] for p, g in parts], axis=0), tri_ones)

        def weights(m):
            acc = {p: lacc_ref[pair_rows(m, p)] for p in range(pairs)}
            ws = {p: [None] * n for p in range(pairs)}
            for idx, (p, g) in enumerate(parts):
                part = sums[m][idx * 2 * hb:(idx + 1) * 2 * hb]
                w = jnp.exp((part[:, :kb] + acc[p]).astype(BF16) + log_beta[m, p, g])
                ws[p][g] = w * keep if g == n - 1 else w
                acc[p] = acc[p] + part[:, kb:]
            for p in range(pairs):
                lacc_ref[pair_rows(m, p)] = acc[p]
                oacc_ref[pair_rows(m, p)] += _dot(jnp.concatenate(ws[p], axis=1),
                                                  v_ref[0, rows[m], p * LANES:(p + 1) * LANES])

        _emit_pipelined((scores, suffix_sums, weights), list(start))

    def sweep(m, bound):
        per_m = 2 * pairs * hb

        def live():
            return jnp.max(lacc_ref[m * per_m:(m + 1) * per_m])

        def cond(c):
            return jnp.logical_and(c[0] > 0, c[1] > SB_SKIP_LOG)

        def body(c):
            first = jnp.maximum(c[0] - kb, 0)
            group({m: first}, 1, lane < c[0] - first)
            return first, live()

        lax.while_loop(cond, body, (bound, live()))

    def lead_and_sweep(base, ms):
        lead = {m: base + (m + 1) * hb - SB_LEAD_KEYS for m in ms}
        lead_group(lead, lane < row + (kb - hb))
        rows_all = slice(ms[0] * 2 * pairs * hb, (ms[-1] + 1) * 2 * pairs * hb)

        @pl.when(jnp.max(lacc_ref[rows_all]) > SB_SKIP_LOG)
        def _():
            for m in ms:
                sweep(m, lead[m])

    short = (SB_LEAD_KEYS - hb) // hb

    @pl.when(q_base > 0)
    def _():
        lead_and_sweep(q_base, list(range(halves)))

    @pl.when(q_base == 0)
    def _():
        for m in range(short):
            oacc_ref[m * 2 * pairs * hb:(m + 1) * 2 * pairs * hb] = jnp.zeros((2 * pairs * hb, LANES), F32)
            lacc_ref[m * 2 * pairs * hb:(m + 1) * 2 * pairs * hb] = jnp.zeros((2 * pairs * hb, LANES), F32)
            first = max(m * hb + hb - kb, 0)
            group({m: first}, 1, lane < row + (m * hb - first))
            sweep(m, first)
        lead_and_sweep(0, list(range(short, halves)))

    for m in range(halves):
        for p in range(pairs):
            both = oacc_ref[pair_rows(m, p)]
            o = jnp.where(low, both[:hb], both[hb:])
            gate = g_ref[0, m * hb:(m + 1) * hb, p * LANES:(p + 1) * LANES].astype(F32)
            o_ref[0, m * hb:(m + 1) * hb, p * LANES:(p + 1) * LANES] = (o * _silu(gate)).astype(BF16)


def _sb(sb, c3):
    b, t, _ = sb.shape
    n_streams = (SB_QTILE // SB_HALF) * SB_HEADS
    return pl.pallas_call(
        _sb_kernel,
        grid=(b, t // SB_QTILE),
        in_specs=[pl.BlockSpec((1, SB_QTILE, SB_W), lambda bi, i: (bi, i, 0)),
                  pl.BlockSpec((1, t, SB_W), lambda bi, i: (bi, 0, 1)),
                  pl.BlockSpec((1, t, SB_W), lambda bi, i: (bi, 0, 2)),
                  pl.BlockSpec((1, SB_QTILE, SB_W), lambda bi, i: (bi, i, 0))],
        out_specs=pl.BlockSpec((1, SB_QTILE, SB_W), lambda bi, i: (bi, i, 0)),
        out_shape=jax.ShapeDtypeStruct((b, t, SB_W), BF16),
        scratch_shapes=[pltpu.VMEM((n_streams * SB_HALF, LANES), F32),
                        pltpu.VMEM((n_streams * SB_HALF, LANES), F32)],
        compiler_params=pltpu.CompilerParams(dimension_semantics=("arbitrary", "arbitrary"),
                                             vmem_limit_bytes=VMEM_LIMIT),
        name="sb",
    )(sb, sb, sb, c3)


def _out_kernel(x_ref, mixg_ref, mixs_ref, mq_ref, mg_ref, kvm_ref, wout_ref, lng_ref, lnb_ref, o_ref, *, tm, sub):
    wout_ref = wout_ref.at[0]
    low = lax.broadcasted_iota(jnp.int32, (sub, LANES), 1) < MEM_DH
    pairs = range(MEM_HEADS // 2)
    s, e, den, y = {}, {}, {}, {}

    def rows(j):
        return slice(j * sub, (j + 1) * sub)

    def scores(j):
        for p in pairs:
            qp = mq_ref[0, rows(j), p * LANES:(p + 1) * LANES]
            km = kvm_ref[0, 0, :, p * LANES:(p + 1) * LANES]
            zero = jnp.zeros_like(qp)
            s[j, p, 0] = _dot_nt(jnp.where(low, qp, zero), km)
            s[j, p, 1] = _dot_nt(jnp.where(low, zero, qp), km)

    def softmax(j):
        for p in pairs:
            for hh in range(2):
                sc = s[j, p, hh]
                ex = jnp.exp(sc - jnp.max(sc, axis=-1, keepdims=True))
                den[j, p, hh] = jnp.sum(ex, axis=-1, keepdims=True)
                e[j, p, hh] = ex.astype(BF16)

    def mix(j):
        mixm = []
        for p in pairs:
            vm = kvm_ref[0, 0, :, MEM_W + p * LANES:MEM_W + (p + 1) * LANES]
            om = jnp.where(low, _dot(e[j, p, 0], vm) / den[j, p, 0], _dot(e[j, p, 1], vm) / den[j, p, 1])
            gate = mg_ref[0, rows(j), p * LANES:(p + 1) * LANES].astype(F32)
            mixm.append((om * _silu(gate)).astype(BF16))
        y[j] = _dot(jnp.concatenate([mixg_ref[0, rows(j), :], mixs_ref[0, rows(j), :]] + mixm, axis=1), wout_ref[...])

    def norm(j):
        r = ALPHA * x_ref[0, rows(j), :] + y[j]
        mu = jnp.mean(r, axis=-1, keepdims=True)
        d = r - mu
        var = jnp.mean(d * d, axis=-1, keepdims=True)
        o_ref[0, rows(j), :] = d * lax.rsqrt(var + LN_EPS) * lng_ref[0] + lnb_ref[0]

    _emit_pipelined((scores, softmax, mix, norm), list(range(tm // sub)))


def _out(x, mixg, mixs, c3, kvm, layer, wout, lng, lnb, tm=OUT_TILE, sub=ROW_TILE):
    b, t, _ = x.shape
    return pl.pallas_call(
        functools.partial(_out_kernel, tm=tm, sub=sub),
        grid=(b, t // tm),
        in_specs=[pl.BlockSpec((1, tm, D_MODEL), lambda i, j: (i, j, 0)),
                  pl.BlockSpec((1, tm, GLA_V_W), lambda i, j: (i, j, 0)),
                  pl.BlockSpec((1, tm, SB_W), lambda i, j: (i, j, 0)),
                  pl.BlockSpec((1, tm, MEM_W), lambda i, j: (i, j, SB_W // MEM_W)),
                  pl.BlockSpec((1, tm, MEM_W), lambda i, j: (i, j, SB_W // MEM_W + 1)),
                  pl.BlockSpec((1, 1, N_MEM, 2 * MEM_W), lambda i, j: (layer, i, 0, 0)),
                  pl.BlockSpec((1, D_MODEL, D_MODEL), lambda i, j: (layer, 0, 0), pipeline_mode=pl.Buffered(1)),
                  pl.BlockSpec((1, 1, D_MODEL), lambda i, j: (layer, 0, 0)),
                  pl.BlockSpec((1, 1, D_MODEL), lambda i, j: (layer, 0, 0))],
        out_specs=pl.BlockSpec((1, tm, D_MODEL), lambda i, j: (i, j, 0)),
        out_shape=jax.ShapeDtypeStruct((b, t, D_MODEL), F32),
        compiler_params=pltpu.CompilerParams(dimension_semantics=("arbitrary", "arbitrary"),
                                             vmem_limit_bytes=OUT_VMEM_LIMIT),
        name="out",
    )(x, mixg, mixs, c3, c3, kvm, wout, lng, lnb)


_W_IN_SEGMENTS = (("gq", GLA_QK_W), ("gk", GLA_QK_W), ("gv", GLA_V_W), ("gg", GLA_V_W), ("ga", GLA_RANK),
                  ("sq", SB_W), ("sk", SB_W), ("sv", SB_W), ("sg", SB_W), ("mq", MEM_W), ("mg", MEM_W))
_W_PROJ_ORDER = ("gq", "gk", "gv", "gg", "sq", "sk", "sv", "sg", "mq", "mg", "ga")
_Q_SCALED = {"gq": GLA_DK ** -0.5, "sq": SB_DH ** -0.5, "mq": MEM_DH ** -0.5}
D_IN = sum(n for _, n in _W_IN_SEGMENTS)


def _wprep_kernel(wt_ref, o_ref):
    src, off = {}, 0
    for name, n in _W_IN_SEGMENTS:
        src[name] = (off, n)
        off += n
    dst = 0
    for name in _W_PROJ_ORDER:
        lo, n = src[name]
        rows = max(n, LANES)
        cols = wt_ref[0, lo:lo + rows, :].T
        if name in _Q_SCALED:
            cols = cols * _Q_SCALED[name]
        if rows > n:
            cols = jnp.where(lax.broadcasted_iota(jnp.int32, cols.shape, 1) < n, cols, 0.0)
        o_ref[0, :, dst:dst + rows] = cols.astype(BF16)
        dst += rows
    assert dst == PROJ_COLS


def _regroup_w_in(wt, tk=256):
    depth, _, d = wt.shape
    return pl.pallas_call(
        _wprep_kernel,
        grid=(depth, d // tk),
        in_specs=[pl.BlockSpec((1, D_IN, tk), lambda l, i: (l, 0, i))],
        out_specs=pl.BlockSpec((1, tk, PROJ_COLS), lambda l, i: (l, i, 0)),
        out_shape=jax.ShapeDtypeStruct((depth, d, PROJ_COLS), BF16),
        compiler_params=pltpu.CompilerParams(dimension_semantics=("arbitrary", "arbitrary"),
                                             vmem_limit_bytes=VMEM_LIMIT),
        name="wprep",
    )(wt)


def kernel(x, mem, w_in, w_alpha2, b_alpha, gla_norm_w, w_mem_kv, w_out, ln_g, ln_b):
    b, t, d = x.shape
    assert d == D_MODEL and mem.shape == (b, N_MEM, d) and w_in.shape == (DEPTH, d, D_IN)
    assert t % GLA_TILE == 0 and t % SB_QTILE == 0 and t % OUT_TILE == 0 and (b * t) % PROJ_TILE == 0
    kvm = _memkv(mem.reshape(b * N_MEM, d), w_mem_kv.astype(BF16)).reshape(DEPTH, b, N_MEM, 2 * MEM_W)
    w_proj = _regroup_w_in(jnp.swapaxes(w_in, 1, 2))
    wa2 = jnp.pad(w_alpha2.astype(BF16), ((0, 0), (0, GA_COLS - GLA_RANK), (0, 0)))
    ba = b_alpha.reshape(DEPTH, 1, GLA_QK_W)
    nw = gla_norm_w.reshape(DEPTH, 1, GLA_DV)
    wout = w_out.astype(BF16)
    lng = ln_g.reshape(DEPTH, 1, d)
    lnb = ln_b.reshape(DEPTH, 1, d)
    for l in range(DEPTH):
        gla, sb, c3, ga = _proj(x.reshape(b * t, d), w_proj, l)
        mixg = _gla(gla.reshape(b, t, GLA_COLS), ga.reshape(b, t, GA_COLS), wa2, ba, nw, l)
        c3 = c3.reshape(b, t, C3_COLS)
        mixs = _sb(sb.reshape(b, t, SB_COLS), c3)
        x = _out(x, mixg, mixs, c3, kvm, l, wout, lng, lnb)
    return x
```

```python
import functools

import jax
import jax.numpy as jnp
from jax import lax
from jax.experimental import pallas as pl
from jax.experimental.pallas import tpu as pltpu

F32 = jnp.float32
BF16 = jnp.bfloat16

D_MODEL = 1024
DEPTH = 4
N_MEM = 256
GLA_HEADS = 4
GLA_DK = 64
GLA_DV = 128
GLA_RANK = 16
GLA_GATE_NORM = 16.0
GLA_CHUNK = 64
SB_HEADS = 4
SB_DH = 64
SB_BLOCK = 128
MEM_HEADS = 4
MEM_DH = 64
GLA_QK_W = GLA_HEADS * GLA_DK
GLA_V_W = GLA_HEADS * GLA_DV
SB_W = SB_HEADS * SB_DH
MEM_W = MEM_HEADS * MEM_DH
ALPHA = (2.0 * DEPTH) ** 0.25
LN_EPS = 1e-5
RMS_EPS = 1e-6

LANES = 128
GLA_COLS = 2 * GLA_QK_W + 2 * GLA_V_W
SB_COLS = 3 * SB_W
C3_COLS = SB_W + 2 * MEM_W
GA_COLS = LANES
PROJ_COLS = GLA_COLS + SB_COLS + C3_COLS + GA_COLS
VMEM_LIMIT = 56 * 1024 * 1024
SB_SKIP_LOG = -106.0
SB_HALF = 64
SB_LEAD_KEYS = 2 * SB_BLOCK
SB_QTILE = 2048
GLA_TILE = 2048
ROW_TILE = 512
OUT_TILE = 1024
PROJ_TILE = 1024
LOG2E = 1.4426950408889634


def _dot(a, b):
    return jnp.dot(a, b, preferred_element_type=F32)


def _dot_nt(a, b):
    return lax.dot_general(a, b, (((1,), (1,)), ((), ())), preferred_element_type=F32)


def _dot_tn(a, b):
    return lax.dot_general(a, b, (((0,), (0,)), ((), ())), preferred_element_type=F32)


def _split_bf16(x, n):
    parts = []
    for _ in range(n - 1):
        h = x.astype(BF16)
        parts.append(h)
        x = x - h.astype(F32)
    parts.append(x.astype(BF16))
    return parts


def _exp_neg(x):
    return jnp.exp2(x * -LOG2E)


def _log_sigmoid(z):
    return jnp.minimum(z, 0.0) - jnp.log(1.0 + _exp_neg(jnp.abs(z)))


def _silu(g):
    return g / (1.0 + _exp_neg(g))


def _emit_pipelined(stages, items):
    for step in range(len(items) + len(stages) - 1):
        for depth, stage in enumerate(stages):
            j = step - depth
            if 0 <= j < len(items):
                stage(items[j])


def _proj_kernel(x_ref, w_ref, gla_ref, sb_ref, c3_ref, ga_ref):
    xb = x_ref[...].astype(BF16)
    w_ref = w_ref.at[0]
    off = 0
    for ref, width in ((gla_ref, GLA_COLS), (sb_ref, SB_COLS), (c3_ref, C3_COLS), (ga_ref, GA_COLS)):
        step = 512 if width % 512 == 0 else width if width < 512 else 256
        for c in range(0, width, step):
            ref[:, c:c + step] = _dot(xb, w_ref[:, off + c:off + c + step]).astype(BF16)
        off += width


def _proj(x2, w, layer, tm=PROJ_TILE):
    n = x2.shape[0]
    outs = (GLA_COLS, SB_COLS, C3_COLS, GA_COLS)
    return pl.pallas_call(
        _proj_kernel,
        grid=(n // tm,),
        in_specs=[pl.BlockSpec((tm, D_MODEL), lambda i: (i, 0)),
                  pl.BlockSpec((1, D_MODEL, PROJ_COLS), lambda i: (layer, 0, 0))],
        out_specs=[pl.BlockSpec((tm, c), lambda i: (i, 0)) for c in outs],
        out_shape=[jax.ShapeDtypeStruct((n, c), BF16) for c in outs],
        compiler_params=pltpu.CompilerParams(dimension_semantics=("arbitrary",),
                                             vmem_limit_bytes=VMEM_LIMIT),
        name="proj",
    )(x2, w)


def _memkv_kernel(m_ref, w_ref, o_ref):
    o_ref[0] = _dot(m_ref[...].astype(BF16), w_ref[0]).astype(BF16)


def _memkv(mem2, w_mkv):
    n = mem2.shape[0]
    return pl.pallas_call(
        _memkv_kernel,
        grid=(DEPTH,),
        in_specs=[pl.BlockSpec((n, D_MODEL), lambda l: (0, 0)),
                  pl.BlockSpec((1, D_MODEL, 2 * MEM_W), lambda l: (l, 0, 0))],
        out_specs=pl.BlockSpec((1, n, 2 * MEM_W), lambda l: (l, 0, 0)),
        out_shape=jax.ShapeDtypeStruct((DEPTH, n, 2 * MEM_W), BF16),
        compiler_params=pltpu.CompilerParams(dimension_semantics=("arbitrary",),
                                             vmem_limit_bytes=VMEM_LIMIT),
        name="memkv",
    )(mem2, w_mkv)


def _gla_kernel(gla_ref, ga_ref, wa2_ref, ba_ref, nw_ref, out_ref, st_ref, *, tt):
    c_sz = GLA_CHUNK

    @pl.when(pl.program_id(1) == 0)
    def _():
        st_ref[...] = jnp.zeros_like(st_ref)

    row = lax.broadcasted_iota(jnp.int32, (c_sz, c_sz), 0)
    col = lax.broadcasted_iota(jnp.int32, (c_sz, c_sz), 1)
    causal = col <= row
    r2 = lax.broadcasted_iota(jnp.int32, (c_sz, 2 * c_sz), 0)
    c2 = lax.broadcasted_iota(jnp.int32, (c_sz, 2 * c_sz), 1) & (c_sz - 1)
    tri_incl2 = jnp.where(c2 <= r2, 1.0, 0.0).astype(BF16)
    low_c = lax.broadcasted_iota(jnp.int32, (c_sz, LANES), 1) < GLA_DK
    chunks = range(tt // c_sz)
    pairs = range(GLA_HEADS // 2)
    rows = [slice(c * c_sz, (c + 1) * c_sz) for c in chunks]
    lanes = [slice(p * LANES, (p + 1) * LANES) for p in pairs]
    v_col = 2 * GLA_QK_W
    g_col = v_col + GLA_V_W

    zz = _dot(ga_ref[0], wa2_ref[0]) + ba_ref[0]
    log_a = _log_sigmoid(zz) * (1.0 / GLA_GATE_NORM)
    hi, lo = _split_bf16(log_a, 2)
    g_cum, kd, decay, q2, a2, upd, inter2 = {}, {}, {}, {}, {}, {}, {}
    state = {p: st_ref[p] for p in pairs}
    norm_w = nw_ref[0] * GLA_DV ** 0.5

    def v_head(c, h):
        return gla_ref[0, rows[c], v_col + h * GLA_DV:v_col + (h + 1) * GLA_DV]

    def cumsum(c):
        g_cum[c] = _dot(tri_incl2, jnp.concatenate([hi[rows[c]], lo[rows[c]]], axis=0))

    def scale(c):
        g_last = g_cum[c][c_sz - 1:c_sz, :]
        q = gla_ref[0, rows[c], 0:GLA_QK_W].astype(F32)
        k = gla_ref[0, rows[c], GLA_QK_W:2 * GLA_QK_W].astype(F32)
        decay[c] = jnp.exp(g_last)
        qg = (q * jnp.exp(g_cum[c])).astype(BF16)
        kgf = k * _exp_neg(g_cum[c])
        kg = kgf.astype(BF16)
        kd[c] = (kgf * decay[c]).astype(BF16)
        for p in pairs:
            qgp = qg[:, lanes[p]]
            q2[c, p] = jnp.concatenate([jnp.where(low_c, qgp, jnp.zeros_like(qgp)),
                                        jnp.where(low_c, jnp.zeros_like(qgp), qgp)], axis=0)
            a2[c, p] = _dot_nt(q2[c, p], kg[:, lanes[p]])

    def update(c):
        for p in pairs:
            kdp = kd[c][:, lanes[p]]
            zero = jnp.zeros_like(kdp)
            upd[c, p] = _dot_tn(jnp.concatenate([v_head(c, 2 * p), v_head(c, 2 * p + 1)], axis=0),
                                jnp.concatenate([jnp.where(low_c, kdp, zero), jnp.where(low_c, zero, kdp)], axis=0))

    def recur(c):
        for p in pairs:
            inter2[c, p] = _dot_nt(q2[c, p], state[p].astype(BF16))
            state[p] = state[p] * decay[c][:, lanes[p]] + upd[c, p]

    def finish(c):
        for p in pairs:
            for hh in range(2):
                h = 2 * p + hh
                a = jnp.where(causal, a2[c, p][hh * c_sz:(hh + 1) * c_sz], 0.0).astype(BF16)
                o = _dot(a, v_head(c, h)) + inter2[c, p][hh * c_sz:(hh + 1) * c_sz]
                ssq = jnp.sum(o * o, axis=-1, keepdims=True)
                gate = gla_ref[0, rows[c], g_col + h * GLA_DV:g_col + (h + 1) * GLA_DV].astype(F32)
                res = o * lax.rsqrt(ssq + GLA_DV * RMS_EPS) * norm_w * _silu(gate)
                out_ref[0, rows[c], h * GLA_DV:(h + 1) * GLA_DV] = res.astype(BF16)

    _emit_pipelined((cumsum, scale, update, recur, finish), list(chunks))
    for p in pairs:
        st_ref[p] = state[p]


def _gla(gla, ga, wa2, ba, nw, layer, tt=GLA_TILE):
    b, t, _ = gla.shape
    return pl.pallas_call(
        functools.partial(_gla_kernel, tt=tt),
        grid=(b, t // tt),
        in_specs=[pl.BlockSpec((1, tt, GLA_COLS), lambda i, j: (i, j, 0)),
                  pl.BlockSpec((1, tt, GA_COLS), lambda i, j: (i, j, 0)),
                  pl.BlockSpec((1, GA_COLS, GLA_QK_W), lambda i, j: (layer, 0, 0)),
                  pl.BlockSpec((1, 1, GLA_QK_W), lambda i, j: (layer, 0, 0)),
                  pl.BlockSpec((1, 1, GLA_DV), lambda i, j: (layer, 0, 0))],
        out_specs=pl.BlockSpec((1, tt, GLA_V_W), lambda i, j: (i, j, 0)),
        out_shape=jax.ShapeDtypeStruct((b, t, GLA_V_W), BF16),
        scratch_shapes=[pltpu.VMEM((GLA_HEADS // 2, GLA_DV, LANES), F32)],
        compiler_params=pltpu.CompilerParams(dimension_semantics=("arbitrary", "arbitrary"),
                                             vmem_limit_bytes=VMEM_LIMIT),
        name="gla",
    )(gla, ga, wa2, ba, nw)


def _sb_kernel(q_ref, k_ref, v_ref, g_ref, o_ref, oacc_ref, lacc_ref):
    hb, kb = SB_HALF, SB_BLOCK
    pairs = SB_HEADS // 2
    halves = SB_QTILE // hb
    q_base = pl.program_id(1) * SB_QTILE
    row = lax.broadcasted_iota(jnp.int32, (hb, kb), 0)
    lane = lax.broadcasted_iota(jnp.int32, (hb, kb), 1)
    low = lane < SB_DH
    rk = lax.broadcasted_iota(jnp.int32, (kb, 2 * kb), 0)
    ck = lax.broadcasted_iota(jnp.int32, (kb, 2 * kb), 1)
    tri_ones = jnp.where((rk > ck) | (ck >= kb), 1.0, 0.0).astype(BF16)

    def pair_rows(m, p):
        st = m * pairs + p
        return slice(st * 2 * hb, (st + 1) * 2 * hb)

    qneg = {}
    for m in range(halves):
        for p in range(pairs):
            qn = -q_ref[0, m * hb:(m + 1) * hb, p * LANES:(p + 1) * LANES]
            zero = jnp.zeros_like(qn)
            qneg[m, p] = jnp.concatenate([jnp.where(low, qn, zero), jnp.where(low, zero, qn)], axis=0)
    def lead_group(start, valid):
        rows = {m: pl.ds(start[m] if isinstance(start[m], int) else pl.multiple_of(start[m], hb), 2 * kb)
                for m in start}
        r4 = lax.broadcasted_iota(jnp.int32, (2 * kb, 2 * kb), 0)
        c4 = lax.broadcasted_iota(jnp.int32, (2 * kb, 2 * kb), 1)
        later = ((r4 > c4) & ((r4 >= kb) == (c4 >= kb))) | ((r4 < kb) & (c4 >= kb))
        tri_pair = jnp.where(later, 1.0, 0.0).astype(BF16)
        keep1 = jnp.where(valid, 1.0, 0.0).astype(BF16)
        keep = jnp.concatenate([keep1, keep1], axis=0)
        log_beta, lf, sums = {}, {}, {}

        def scores(m):
            for p in range(pairs):
                s = _dot_nt(qneg[m, p], k_ref[0, rows[m], p * LANES:(p + 1) * LANES])
                for g in (1, 0):
                    sg = s[:, g * kb:(g + 1) * kb].astype(BF16)
                    lsn = _log_sigmoid(sg)
                    log_beta[m, p, g] = lsn - sg
                    lf[m, p, g] = lsn * keep if g == 1 else lsn

        def suffix_sums(m):
            lhs = [jnp.concatenate([lf[m, p, 1], lf[m, p, 0]], axis=1) for p in range(pairs)]
            sums[m] = _dot(jnp.concatenate(lhs, axis=0), tri_pair)

        def weights(m):
            for p in range(pairs):
                part = sums[m][p * 2 * hb:(p + 1) * 2 * hb]
                w1 = jnp.exp(part[:, :kb].astype(BF16) + log_beta[m, p, 1]) * keep
                w0 = jnp.exp(part[:, kb:].astype(BF16) + log_beta[m, p, 0])
                total = part[:, kb:kb + 1] + lf[m, p, 0][:, 0:1].astype(F32)
                lacc_ref[pair_rows(m, p)] = jnp.broadcast_to(total, (2 * hb, kb))
                oacc_ref[pair_rows(m, p)] = _dot(jnp.concatenate([w0, w1], axis=1),
                                                 v_ref[0, rows[m], p * LANES:(p + 1) * LANES])

        _emit_pipelined((scores, suffix_sums, weights), list(start))

    def group(start, n, valid):
        order = list(reversed(range(n)))
        rows = {m: pl.ds(start[m] if isinstance(start[m], int) else pl.multiple_of(start[m], hb), n * kb)
                for m in start}
        log_beta, lf, sums = {}, {}, {}
        keep1 = jnp.where(valid, 1.0, 0.0).astype(BF16)
        keep = jnp.concatenate([keep1, keep1], axis=0)
        parts = [(p, g) for g in order for p in range(pairs)]

        def scores(m):
            s = {p: _dot_nt(qneg[m, p], k_ref[0, rows[m], p * LANES:(p + 1) * LANES])
                 for p in range(pairs)}
            for p, g in parts:
                sg = s[p][:, g * kb:(g + 1) * kb].astype(BF16)
                lsn = _log_sigmoid(sg)
                log_beta[m, p, g] = lsn - sg
                lf[m, p, g] = lsn * keep if g == n - 1 else lsn

        def suffix_sums(m):
            sums[m] = _dot(jnp.concatenate([lf[m, p, g] for p, g in parts], axis=0), tri_ones)

        def weights(m):
            acc = {p: lacc_ref[pair_rows(m, p)] for p in range(pairs)}
            ws = {p: [None] * n for p in range(pairs)}
            for idx, (p, g) in enumerate(parts):
                part = sums[m][idx * 2 * hb:(idx + 1) * 2 * hb]
                w = jnp.exp((part[:, :kb] + acc[p]).astype(BF16) + log_beta[m, p, g])
                ws[p][g] = w * keep if g == n - 1 else w
                acc[p] = acc[p] + part[:, kb:]
            for p in range(pairs):
                lacc_ref[pair_rows(m, p)] = acc[p]
                oacc_ref[pair_rows(m, p)] += _dot(jnp.concatenate(ws[p], axis=1),
                                                  v_ref[0, rows[m], p * LANES:(p + 1) * LANES])

        _emit_pipelined((scores, suffix_sums, weights), list(start))

    def sweep(m, bound):
        per_m = 2 * pairs * hb

        def live():
            return jnp.max(lacc_ref[m * per_m:(m + 1) * per_m])

        def cond(c):
            return jnp.logical_and(c[0] > 0, c[1] > SB_SKIP_LOG)

        def body(c):
            first = jnp.maximum(c[0] - kb, 0)
            group({m: first}, 1, lane < c[0] - first)
            return first, live()

        lax.while_loop(cond, body, (bound, live()))

    def lead_and_sweep(base, ms):
        lead = {m: base + (m + 1) * hb - SB_LEAD_KEYS for m in ms}
        lead_group(lead, lane < row + (kb - hb))
        rows_all = slice(ms[0] * 2 * pairs * hb, (ms[-1] + 1) * 2 * pairs * hb)

        @pl.when(jnp.max(lacc_ref[rows_all]) > SB_SKIP_LOG)
        def _():
            for m in ms:
                sweep(m, lead[m])

    short = (SB_LEAD_KEYS - hb) // hb

    @pl.when(q_base > 0)
    def _():
        lead_and_sweep(q_base, list(range(halves)))

    @pl.when(q_base == 0)
    def _():
        for m in range(short):
            oacc_ref[m * 2 * pairs * hb:(m + 1) * 2 * pairs * hb] = jnp.zeros((2 * pairs * hb, LANES), F32)
            lacc_ref[m * 2 * pairs * hb:(m + 1) * 2 * pairs * hb] = jnp.zeros((2 * pairs * hb, LANES), F32)
            first = max(m * hb + hb - kb, 0)
            group({m: first}, 1, lane < row + (m * hb - first))
            sweep(m, first)
        lead_and_sweep(0, list(range(short, halves)))

    for m in range(halves):
        for p in range(pairs):
            both = oacc_ref[pair_rows(m, p)]
            o = jnp.where(low, both[:hb], both[hb:])
            gate = g_ref[0, m * hb:(m + 1) * hb, p * LANES:(p + 1) * LANES].astype(F32)
            o_ref[0, m * hb:(m + 1) * hb, p * LANES:(p + 1) * LANES] = (o * _silu(gate)).astype(BF16)


def _sb(sb, c3):
    b, t, _ = sb.shape
    n_streams = (SB_QTILE // SB_HALF) * SB_HEADS
    return pl.pallas_call(
        _sb_kernel,
        grid=(b, t // SB_QTILE),
        in_specs=[pl.BlockSpec((1, SB_QTILE, SB_W), lambda bi, i: (bi, i, 0)),
                  pl.BlockSpec((1, t, SB_W), lambda bi, i: (bi, 0, 1)),
                  pl.BlockSpec((1, t, SB_W), lambda bi, i: (bi, 0, 2)),
                  pl.BlockSpec((1, SB_QTILE, SB_W), lambda bi, i: (bi, i, 0))],
        out_specs=pl.BlockSpec((1, SB_QTILE, SB_W), lambda bi, i: (bi, i, 0)),
        out_shape=jax.ShapeDtypeStruct((b, t, SB_W), BF16),
        scratch_shapes=[pltpu.VMEM((n_streams * SB_HALF, LANES), F32),
                        pltpu.VMEM((n_streams * SB_HALF, LANES), F32)],
        compiler_params=pltpu.CompilerParams(dimension_semantics=("arbitrary", "arbitrary"),
                                             vmem_limit_bytes=VMEM_LIMIT),
        name="sb",
    )(sb, sb, sb, c3)


def _out_kernel(x_ref, mixg_ref, mixs_ref, mq_ref, mg_ref, kvm_ref, wout_ref, lng_ref, lnb_ref, o_ref, *, tm, sub):
    wout_ref = wout_ref.at[0]
    low = lax.broadcasted_iota(jnp.int32, (sub, LANES), 1) < MEM_DH
    pairs = range(MEM_HEADS // 2)
    s, e, den, y = {}, {}, {}, {}

    def rows(j):
        return slice(j * sub, (j + 1) * sub)

    def scores(j):
        for p in pairs:
            qp = mq_ref[0, rows(j), p * LANES:(p + 1) * LANES]
            km = kvm_ref[0, 0, :, p * LANES:(p + 1) * LANES]
            zero = jnp.zeros_like(qp)
            s[j, p, 0] = _dot_nt(jnp.where(low, qp, zero), km)
            s[j, p, 1] = _dot_nt(jnp.where(low, zero, qp), km)

    def softmax(j):
        for p in pairs:
            for hh in range(2):
                sc = s[j, p, hh]
                ex = jnp.exp(sc - jnp.max(sc, axis=-1, keepdims=True))
                den[j, p, hh] = jnp.sum(ex, axis=-1, keepdims=True)
                e[j, p, hh] = ex.astype(BF16)

    def mix(j):
        mixm = []
        for p in pairs:
            vm = kvm_ref[0, 0, :, MEM_W + p * LANES:MEM_W + (p + 1) * LANES]
            om = jnp.where(low, _dot(e[j, p, 0], vm) / den[j, p, 0], _dot(e[j, p, 1], vm) / den[j, p, 1])
            gate = mg_ref[0, rows(j), p * LANES:(p + 1) * LANES].astype(F32)
            mixm.append((om * _silu(gate)).astype(BF16))
        y[j] = _dot(jnp.concatenate([mixg_ref[0, rows(j), :], mixs_ref[0, rows(j), :]] + mixm, axis=1), wout_ref[...])

    def norm(j):
        r = ALPHA * x_ref[0, rows(j), :] + y[j]
        mu = jnp.mean(r, axis=-1, keepdims=True)
        d = r - mu
        var = jnp.mean(d * d, axis=-1, keepdims=True)
        o_ref[0, rows(j), :] = d * lax.rsqrt(var + LN_EPS) * lng_ref[0] + lnb_ref[0]

    _emit_pipelined((scores, softmax, mix, norm), list(range(tm // sub)))


def _out(x, mixg, mixs, c3, kvm, layer, wout, lng, lnb, tm=OUT_TILE, sub=ROW_TILE):
    b, t, _ = x.shape
    return pl.pallas_call(
        functools.partial(_out_kernel, tm=tm, sub=sub),
        grid=(b, t // tm),
        in_specs=[pl.BlockSpec((1, tm, D_MODEL), lambda i, j: (i, j, 0)),
                  pl.BlockSpec((1, tm, GLA_V_W), lambda i, j: (i, j, 0)),
                  pl.BlockSpec((1, tm, SB_W), lambda i, j: (i, j, 0)),
                  pl.BlockSpec((1, tm, MEM_W), lambda i, j: (i, j, SB_W // MEM_W)),
                  pl.BlockSpec((1, tm, MEM_W), lambda i, j: (i, j, SB_W // MEM_W + 1)),
                  pl.BlockSpec((1, 1, N_MEM, 2 * MEM_W), lambda i, j: (layer, i, 0, 0)),
                  pl.BlockSpec((1, D_MODEL, D_MODEL), lambda i, j: (layer, 0, 0), pipeline_mode=pl.Buffered(1)),
                  pl.BlockSpec((1, 1, D_MODEL), lambda i, j: (layer, 0, 0)),
                  pl.BlockSpec((1, 1, D_MODEL), lambda i, j: (layer, 0, 0))],
        out_specs=pl.BlockSpec((1, tm, D_MODEL), lambda i, j: (i, j, 0)),
        out_shape=jax.ShapeDtypeStruct((b, t, D_MODEL), F32),
        compiler_params=pltpu.CompilerParams(dimension_semantics=("arbitrary", "arbitrary"),
                                             vmem_limit_bytes=VMEM_LIMIT),
        name="out",
    )(x, mixg, mixs, c3, c3, kvm, wout, lng, lnb)


_W_IN_SEGMENTS = (("gq", GLA_QK_W), ("gk", GLA_QK_W), ("gv", GLA_V_W), ("gg", GLA_V_W), ("ga", GLA_RANK),
                  ("sq", SB_W), ("sk", SB_W), ("sv", SB_W), ("sg", SB_W), ("mq", MEM_W), ("mg", MEM_W))
_W_PROJ_ORDER = ("gq", "gk", "gv", "gg", "sq", "sk", "sv", "sg", "mq", "mg", "ga")
_Q_SCALED = {"gq": GLA_DK ** -0.5, "sq": SB_DH ** -0.5, "mq": MEM_DH ** -0.5}
D_IN = sum(n for _, n in _W_IN_SEGMENTS)


def _wprep_kernel(wt_ref, o_ref):
    src, off = {}, 0
    for name, n in _W_IN_SEGMENTS:
        src[name] = (off, n)
        off += n
    dst = 0
    for name in _W_PROJ_ORDER:
        lo, n = src[name]
        rows = max(n, LANES)
        cols = wt_ref[0, lo:lo + rows, :].T
        if name in _Q_SCALED:
            cols = cols * _Q_SCALED[name]
        if rows > n:
            cols = jnp.where(lax.broadcasted_iota(jnp.int32, cols.shape, 1) < n, cols, 0.0)
        o_ref[0, :, dst:dst + rows] = cols.astype(BF16)
        dst += rows
    assert dst == PROJ_COLS


def _regroup_w_in(wt, tk=256):
    depth, _, d = wt.shape
    return pl.pallas_call(
        _wprep_kernel,
        grid=(depth, d // tk),
        in_specs=[pl.BlockSpec((1, D_IN, tk), lambda l, i: (l, 0, i))],
        out_specs=pl.BlockSpec((1, tk, PROJ_COLS), lambda l, i: (l, i, 0)),
        out_shape=jax.ShapeDtypeStruct((depth, d, PROJ_COLS), BF16),
        compiler_params=pltpu.CompilerParams(dimension_semantics=("arbitrary", "arbitrary"),
                                             vmem_limit_bytes=VMEM_LIMIT),
        name="wprep",
    )(wt)


def kernel(x, mem, w_in, w_alpha2, b_alpha, gla_norm_w, w_mem_kv, w_out, ln_g, ln_b):
    b, t, d = x.shape
    assert d == D_MODEL and mem.shape == (b, N_MEM, d) and w_in.shape == (DEPTH, d, D_IN)
    assert t % GLA_TILE == 0 and t % SB_QTILE == 0 and t % OUT_TILE == 0 and (b * t) % PROJ_TILE == 0
    kvm = _memkv(mem.reshape(b * N_MEM, d), w_mem_kv.astype(BF16)).reshape(DEPTH, b, N_MEM, 2 * MEM_W)
    w_proj = _regroup_w_in(jnp.swapaxes(w_in, 1, 2))
    wa2 = jnp.pad(w_alpha2.astype(BF16), ((0, 0), (0, GA_COLS - GLA_RANK), (0, 0)))
    ba = b_alpha.reshape(DEPTH, 1, GLA_QK_W)
    nw = gla_norm_w.reshape(DEPTH, 1, GLA_DV)
    wout = w_out.astype(BF16)
    lng = ln_g.reshape(DEPTH, 1, d)
    lnb = ln_b.reshape(DEPTH, 1, d)
    for l in range(DEPTH):
        gla, sb, c3, ga = _proj(x.reshape(b * t, d), w_proj, l)
        mixg = _gla(gla.reshape(b, t, GLA_COLS), ga.reshape(b, t, GA_COLS), wa2, ba, nw, l)
        c3 = c3.reshape(b, t, C3_COLS)
        mixs = _sb(sb.reshape(b, t, SB_COLS), c3)
        x = _out(x, mixg, mixs, c3, kvm, l, wout, lng, lnb)
    return x
```

```python
import functools

import jax
import jax.numpy as jnp
from jax import lax
from jax.experimental import pallas as pl
from jax.experimental.pallas import tpu as pltpu

F32 = jnp.float32
BF16 = jnp.bfloat16

D_MODEL = 1024
DEPTH = 4
N_MEM = 256
GLA_HEADS = 4
GLA_DK = 64
GLA_DV = 128
GLA_RANK = 16
GLA_GATE_NORM = 16.0
GLA_CHUNK = 64
SB_HEADS = 4
SB_DH = 64
SB_BLOCK = 128
MEM_HEADS = 4
MEM_DH = 64
GLA_QK_W = GLA_HEADS * GLA_DK
GLA_V_W = GLA_HEADS * GLA_DV
SB_W = SB_HEADS * SB_DH
MEM_W = MEM_HEADS * MEM_DH
ALPHA = (2.0 * DEPTH) ** 0.25
LN_EPS = 1e-5
RMS_EPS = 1e-6

LANES = 128
GLA_COLS = 2 * GLA_QK_W + 2 * GLA_V_W
SB_COLS = 3 * SB_W
C3_COLS = SB_W + 2 * MEM_W
GA_COLS = LANES
PROJ_COLS = GLA_COLS + SB_COLS + C3_COLS + GA_COLS
VMEM_LIMIT = 56 * 1024 * 1024
SB_SKIP_LOG = -106.0
SB_HALF = 64
SB_LEAD_KEYS = 2 * SB_BLOCK
SB_QTILE = 2048
GLA_TILE = 4096
ROW_TILE = 512
OUT_TILE = 1024
PROJ_TILE = 1024
LOG2E = 1.4426950408889634


def _dot(a, b):
    return jnp.dot(a, b, preferred_element_type=F32)


def _dot_nt(a, b):
    return lax.dot_general(a, b, (((1,), (1,)), ((), ())), preferred_element_type=F32)


def _dot_tn(a, b):
    return lax.dot_general(a, b, (((0,), (0,)), ((), ())), preferred_element_type=F32)


def _split_bf16(x, n):
    parts = []
    for _ in range(n - 1):
        h = x.astype(BF16)
        parts.append(h)
        x = x - h.astype(F32)
    parts.append(x.astype(BF16))
    return parts


def _exp_neg(x):
    return jnp.exp2(x * -LOG2E)


def _log_sigmoid(z):
    return jnp.minimum(z, 0.0) - jnp.log(1.0 + _exp_neg(jnp.abs(z)))


def _silu(g):
    return g / (1.0 + _exp_neg(g))


def _emit_pipelined(stages, items):
    for step in range(len(items) + len(stages) - 1):
        for depth, stage in enumerate(stages):
            j = step - depth
            if 0 <= j < len(items):
                stage(items[j])


def _proj_kernel(x_ref, w_ref, gla_ref, sb_ref, c3_ref, ga_ref):
    xb = x_ref[...].astype(BF16)
    w_ref = w_ref.at[0]
    off = 0
    for ref, width in ((gla_ref, GLA_COLS), (sb_ref, SB_COLS), (c3_ref, C3_COLS), (ga_ref, GA_COLS)):
        step = 512 if width % 512 == 0 else width if width < 512 else 256
        for c in range(0, width, step):
            ref[:, c:c + step] = _dot(xb, w_ref[:, off + c:off + c + step]).astype(BF16)
        off += width


def _proj(x2, w, layer, tm=PROJ_TILE):
    n = x2.shape[0]
    outs = (GLA_COLS, SB_COLS, C3_COLS, GA_COLS)
    return pl.pallas_call(
        _proj_kernel,
        grid=(n // tm,),
        in_specs=[pl.BlockSpec((tm, D_MODEL), lambda i: (i, 0)),
                  pl.BlockSpec((1, D_MODEL, PROJ_COLS), lambda i: (layer, 0, 0))],
        out_specs=[pl.BlockSpec((tm, c), lambda i: (i, 0)) for c in outs],
        out_shape=[jax.ShapeDtypeStruct((n, c), BF16) for c in outs],
        compiler_params=pltpu.CompilerParams(dimension_semantics=("arbitrary",),
                                             vmem_limit_bytes=VMEM_LIMIT),
        name="proj",
    )(x2, w)


def _memkv_kernel(m_ref, w_ref, o_ref):
    o_ref[0] = _dot(m_ref[...].astype(BF16), w_ref[0]).astype(BF16)


def _memkv(mem2, w_mkv):
    n = mem2.shape[0]
    return pl.pallas_call(
        _memkv_kernel,
        grid=(DEPTH,),
        in_specs=[pl.BlockSpec((n, D_MODEL), lambda l: (0, 0)),
                  pl.BlockSpec((1, D_MODEL, 2 * MEM_W), lambda l: (l, 0, 0))],
        out_specs=pl.BlockSpec((1, n, 2 * MEM_W), lambda l: (l, 0, 0)),
        out_shape=jax.ShapeDtypeStruct((DEPTH, n, 2 * MEM_W), BF16),
        compiler_params=pltpu.CompilerParams(dimension_semantics=("arbitrary",),
                                             vmem_limit_bytes=VMEM_LIMIT),
        name="memkv",
    )(mem2, w_mkv)


def _gla_kernel(gla_ref, ga_ref, wa2_ref, ba_ref, nw_ref, out_ref, st_ref, *, tt):
    c_sz = GLA_CHUNK

    @pl.when(pl.program_id(1) == 0)
    def _():
        st_ref[...] = jnp.zeros_like(st_ref)

    row = lax.broadcasted_iota(jnp.int32, (c_sz, c_sz), 0)
    col = lax.broadcasted_iota(jnp.int32, (c_sz, c_sz), 1)
    causal = col <= row
    r2 = lax.broadcasted_iota(jnp.int32, (c_sz, 2 * c_sz), 0)
    c2 = lax.broadcasted_iota(jnp.int32, (c_sz, 2 * c_sz), 1) & (c_sz - 1)
    tri_incl2 = jnp.where(c2 <= r2, 1.0, 0.0).astype(BF16)
    low_c = lax.broadcasted_iota(jnp.int32, (c_sz, LANES), 1) < GLA_DK
    chunks = range(tt // c_sz)
    pairs = range(GLA_HEADS // 2)
    rows = [slice(c * c_sz, (c + 1) * c_sz) for c in chunks]
    lanes = [slice(p * LANES, (p + 1) * LANES) for p in pairs]
    v_col = 2 * GLA_QK_W
    g_col = v_col + GLA_V_W

    zz = _dot(ga_ref[0], wa2_ref[0]) + ba_ref[0]
    log_a = _log_sigmoid(zz) * (1.0 / GLA_GATE_NORM)
    hi, lo = _split_bf16(log_a, 2)
    g_cum, kd, decay, q2, a2, upd, inter2 = {}, {}, {}, {}, {}, {}, {}
    state = {p: st_ref[p] for p in pairs}
    norm_w = nw_ref[0] * GLA_DV ** 0.5

    def v_head(c, h):
        return gla_ref[0, rows[c], v_col + h * GLA_DV:v_col + (h + 1) * GLA_DV]

    def cumsum(c):
        g_cum[c] = _dot(tri_incl2, jnp.concatenate([hi[rows[c]], lo[rows[c]]], axis=0))

    def scale(c):
        g_last = g_cum[c][c_sz - 1:c_sz, :]
        q = gla_ref[0, rows[c], 0:GLA_QK_W].astype(F32)
        k = gla_ref[0, rows[c], GLA_QK_W:2 * GLA_QK_W].astype(F32)
        decay[c] = jnp.exp(g_last)
        qg = (q * jnp.exp(g_cum[c])).astype(BF16)
        kgf = k * _exp_neg(g_cum[c])
        kg = kgf.astype(BF16)
        kd[c] = (kgf * decay[c]).astype(BF16)
        for p in pairs:
            qgp = qg[:, lanes[p]]
            q2[c, p] = jnp.concatenate([jnp.where(low_c, qgp, jnp.zeros_like(qgp)),
                                        jnp.where(low_c, jnp.zeros_like(qgp), qgp)], axis=0)
            a2[c, p] = _dot_nt(q2[c, p], kg[:, lanes[p]])

    def update(c):
        for p in pairs:
            kdp = kd[c][:, lanes[p]]
            zero = jnp.zeros_like(kdp)
            upd[c, p] = _dot_tn(jnp.concatenate([v_head(c, 2 * p), v_head(c, 2 * p + 1)], axis=0),
                                jnp.concatenate([jnp.where(low_c, kdp, zero), jnp.where(low_c, zero, kdp)], axis=0))

    def recur(c):
        for p in pairs:
            inter2[c, p] = _dot_nt(q2[c, p], state[p].astype(BF16))
            state[p] = state[p] * decay[c][:, lanes[p]] + upd[c, p]

    def finish(c):
        for p in pairs:
            for hh in range(2):
                h = 2 * p + hh
                a = jnp.where(causal, a2[c, p][hh * c_sz:(hh + 1) * c_sz], 0.0).astype(BF16)
                o = _dot(a, v_head(c, h)) + inter2[c, p][hh * c_sz:(hh + 1) * c_sz]
                ssq = jnp.sum(o * o, axis=-1, keepdims=True)
                gate = gla_ref[0, rows[c], g_col + h * GLA_DV:g_col + (h + 1) * GLA_DV].astype(F32)
                res = o * lax.rsqrt(ssq + GLA_DV * RMS_EPS) * norm_w * _silu(gate)
                out_ref[0, rows[c], h * GLA_DV:(h + 1) * GLA_DV] = res.astype(BF16)

    _emit_pipelined((cumsum, scale, update, recur, finish), list(chunks))
    for p in pairs:
        st_ref[p] = state[p]


def _gla(gla, ga, wa2, ba, nw, layer, tt=GLA_TILE):
    b, t, _ = gla.shape
    return pl.pallas_call(
        functools.partial(_gla_kernel, tt=tt),
        grid=(b, t // tt),
        in_specs=[pl.BlockSpec((1, tt, GLA_COLS), lambda i, j: (i, j, 0)),
                  pl.BlockSpec((1, tt, GA_COLS), lambda i, j: (i, j, 0)),
                  pl.BlockSpec((1, GA_COLS, GLA_QK_W), lambda i, j: (layer, 0, 0)),
                  pl.BlockSpec((1, 1, GLA_QK_W), lambda i, j: (layer, 0, 0)),
                  pl.BlockSpec((1, 1, GLA_DV), lambda i, j: (layer, 0, 0))],
        out_specs=pl.BlockSpec((1, tt, GLA_V_W), lambda i, j: (i, j, 0)),
        out_shape=jax.ShapeDtypeStruct((b, t, GLA_V_W), BF16),
        scratch_shapes=[pltpu.VMEM((GLA_HEADS // 2, GLA_DV, LANES), F32)],
        compiler_params=pltpu.CompilerParams(dimension_semantics=("arbitrary", "arbitrary"),
                                             vmem_limit_bytes=VMEM_LIMIT),
        name="gla",
    )(gla, ga, wa2, ba, nw)


def _sb_kernel(q_ref, k_ref, v_ref, g_ref, o_ref, oacc_ref, lacc_ref):
    hb, kb = SB_HALF, SB_BLOCK
    pairs = SB_HEADS // 2
    halves = SB_QTILE // hb
    q_base = pl.program_id(1) * SB_QTILE
    row = lax.broadcasted_iota(jnp.int32, (hb, kb), 0)
    lane = lax.broadcasted_iota(jnp.int32, (hb, kb), 1)
    low = lane < SB_DH
    rk = lax.broadcasted_iota(jnp.int32, (kb, 2 * kb), 0)
    ck = lax.broadcasted_iota(jnp.int32, (kb, 2 * kb), 1)
    tri_ones = jnp.where((rk > ck) | (ck >= kb), 1.0, 0.0).astype(BF16)

    def pair_rows(m, p):
        st = m * pairs + p
        return slice(st * 2 * hb, (st + 1) * 2 * hb)

    qneg = {}
    for m in range(halves):
        for p in range(pairs):
            qn = -q_ref[0, m * hb:(m + 1) * hb, p * LANES:(p + 1) * LANES]
            zero = jnp.zeros_like(qn)
            qneg[m, p] = jnp.concatenate([jnp.where(low, qn, zero), jnp.where(low, zero, qn)], axis=0)
    def lead_group(start, valid):
        rows = {m: pl.ds(start[m] if isinstance(start[m], int) else pl.multiple_of(start[m], hb), 2 * kb)
                for m in start}
        r4 = lax.broadcasted_iota(jnp.int32, (2 * kb, 2 * kb), 0)
        c4 = lax.broadcasted_iota(jnp.int32, (2 * kb, 2 * kb), 1)
        later = ((r4 > c4) & ((r4 >= kb) == (c4 >= kb))) | ((r4 < kb) & (c4 >= kb))
        tri_pair = jnp.where(later, 1.0, 0.0).astype(BF16)
        keep1 = jnp.where(valid, 1.0, 0.0).astype(BF16)
        keep = jnp.concatenate([keep1, keep1], axis=0)
        log_beta, lf, sums = {}, {}, {}

        def scores(m):
            for p in range(pairs):
                s = _dot_nt(qneg[m, p], k_ref[0, rows[m], p * LANES:(p + 1) * LANES])
                for g in (1, 0):
                    sg = s[:, g * kb:(g + 1) * kb].astype(BF16)
                    lsn = _log_sigmoid(sg)
                    log_beta[m, p, g] = lsn - sg
                    lf[m, p, g] = lsn * keep if g == 1 else lsn

        def suffix_sums(m):
            lhs = [jnp.concatenate([lf[m, p, 1], lf[m, p, 0]], axis=1) for p in range(pairs)]
            sums[m] = _dot(jnp.concatenate(lhs, axis=0), tri_pair)

        def weights(m):
            for p in range(pairs):
                part = sums[m][p * 2 * hb:(p + 1) * 2 * hb]
                w1 = jnp.exp(part[:, :kb].astype(BF16) + log_beta[m, p, 1]) * keep
                w0 = jnp.exp(part[:, kb:].astype(BF16) + log_beta[m, p, 0])
                total = part[:, kb:kb + 1] + lf[m, p, 0][:, 0:1].astype(F32)
                lacc_ref[pair_rows(m, p)] = jnp.broadcast_to(total, (2 * hb, kb))
                oacc_ref[pair_rows(m, p)] = _dot(jnp.concatenate([w0, w1], axis=1),
                                                 v_ref[0, rows[m], p * LANES:(p + 1) * LANES])

        _emit_pipelined((scores, suffix_sums, weights), list(start))

    def group(start, n, valid):
        order = list(reversed(range(n)))
        rows = {m: pl.ds(start[m] if isinstance(start[m], int) else pl.multiple_of(start[m], hb), n * kb)
                for m in start}
        log_beta, lf, sums = {}, {}, {}
        keep1 = jnp.where(valid, 1.0, 0.0).astype(BF16)
        keep = jnp.concatenate([keep1, keep1], axis=0)
        parts = [(p, g) for g in order for p in range(pairs)]

        def scores(m):
            s = {p: _dot_nt(qneg[m, p], k_ref[0, rows[m], p * LANES:(p + 1) * LANES])
                 for p in range(pairs)}
            for p, g in parts:
                sg = s[p][:, g * kb:(g + 1) * kb].astype(BF16)
                lsn = _log_sigmoid(sg)
                log_beta[m, p, g] = lsn - sg
                lf[m, p, g] = lsn * keep if g == n - 1 else lsn

        def suffix_sums(m):
            sums[m] = _dot(jnp.concatenate([lf[m, p, g] for p, g in parts], axis=0), tri_ones)

        def weights(m):
            acc = {p: lacc_ref[pair_rows(m, p)] for p in range(pairs)}
            ws = {p: [None] * n for p in range(pairs)}
            for idx, (p, g) in enumerate(parts):
                part = sums[m][idx * 2 * hb:(idx + 1) * 2 * hb]
                w = jnp.exp((part[:, :kb] + acc[p]).astype(BF16) + log_beta[m, p, g])
                ws[p][g] = w * keep if g == n - 1 else w
                acc[p] = acc[p] + part[:, kb:]
            for p in range(pairs):
                lacc_ref[pair_rows(m, p)] = acc[p]
                oacc_ref[pair_rows(m, p)] += _dot(jnp.concatenate(ws[p], axis=1),
                                                  v_ref[0, rows[m], p * LANES:(p + 1) * LANES])

        _emit_pipelined((scores, suffix_sums, weights), list(start))

    def sweep(m, bound):
        per_m = 2 * pairs * hb

        def live():
            return jnp.max(lacc_ref[m * per_m:(m + 1) * per_m])

        def cond(c):
            return jnp.logical_and(c[0] > 0, c[1] > SB_SKIP_LOG)

        def body(c):
            first = jnp.maximum(c[0] - kb, 0)
            group({m: first}, 1, lane < c[0] - first)
            return first, live()

        lax.while_loop(cond, body, (bound, live()))

    def lead_and_sweep(base, ms):
        lead = {m: base + (m + 1) * hb - SB_LEAD_KEYS for m in ms}
        lead_group(lead, lane < row + (kb - hb))
        rows_all = slice(ms[0] * 2 * pairs * hb, (ms[-1] + 1) * 2 * pairs * hb)

        @pl.when(jnp.max(lacc_ref[rows_all]) > SB_SKIP_LOG)
        def _():
            for m in ms:
                sweep(m, lead[m])

    short = (SB_LEAD_KEYS - hb) // hb

    @pl.when(q_base > 0)
    def _():
        lead_and_sweep(q_base, list(range(halves)))

    @pl.when(q_base == 0)
    def _():
        for m in range(short):
            oacc_ref[m * 2 * pairs * hb:(m + 1) * 2 * pairs * hb] = jnp.zeros((2 * pairs * hb, LANES), F32)
            lacc_ref[m * 2 * pairs * hb:(m + 1) * 2 * pairs * hb] = jnp.zeros((2 * pairs * hb, LANES), F32)
            first = max(m * hb + hb - kb, 0)
            group({m: first}, 1, lane < row + (m * hb - first))
            sweep(m, first)
        lead_and_sweep(0, list(range(short, halves)))

    for m in range(halves):
        for p in range(pairs):
            both = oacc_ref[pair_rows(m, p)]
            o = jnp.where(low, both[:hb], both[hb:])
            gate = g_ref[0, m * hb:(m + 1) * hb, p * LANES:(p + 1) * LANES].astype(F32)
            o_ref[0, m * hb:(m + 1) * hb, p * LANES:(p + 1) * LANES] = (o * _silu(gate)).astype(BF16)


def _sb(sb, c3):
    b, t, _ = sb.shape
    n_streams = (SB_QTILE // SB_HALF) * SB_HEADS
    return pl.pallas_call(
        _sb_kernel,
        grid=(b, t // SB_QTILE),
        in_specs=[pl.BlockSpec((1, SB_QTILE, SB_W), lambda bi, i: (bi, i, 0)),
                  pl.BlockSpec((1, t, SB_W), lambda bi, i: (bi, 0, 1)),
                  pl.BlockSpec((1, t, SB_W), lambda bi, i: (bi, 0, 2)),
                  pl.BlockSpec((1, SB_QTILE, SB_W), lambda bi, i: (bi, i, 0))],
        out_specs=pl.BlockSpec((1, SB_QTILE, SB_W), lambda bi, i: (bi, i, 0)),
        out_shape=jax.ShapeDtypeStruct((b, t, SB_W), BF16),
        scratch_shapes=[pltpu.VMEM((n_streams * SB_HALF, LANES), F32),
                        pltpu.VMEM((n_streams * SB_HALF, LANES), F32)],
        compiler_params=pltpu.CompilerParams(dimension_semantics=("arbitrary", "arbitrary"),
                                             vmem_limit_bytes=VMEM_LIMIT),
        name="sb",
    )(sb, sb, sb, c3)


def _out_kernel(x_ref, mixg_ref, mixs_ref, mq_ref, mg_ref, kvm_ref, wout_ref, lng_ref, lnb_ref, o_ref, *, tm, sub):
    wout_ref = wout_ref.at[0]
    low = lax.broadcasted_iota(jnp.int32, (sub, LANES), 1) < MEM_DH
    pairs = range(MEM_HEADS // 2)
    s, e, den, y = {}, {}, {}, {}

    def rows(j):
        return slice(j * sub, (j + 1) * sub)

    def scores(j):
        for p in pairs:
            qp = mq_ref[0, rows(j), p * LANES:(p + 1) * LANES]
            km = kvm_ref[0, 0, :, p * LANES:(p + 1) * LANES]
            zero = jnp.zeros_like(qp)
            s[j, p, 0] = _dot_nt(jnp.where(low, qp, zero), km)
            s[j, p, 1] = _dot_nt(jnp.where(low, zero, qp), km)

    def softmax(j):
        for p in pairs:
            for hh in range(2):
                sc = s[j, p, hh]
                ex = jnp.exp(sc - jnp.max(sc, axis=-1, keepdims=True))
                den[j, p, hh] = jnp.sum(ex, axis=-1, keepdims=True)
                e[j, p, hh] = ex.astype(BF16)

    def mix(j):
        mixm = []
        for p in pairs:
            vm = kvm_ref[0, 0, :, MEM_W + p * LANES:MEM_W + (p + 1) * LANES]
            om = jnp.where(low, _dot(e[j, p, 0], vm) / den[j, p, 0], _dot(e[j, p, 1], vm) / den[j, p, 1])
            gate = mg_ref[0, rows(j), p * LANES:(p + 1) * LANES].astype(F32)
            mixm.append((om * _silu(gate)).astype(BF16))
        y[j] = _dot(jnp.concatenate([mixg_ref[0, rows(j), :], mixs_ref[0, rows(j), :]] + mixm, axis=1), wout_ref[...])

    def norm(j):
        r = ALPHA * x_ref[0, rows(j), :] + y[j]
        mu = jnp.mean(r, axis=-1, keepdims=True)
        d = r - mu
        var = jnp.mean(d * d, axis=-1, keepdims=True)
        o_ref[0, rows(j), :] = d * lax.rsqrt(var + LN_EPS) * lng_ref[0] + lnb_ref[0]

    _emit_pipelined((scores, softmax, mix, norm), list(range(tm // sub)))


def _out(x, mixg, mixs, c3, kvm, layer, wout, lng, lnb, tm=OUT_TILE, sub=ROW_TILE):
    b, t, _ = x.shape
    return pl.pallas_call(
        functools.partial(_out_kernel, tm=tm, sub=sub),
        grid=(b, t // tm),
        in_specs=[pl.BlockSpec((1, tm, D_MODEL), lambda i, j: (i, j, 0)),
                  pl.BlockSpec((1, tm, GLA_V_W), lambda i, j: (i, j, 0)),
                  pl.BlockSpec((1, tm, SB_W), lambda i, j: (i, j, 0)),
                  pl.BlockSpec((1, tm, MEM_W), lambda i, j: (i, j, SB_W // MEM_W)),
                  pl.BlockSpec((1, tm, MEM_W), lambda i, j: (i, j, SB_W // MEM_W + 1)),
                  pl.BlockSpec((1, 1, N_MEM, 2 * MEM_W), lambda i, j: (layer, i, 0, 0)),
                  pl.BlockSpec((1, D_MODEL, D_MODEL), lambda i, j: (layer, 0, 0), pipeline_mode=pl.Buffered(1)),
                  pl.BlockSpec((1, 1, D_MODEL), lambda i, j: (layer, 0, 0)),
                  pl.BlockSpec((1, 1, D_MODEL), lambda i, j: (layer, 0, 0))],
        out_specs=pl.BlockSpec((1, tm, D_MODEL), lambda i, j: (i, j, 0)),
        out_shape=jax.ShapeDtypeStruct((b, t, D_MODEL), F32),
        compiler_params=pltpu.CompilerParams(dimension_semantics=("arbitrary", "arbitrary"),
                                             vmem_limit_bytes=VMEM_LIMIT),
        name="out",
    )(x, mixg, mixs, c3, c3, kvm, wout, lng, lnb)


_W_IN_SEGMENTS = (("gq", GLA_QK_W), ("gk", GLA_QK_W), ("gv", GLA_V_W), ("gg", GLA_V_W), ("ga", GLA_RANK),
                  ("sq", SB_W), ("sk", SB_W), ("sv", SB_W), ("sg", SB_W), ("mq", MEM_W), ("mg", MEM_W))
_W_PROJ_ORDER = ("gq", "gk", "gv", "gg", "sq", "sk", "sv", "sg", "mq", "mg", "ga")
_Q_SCALED = {"gq": GLA_DK ** -0.5, "sq": SB_DH ** -0.5, "mq": MEM_DH ** -0.5}
D_IN = sum(n for _, n in _W_IN_SEGMENTS)


def _wprep_kernel(wt_ref, o_ref):
    src, off = {}, 0
    for name, n in _W_IN_SEGMENTS:
        src[name] = (off, n)
        off += n
    dst = 0
    for name in _W_PROJ_ORDER:
        lo, n = src[name]
        rows = max(n, LANES)
        cols = wt_ref[0, lo:lo + rows, :].T
        if name in _Q_SCALED:
            cols = cols * _Q_SCALED[name]
        if rows > n:
            cols = jnp.where(lax.broadcasted_iota(jnp.int32, cols.shape, 1) < n, cols, 0.0)
        o_ref[0, :, dst:dst + rows] = cols.astype(BF16)
        dst += rows
    assert dst == PROJ_COLS


def _regroup_w_in(wt, tk=256):
    depth, _, d = wt.shape
    return pl.pallas_call(
        _wprep_kernel,
        grid=(depth, d // tk),
        in_specs=[pl.BlockSpec((1, D_IN, tk), lambda l, i: (l, 0, i))],
        out_specs=pl.BlockSpec((1, tk, PROJ_COLS), lambda l, i: (l, i, 0)),
        out_shape=jax.ShapeDtypeStruct((depth, d, PROJ_COLS), BF16),
        compiler_params=pltpu.CompilerParams(dimension_semantics=("arbitrary", "arbitrary"),
                                             vmem_limit_bytes=VMEM_LIMIT),
        name="wprep",
    )(wt)


def kernel(x, mem, w_in, w_alpha2, b_alpha, gla_norm_w, w_mem_kv, w_out, ln_g, ln_b):
    b, t, d = x.shape
    assert d == D_MODEL and mem.shape == (b, N_MEM, d) and w_in.shape == (DEPTH, d, D_IN)
    assert t % GLA_TILE == 0 and t % SB_QTILE == 0 and t % OUT_TILE == 0 and (b * t) % PROJ_TILE == 0
    kvm = _memkv(mem.reshape(b * N_MEM, d), w_mem_kv.astype(BF16)).reshape(DEPTH, b, N_MEM, 2 * MEM_W)
    w_proj = _regroup_w_in(jnp.swapaxes(w_in, 1, 2))
    wa2 = jnp.pad(w_alpha2.astype(BF16), ((0, 0), (0, GA_COLS - GLA_RANK), (0, 0)))
    ba = b_alpha.reshape(DEPTH, 1, GLA_QK_W)
    nw = gla_norm_w.reshape(DEPTH, 1, GLA_DV)
    wout = w_out.astype(BF16)
    lng = ln_g.reshape(DEPTH, 1, d)
    lnb = ln_b.reshape(DEPTH, 1, d)
    for l in range(DEPTH):
        gla, sb, c3, ga = _proj(x.reshape(b * t, d), w_proj, l)
        mixg = _gla(gla.reshape(b, t, GLA_COLS), ga.reshape(b, t, GA_COLS), wa2, ba, nw, l)
        c3 = c3.reshape(b, t, C3_COLS)
        mixs = _sb(sb.reshape(b, t, SB_COLS), c3)
        x = _out(x, mixg, mixs, c3, kvm, l, wout, lng, lnb)
    return x
```

```python
import functools

import jax
import jax.numpy as jnp
from jax import lax
from jax.experimental import pallas as pl
from jax.experimental.pallas import tpu as pltpu

F32 = jnp.float32
BF16 = jnp.bfloat16

D_MODEL = 1024
DEPTH = 4
N_MEM = 256
GLA_HEADS = 4
GLA_DK = 64
GLA_DV = 128
GLA_RANK = 16
GLA_GATE_NORM = 16.0
GLA_CHUNK = 64
SB_HEADS = 4
SB_DH = 64
SB_BLOCK = 128
MEM_HEADS = 4
MEM_DH = 64
GLA_QK_W = GLA_HEADS * GLA_DK
GLA_V_W = GLA_HEADS * GLA_DV
SB_W = SB_HEADS * SB_DH
MEM_W = MEM_HEADS * MEM_DH
ALPHA = (2.0 * DEPTH) ** 0.25
LN_EPS = 1e-5
RMS_EPS = 1e-6

LANES = 128
GLA_COLS = 2 * GLA_QK_W + 2 * GLA_V_W
SB_COLS = 3 * SB_W
C3_COLS = SB_W + 2 * MEM_W
GA_COLS = LANES
PROJ_COLS = GLA_COLS + SB_COLS + C3_COLS + GA_COLS
VMEM_LIMIT = 56 * 1024 * 1024
SB_SKIP_LOG = -106.0
SB_HALF = 64
SB_LEAD_KEYS = 2 * SB_BLOCK
SB_QTILE = 2048
GLA_TILE = 2048
ROW_TILE = 512
OUT_TILE = 1024
PROJ_TILE = 2048
LOG2E = 1.4426950408889634


def _dot(a, b):
    return jnp.dot(a, b, preferred_element_type=F32)


def _dot_nt(a, b):
    return lax.dot_general(a, b, (((1,), (1,)), ((), ())), preferred_element_type=F32)


def _dot_tn(a, b):
    return lax.dot_general(a, b, (((0,), (0,)), ((), ())), preferred_element_type=F32)


def _split_bf16(x, n):
    parts = []
    for _ in range(n - 1):
        h = x.astype(BF16)
        parts.append(h)
        x = x - h.astype(F32)
    parts.append(x.astype(BF16))
    return parts


def _exp_neg(x):
    return jnp.exp2(x * -LOG2E)


def _log_sigmoid(z):
    return jnp.minimum(z, 0.0) - jnp.log(1.0 + _exp_neg(jnp.abs(z)))


def _silu(g):
    return g / (1.0 + _exp_neg(g))


def _emit_pipelined(stages, items):
    for step in range(len(items) + len(stages) - 1):
        for depth, stage in enumerate(stages):
            j = step - depth
            if 0 <= j < len(items):
                stage(items[j])


def _proj_kernel(x_ref, w_ref, gla_ref, sb_ref, c3_ref, ga_ref):
    xb = x_ref[...].astype(BF16)
    w_ref = w_ref.at[0]
    off = 0
    for ref, width in ((gla_ref, GLA_COLS), (sb_ref, SB_COLS), (c3_ref, C3_COLS), (ga_ref, GA_COLS)):
        step = 512 if width % 512 == 0 else width if width < 512 else 256
        for c in range(0, width, step):
            ref[:, c:c + step] = _dot(xb, w_ref[:, off + c:off + c + step]).astype(BF16)
        off += width


def _proj(x2, w, layer, tm=PROJ_TILE):
    n = x2.shape[0]
    outs = (GLA_COLS, SB_COLS, C3_COLS, GA_COLS)
    return pl.pallas_call(
        _proj_kernel,
        grid=(n // tm,),
        in_specs=[pl.BlockSpec((tm, D_MODEL), lambda i: (i, 0)),
                  pl.BlockSpec((1, D_MODEL, PROJ_COLS), lambda i: (layer, 0, 0), pipeline_mode=pl.Buffered(1))],
        out_specs=[pl.BlockSpec((tm, c), lambda i: (i, 0)) for c in outs],
        out_shape=[jax.ShapeDtypeStruct((n, c), BF16) for c in outs],
        compiler_params=pltpu.CompilerParams(dimension_semantics=("arbitrary",),
                                             vmem_limit_bytes=VMEM_LIMIT),
        name="proj",
    )(x2, w)


def _memkv_kernel(m_ref, w_ref, o_ref):
    o_ref[0] = _dot(m_ref[...].astype(BF16), w_ref[0]).astype(BF16)


def _memkv(mem2, w_mkv):
    n = mem2.shape[0]
    return pl.pallas_call(
        _memkv_kernel,
        grid=(DEPTH,),
        in_specs=[pl.BlockSpec((n, D_MODEL), lambda l: (0, 0)),
                  pl.BlockSpec((1, D_MODEL, 2 * MEM_W), lambda l: (l, 0, 0))],
        out_specs=pl.BlockSpec((1, n, 2 * MEM_W), lambda l: (l, 0, 0)),
        out_shape=jax.ShapeDtypeStruct((DEPTH, n, 2 * MEM_W), BF16),
        compiler_params=pltpu.CompilerParams(dimension_semantics=("arbitrary",),
                                             vmem_limit_bytes=VMEM_LIMIT),
        name="memkv",
    )(mem2, w_mkv)


def _gla_kernel(gla_ref, ga_ref, wa2_ref, ba_ref, nw_ref, out_ref, st_ref, *, tt):
    c_sz = GLA_CHUNK

    @pl.when(pl.program_id(1) == 0)
    def _():
        st_ref[...] = jnp.zeros_like(st_ref)

    row = lax.broadcasted_iota(jnp.int32, (c_sz, c_sz), 0)
    col = lax.broadcasted_iota(jnp.int32, (c_sz, c_sz), 1)
    causal = col <= row
    r2 = lax.broadcasted_iota(jnp.int32, (c_sz, 2 * c_sz), 0)
    c2 = lax.broadcasted_iota(jnp.int32, (c_sz, 2 * c_sz), 1) & (c_sz - 1)
    tri_incl2 = jnp.where(c2 <= r2, 1.0, 0.0).astype(BF16)
    low_c = lax.broadcasted_iota(jnp.int32, (c_sz, LANES), 1) < GLA_DK
    chunks = range(tt // c_sz)
    pairs = range(GLA_HEADS // 2)
    rows = [slice(c * c_sz, (c + 1) * c_sz) for c in chunks]
    lanes = [slice(p * LANES, (p + 1) * LANES) for p in pairs]
    v_col = 2 * GLA_QK_W
    g_col = v_col + GLA_V_W

    zz = _dot(ga_ref[0], wa2_ref[0]) + ba_ref[0]
    log_a = _log_sigmoid(zz) * (1.0 / GLA_GATE_NORM)
    hi, lo = _split_bf16(log_a, 2)
    g_cum, kd, decay, q2, a2, upd, inter2 = {}, {}, {}, {}, {}, {}, {}
    state = {p: st_ref[p] for p in pairs}
    norm_w = nw_ref[0] * GLA_DV ** 0.5

    def v_head(c, h):
        return gla_ref[0, rows[c], v_col + h * GLA_DV:v_col + (h + 1) * GLA_DV]

    def cumsum(c):
        g_cum[c] = _dot(tri_incl2, jnp.concatenate([hi[rows[c]], lo[rows[c]]], axis=0))

    def scale(c):
        g_last = g_cum[c][c_sz - 1:c_sz, :]
        q = gla_ref[0, rows[c], 0:GLA_QK_W].astype(F32)
        k = gla_ref[0, rows[c], GLA_QK_W:2 * GLA_QK_W].astype(F32)
        decay[c] = jnp.exp(g_last)
        qg = (q * jnp.exp(g_cum[c])).astype(BF16)
        kgf = k * _exp_neg(g_cum[c])
        kg = kgf.astype(BF16)
        kd[c] = (kgf * decay[c]).astype(BF16)
        for p in pairs:
            qgp = qg[:, lanes[p]]
            q2[c, p] = jnp.concatenate([jnp.where(low_c, qgp, jnp.zeros_like(qgp)),
                                        jnp.where(low_c, jnp.zeros_like(qgp), qgp)], axis=0)
            a2[c, p] = _dot_nt(q2[c, p], kg[:, lanes[p]])

    def update(c):
        for p in pairs:
            kdp = kd[c][:, lanes[p]]
            zero = jnp.zeros_like(kdp)
            upd[c, p] = _dot_tn(jnp.concatenate([v_head(c, 2 * p), v_head(c, 2 * p + 1)], axis=0),
                                jnp.concatenate([jnp.where(low_c, kdp, zero), jnp.where(low_c, zero, kdp)], axis=0))

    def recur(c):
        for p in pairs:
            inter2[c, p] = _dot_nt(q2[c, p], state[p].astype(BF16))
            state[p] = state[p] * decay[c][:, lanes[p]] + upd[c, p]

    def finish(c):
        for p in pairs:
            for hh in range(2):
                h = 2 * p + hh
                a = jnp.where(causal, a2[c, p][hh * c_sz:(hh + 1) * c_sz], 0.0).astype(BF16)
                o = _dot(a, v_head(c, h)) + inter2[c, p][hh * c_sz:(hh + 1) * c_sz]
                ssq = jnp.sum(o * o, axis=-1, keepdims=True)
                gate = gla_ref[0, rows[c], g_col + h * GLA_DV:g_col + (h + 1) * GLA_DV].astype(F32)
                res = o * lax.rsqrt(ssq + GLA_DV * RMS_EPS) * norm_w * _silu(gate)
                out_ref[0, rows[c], h * GLA_DV:(h + 1) * GLA_DV] = res.astype(BF16)

    _emit_pipelined((cumsum, scale, update, recur, finish), list(chunks))
    for p in pairs:
        st_ref[p] = state[p]


def _gla(gla, ga, wa2, ba, nw, layer, tt=GLA_TILE):
    b, t, _ = gla.shape
    return pl.pallas_call(
        functools.partial(_gla_kernel, tt=tt),
        grid=(b, t // tt),
        in_specs=[pl.BlockSpec((1, tt, GLA_COLS), lambda i, j: (i, j, 0)),
                  pl.BlockSpec((1, tt, GA_COLS), lambda i, j: (i, j, 0)),
                  pl.BlockSpec((1, GA_COLS, GLA_QK_W), lambda i, j: (layer, 0, 0)),
                  pl.BlockSpec((1, 1, GLA_QK_W), lambda i, j: (layer, 0, 0)),
                  pl.BlockSpec((1, 1, GLA_DV), lambda i, j: (layer, 0, 0))],
        out_specs=pl.BlockSpec((1, tt, GLA_V_W), lambda i, j: (i, j, 0)),
        out_shape=jax.ShapeDtypeStruct((b, t, GLA_V_W), BF16),
        scratch_shapes=[pltpu.VMEM((GLA_HEADS // 2, GLA_DV, LANES), F32)],
        compiler_params=pltpu.CompilerParams(dimension_semantics=("arbitrary", "arbitrary"),
                                             vmem_limit_bytes=VMEM_LIMIT),
        name="gla",
    )(gla, ga, wa2, ba, nw)


def _sb_kernel(q_ref, k_ref, v_ref, g_ref, o_ref, oacc_ref, lacc_ref):
    hb, kb = SB_HALF, SB_BLOCK
    pairs = SB_HEADS // 2
    halves = SB_QTILE // hb
    q_base = pl.program_id(1) * SB_QTILE
    row = lax.broadcasted_iota(jnp.int32, (hb, kb), 0)
    lane = lax.broadcasted_iota(jnp.int32, (hb, kb), 1)
    low = lane < SB_DH
    rk = lax.broadcasted_iota(jnp.int32, (kb, 2 * kb), 0)
    ck = lax.broadcasted_iota(jnp.int32, (kb, 2 * kb), 1)
    tri_ones = jnp.where((rk > ck) | (ck >= kb), 1.0, 0.0).astype(BF16)

    def pair_rows(m, p):
        st = m * pairs + p
        return slice(st * 2 * hb, (st + 1) * 2 * hb)

    qneg = {}
    for m in range(halves):
        for p in range(pairs):
            qn = -q_ref[0, m * hb:(m + 1) * hb, p * LANES:(p + 1) * LANES]
            zero = jnp.zeros_like(qn)
            qneg[m, p] = jnp.concatenate([jnp.where(low, qn, zero), jnp.where(low, zero, qn)], axis=0)
    def lead_group(start, valid):
        rows = {m: pl.ds(start[m] if isinstance(start[m], int) else pl.multiple_of(start[m], hb), 2 * kb)
                for m in start}
        r4 = lax.broadcasted_iota(jnp.int32, (2 * kb, 2 * kb), 0)
        c4 = lax.broadcasted_iota(jnp.int32, (2 * kb, 2 * kb), 1)
        later = ((r4 > c4) & ((r4 >= kb) == (c4 >= kb))) | ((r4 < kb) & (c4 >= kb))
        tri_pair = jnp.where(later, 1.0, 0.0).astype(BF16)
        keep1 = jnp.where(valid, 1.0, 0.0).astype(BF16)
        keep = jnp.concatenate([keep1, keep1], axis=0)
        log_beta, lf, sums = {}, {}, {}

        def scores(m):
            for p in range(pairs):
                s = _dot_nt(qneg[m, p], k_ref[0, rows[m], p * LANES:(p + 1) * LANES])
                for g in (1, 0):
                    sg = s[:, g * kb:(g + 1) * kb].astype(BF16)
                    lsn = _log_sigmoid(sg)
                    log_beta[m, p, g] = lsn - sg
                    lf[m, p, g] = lsn * keep if g == 1 else lsn

        def suffix_sums(m):
            lhs = [jnp.concatenate([lf[m, p, 1], lf[m, p, 0]], axis=1) for p in range(pairs)]
            sums[m] = _dot(jnp.concatenate(lhs, axis=0), tri_pair)

        def weights(m):
            for p in range(pairs):
                part = sums[m][p * 2 * hb:(p + 1) * 2 * hb]
                w1 = jnp.exp(part[:, :kb].astype(BF16) + log_beta[m, p, 1]) * keep
                w0 = jnp.exp(part[:, kb:].astype(BF16) + log_beta[m, p, 0])
                total = part[:, kb:kb + 1] + lf[m, p, 0][:, 0:1].astype(F32)
                lacc_ref[pair_rows(m, p)] = jnp.broadcast_to(total, (2 * hb, kb))
                oacc_ref[pair_rows(m, p)] = _dot(jnp.concatenate([w0, w1], axis=1),
                                                 v_ref[0, rows[m], p * LANES:(p + 1) * LANES])

        _emit_pipelined((scores, suffix_sums, weights), list(start))

    def group(start, n, valid):
        order = list(reversed(range(n)))
        rows = {m: pl.ds(start[m] if isinstance(start[m], int) else pl.multiple_of(start[m], hb), n * kb)
                for m in start}
        log_beta, lf, sums = {}, {}, {}
        keep1 = jnp.where(valid, 1.0, 0.0).astype(BF16)
        keep = jnp.concatenate([keep1, keep1], axis=0)
        parts = [(p, g) for g in order for p in range(pairs)]

        def scores(m):
            s = {p: _dot_nt(qneg[m, p], k_ref[0, rows[m], p * LANES:(p + 1) * LANES])
                 for p in range(pairs)}
            for p, g in parts:
                sg = s[p][:, g * kb:(g + 1) * kb].astype(BF16)
                lsn = _log_sigmoid(sg)
                log_beta[m, p, g] = lsn - sg
                lf[m, p, g] = lsn * keep if g == n - 1 else lsn

        def suffix_sums(m):
            sums[m] = _dot(jnp.concatenate([lf[m, p, g] for p, g in parts], axis=0), tri_ones)

        def weights(m):
            acc = {p: lacc_ref[pair_rows(m, p)] for p in range(pairs)}
            ws = {p: [None] * n for p in range(pairs)}
            for idx, (p, g) in enumerate(parts):
                part = sums[m][idx * 2 * hb:(idx + 1) * 2 * hb]
                w = jnp.exp((part[:, :kb] + acc[p]).astype(BF16) + log_beta[m, p, g])
                ws[p][g] = w * keep if g == n - 1 else w
                acc[p] = acc[p] + part[:, kb:]
            for p in range(pairs):
                lacc_ref[pair_rows(m, p)] = acc[p]
                oacc_ref[pair_rows(m, p)] += _dot(jnp.concatenate(ws[p], axis=1),
                                                  v_ref[0, rows[m], p * LANES:(p + 1) * LANES])

        _emit_pipelined((scores, suffix_sums, weights), list(start))

    def sweep(m, bound):
        per_m = 2 * pairs * hb

        def live():
            return jnp.max(lacc_ref[m * per_m:(m + 1) * per_m])

        def cond(c):
            return jnp.logical_and(c[0] > 0, c[1] > SB_SKIP_LOG)

        def body(c):
            first = jnp.maximum(c[0] - kb, 0)
            group({m: first}, 1, lane < c[0] - first)
            return first, live()

        lax.while_loop(cond, body, (bound, live()))

    def lead_and_sweep(base, ms):
        lead = {m: base + (m + 1) * hb - SB_LEAD_KEYS for m in ms}
        lead_group(lead, lane < row + (kb - hb))
        rows_all = slice(ms[0] * 2 * pairs * hb, (ms[-1] + 1) * 2 * pairs * hb)

        @pl.when(jnp.max(lacc_ref[rows_all]) > SB_SKIP_LOG)
        def _():
            for m in ms:
                sweep(m, lead[m])

    short = (SB_LEAD_KEYS - hb) // hb

    @pl.when(q_base > 0)
    def _():
        lead_and_sweep(q_base, list(range(halves)))

    @pl.when(q_base == 0)
    def _():
        for m in range(short):
            oacc_ref[m * 2 * pairs * hb:(m + 1) * 2 * pairs * hb] = jnp.zeros((2 * pairs * hb, LANES), F32)
            lacc_ref[m * 2 * pairs * hb:(m + 1) * 2 * pairs * hb] = jnp.zeros((2 * pairs * hb, LANES), F32)
            first = max(m * hb + hb - kb, 0)
            group({m: first}, 1, lane < row + (m * hb - first))
            sweep(m, first)
        lead_and_sweep(0, list(range(short, halves)))

    for m in range(halves):
        for p in range(pairs):
            both = oacc_ref[pair_rows(m, p)]
            o = jnp.where(low, both[:hb], both[hb:])
            gate = g_ref[0, m * hb:(m + 1) * hb, p * LANES:(p + 1) * LANES].astype(F32)
            o_ref[0, m * hb:(m + 1) * hb, p * LANES:(p + 1) * LANES] = (o * _silu(gate)).astype(BF16)


def _sb(sb, c3):
    b, t, _ = sb.shape
    n_streams = (SB_QTILE // SB_HALF) * SB_HEADS
    return pl.pallas_call(
        _sb_kernel,
        grid=(b, t // SB_QTILE),
        in_specs=[pl.BlockSpec((1, SB_QTILE, SB_W), lambda bi, i: (bi, i, 0)),
                  pl.BlockSpec((1, t, SB_W), lambda bi, i: (bi, 0, 1)),
                  pl.BlockSpec((1, t, SB_W), lambda bi, i: (bi, 0, 2)),
                  pl.BlockSpec((1, SB_QTILE, SB_W), lambda bi, i: (bi, i, 0))],
        out_specs=pl.BlockSpec((1, SB_QTILE, SB_W), lambda bi, i: (bi, i, 0)),
        out_shape=jax.ShapeDtypeStruct((b, t, SB_W), BF16),
        scratch_shapes=[pltpu.VMEM((n_streams * SB_HALF, LANES), F32),
                        pltpu.VMEM((n_streams * SB_HALF, LANES), F32)],
        compiler_params=pltpu.CompilerParams(dimension_semantics=("arbitrary", "arbitrary"),
                                             vmem_limit_bytes=VMEM_LIMIT),
        name="sb",
    )(sb, sb, sb, c3)


def _out_kernel(x_ref, mixg_ref, mixs_ref, mq_ref, mg_ref, kvm_ref, wout_ref, lng_ref, lnb_ref, o_ref, *, tm, sub):
    wout_ref = wout_ref.at[0]
    low = lax.broadcasted_iota(jnp.int32, (sub, LANES), 1) < MEM_DH
    pairs = range(MEM_HEADS // 2)
    s, e, den, y = {}, {}, {}, {}

    def rows(j):
        return slice(j * sub, (j + 1) * sub)

    def scores(j):
        for p in pairs:
            qp = mq_ref[0, rows(j), p * LANES:(p + 1) * LANES]
            km = kvm_ref[0, 0, :, p * LANES:(p + 1) * LANES]
            zero = jnp.zeros_like(qp)
            s[j, p, 0] = _dot_nt(jnp.where(low, qp, zero), km)
            s[j, p, 1] = _dot_nt(jnp.where(low, zero, qp), km)

    def softmax(j):
        for p in pairs:
            for hh in range(2):
                sc = s[j, p, hh]
                ex = jnp.exp(sc - jnp.max(sc, axis=-1, keepdims=True))
                den[j, p, hh] = jnp.sum(ex, axis=-1, keepdims=True)
                e[j, p, hh] = ex.astype(BF16)

    def mix(j):
        mixm = []
        for p in pairs:
            vm = kvm_ref[0, 0, :, MEM_W + p * LANES:MEM_W + (p + 1) * LANES]
            om = jnp.where(low, _dot(e[j, p, 0], vm) / den[j, p, 0], _dot(e[j, p, 1], vm) / den[j, p, 1])
            gate = mg_ref[0, rows(j), p * LANES:(p + 1) * LANES].astype(F32)
            mixm.append((om * _silu(gate)).astype(BF16))
        y[j] = _dot(jnp.concatenate([mixg_ref[0, rows(j), :], mixs_ref[0, rows(j), :]] + mixm, axis=1), wout_ref[...])

    def norm(j):
        r = ALPHA * x_ref[0, rows(j), :] + y[j]
        mu = jnp.mean(r, axis=-1, keepdims=True)
        d = r - mu
        var = jnp.mean(d * d, axis=-1, keepdims=True)
        o_ref[0, rows(j), :] = d * lax.rsqrt(var + LN_EPS) * lng_ref[0] + lnb_ref[0]

    _emit_pipelined((scores, softmax, mix, norm), list(range(tm // sub)))


def _out(x, mixg, mixs, c3, kvm, layer, wout, lng, lnb, tm=OUT_TILE, sub=ROW_TILE):
    b, t, _ = x.shape
    return pl.pallas_call(
        functools.partial(_out_kernel, tm=tm, sub=sub),
        grid=(b, t // tm),
        in_specs=[pl.BlockSpec((1, tm, D_MODEL), lambda i, j: (i, j, 0)),
                  pl.BlockSpec((1, tm, GLA_V_W), lambda i, j: (i, j, 0)),
                  pl.BlockSpec((1, tm, SB_W), lambda i, j: (i, j, 0)),
                  pl.BlockSpec((1, tm, MEM_W), lambda i, j: (i, j, SB_W // MEM_W)),
                  pl.BlockSpec((1, tm, MEM_W), lambda i, j: (i, j, SB_W // MEM_W + 1)),
                  pl.BlockSpec((1, 1, N_MEM, 2 * MEM_W), lambda i, j: (layer, i, 0, 0)),
                  pl.BlockSpec((1, D_MODEL, D_MODEL), lambda i, j: (layer, 0, 0), pipeline_mode=pl.Buffered(1)),
                  pl.BlockSpec((1, 1, D_MODEL), lambda i, j: (layer, 0, 0)),
                  pl.BlockSpec((1, 1, D_MODEL), lambda i, j: (layer, 0, 0))],
        out_specs=pl.BlockSpec((1, tm, D_MODEL), lambda i, j: (i, j, 0)),
        out_shape=jax.ShapeDtypeStruct((b, t, D_MODEL), F32),
        compiler_params=pltpu.CompilerParams(dimension_semantics=("arbitrary", "arbitrary"),
                                             vmem_limit_bytes=VMEM_LIMIT),
        name="out",
    )(x, mixg, mixs, c3, c3, kvm, wout, lng, lnb)


_W_IN_SEGMENTS = (("gq", GLA_QK_W), ("gk", GLA_QK_W), ("gv", GLA_V_W), ("gg", GLA_V_W), ("ga", GLA_RANK),
                  ("sq", SB_W), ("sk", SB_W), ("sv", SB_W), ("sg", SB_W), ("mq", MEM_W), ("mg", MEM_W))
_W_PROJ_ORDER = ("gq", "gk", "gv", "gg", "sq", "sk", "sv", "sg", "mq", "mg", "ga")
_Q_SCALED = {"gq": GLA_DK ** -0.5, "sq": SB_DH ** -0.5, "mq": MEM_DH ** -0.5}
D_IN = sum(n for _, n in _W_IN_SEGMENTS)


def _wprep_kernel(wt_ref, o_ref):
    src, off = {}, 0
    for name, n in _W_IN_SEGMENTS:
        src[name] = (off, n)
        off += n
    dst = 0
    for name in _W_PROJ_ORDER:
        lo, n = src[name]
        rows = max(n, LANES)
        cols = wt_ref[0, lo:lo + rows, :].T
        if name in _Q_SCALED:
            cols = cols * _Q_SCALED[name]
        if rows > n:
            cols = jnp.where(lax.broadcasted_iota(jnp.int32, cols.shape, 1) < n, cols, 0.0)
        o_ref[0, :, dst:dst + rows] = cols.astype(BF16)
        dst += rows
    assert dst == PROJ_COLS


def _regroup_w_in(wt, tk=256):
    depth, _, d = wt.shape
    return pl.pallas_call(
        _wprep_kernel,
        grid=(depth, d // tk),
        in_specs=[pl.BlockSpec((1, D_IN, tk), lambda l, i: (l, 0, i))],
        out_specs=pl.BlockSpec((1, tk, PROJ_COLS), lambda l, i: (l, i, 0)),
        out_shape=jax.ShapeDtypeStruct((depth, d, PROJ_COLS), BF16),
        compiler_params=pltpu.CompilerParams(dimension_semantics=("arbitrary", "arbitrary"),
                                             vmem_limit_bytes=VMEM_LIMIT),
        name="wprep",
    )(wt)


def kernel(x, mem, w_in, w_alpha2, b_alpha, gla_norm_w, w_mem_kv, w_out, ln_g, ln_b):
    b, t, d = x.shape
    assert d == D_MODEL and mem.shape == (b, N_MEM, d) and w_in.shape == (DEPTH, d, D_IN)
    assert t % GLA_TILE == 0 and t % SB_QTILE == 0 and t % OUT_TILE == 0 and (b * t) % PROJ_TILE == 0
    kvm = _memkv(mem.reshape(b * N_MEM, d), w_mem_kv.astype(BF16)).reshape(DEPTH, b, N_MEM, 2 * MEM_W)
    w_proj = _regroup_w_in(jnp.swapaxes(w_in, 1, 2))
    wa2 = jnp.pad(w_alpha2.astype(BF16), ((0, 0), (0, GA_COLS - GLA_RANK), (0, 0)))
    ba = b_alpha.reshape(DEPTH, 1, GLA_QK_W)
    nw = gla_norm_w.reshape(DEPTH, 1, GLA_DV)
    wout = w_out.astype(BF16)
    lng = ln_g.reshape(DEPTH, 1, d)
    lnb = ln_b.reshape(DEPTH, 1, d)
    for l in range(DEPTH):
        gla, sb, c3, ga = _proj(x.reshape(b * t, d), w_proj, l)
        mixg = _gla(gla.reshape(b, t, GLA_COLS), ga.reshape(b, t, GA_COLS), wa2, ba, nw, l)
        c3 = c3.reshape(b, t, C3_COLS)
        mixs = _sb(sb.reshape(b, t, SB_COLS), c3)
        x = _out(x, mixg, mixs, c3, kvm, l, wout, lng, lnb)
    return x
```

```python
import functools

import jax
import jax.numpy as jnp
from jax import lax
from jax.experimental import pallas as pl
from jax.experimental.pallas import tpu as pltpu

F32 = jnp.float32
BF16 = jnp.bfloat16

D_MODEL = 1024
DEPTH = 4
N_MEM = 256
GLA_HEADS = 4
GLA_DK = 64
GLA_DV = 128
GLA_RANK = 16
GLA_GATE_NORM = 16.0
GLA_CHUNK = 64
SB_HEADS = 4
SB_DH = 64
SB_BLOCK = 128
MEM_HEADS = 4
MEM_DH = 64
GLA_QK_W = GLA_HEADS * GLA_DK
GLA_V_W = GLA_HEADS * GLA_DV
SB_W = SB_HEADS * SB_DH
MEM_W = MEM_HEADS * MEM_DH
ALPHA = (2.0 * DEPTH) ** 0.25
LN_EPS = 1e-5
RMS_EPS = 1e-6

LANES = 128
GLA_COLS = 2 * GLA_QK_W + 2 * GLA_V_W
SB_COLS = 3 * SB_W
C3_COLS = SB_W + 2 * MEM_W
GA_COLS = LANES
PROJ_COLS = GLA_COLS + SB_COLS + C3_COLS + GA_COLS
VMEM_LIMIT = 56 * 1024 * 1024
SB_SKIP_LOG = -106.0
SB_HALF = 64
SB_LEAD_KEYS = 2 * SB_BLOCK
SB_QTILE = 2048
SB_LIVE_GROUP = 8
GLA_TILE = 2048
ROW_TILE = 512
OUT_TILE = 1024
PROJ_TILE = 2048
LOG2E = 1.4426950408889634


def _dot(a, b):
    return jnp.dot(a, b, preferred_element_type=F32)


def _dot_nt(a, b):
    return lax.dot_general(a, b, (((1,), (1,)), ((), ())), preferred_element_type=F32)


def _dot_tn(a, b):
    return lax.dot_general(a, b, (((0,), (0,)), ((), ())), preferred_element_type=F32)


def _split_bf16(x, n):
    parts = []
    for _ in range(n - 1):
        h = x.astype(BF16)
        parts.append(h)
        x = x - h.astype(F32)
    parts.append(x.astype(BF16))
    return parts


def _exp_neg(x):
    return jnp.exp2(x * -LOG2E)


def _log_sigmoid(z):
    return jnp.minimum(z, 0.0) - jnp.log(1.0 + _exp_neg(jnp.abs(z)))


def _silu(g):
    return g / (1.0 + _exp_neg(g))


def _emit_pipelined(stages, items):
    for step in range(len(items) + len(stages) - 1):
        for depth, stage in enumerate(stages):
            j = step - depth
            if 0 <= j < len(items):
                stage(items[j])


def _proj_kernel(x_ref, w_ref, gla_ref, sb_ref, c3_ref, ga_ref):
    xb = x_ref[...].astype(BF16)
    w_ref = w_ref.at[0]
    off = 0
    for ref, width in ((gla_ref, GLA_COLS), (sb_ref, SB_COLS), (c3_ref, C3_COLS), (ga_ref, GA_COLS)):
        step = 512 if width % 512 == 0 else width if width < 512 else 256
        for c in range(0, width, step):
            ref[:, c:c + step] = _dot(xb, w_ref[:, off + c:off + c + step]).astype(BF16)
        off += width


def _proj(x2, w, layer, tm=PROJ_TILE):
    n = x2.shape[0]
    outs = (GLA_COLS, SB_COLS, C3_COLS, GA_COLS)
    return pl.pallas_call(
        _proj_kernel,
        grid=(n // tm,),
        in_specs=[pl.BlockSpec((tm, D_MODEL), lambda i: (i, 0)),
                  pl.BlockSpec((1, D_MODEL, PROJ_COLS), lambda i: (layer, 0, 0), pipeline_mode=pl.Buffered(1))],
        out_specs=[pl.BlockSpec((tm, c), lambda i: (i, 0)) for c in outs],
        out_shape=[jax.ShapeDtypeStruct((n, c), BF16) for c in outs],
        compiler_params=pltpu.CompilerParams(dimension_semantics=("arbitrary",),
                                             vmem_limit_bytes=VMEM_LIMIT),
        name="proj",
    )(x2, w)


def _memkv_kernel(m_ref, w_ref, o_ref):
    o_ref[0] = _dot(m_ref[...].astype(BF16), w_ref[0]).astype(BF16)


def _memkv(mem2, w_mkv):
    n = mem2.shape[0]
    return pl.pallas_call(
        _memkv_kernel,
        grid=(DEPTH,),
        in_specs=[pl.BlockSpec((n, D_MODEL), lambda l: (0, 0)),
                  pl.BlockSpec((1, D_MODEL, 2 * MEM_W), lambda l: (l, 0, 0))],
        out_specs=pl.BlockSpec((1, n, 2 * MEM_W), lambda l: (l, 0, 0)),
        out_shape=jax.ShapeDtypeStruct((DEPTH, n, 2 * MEM_W), BF16),
        compiler_params=pltpu.CompilerParams(dimension_semantics=("arbitrary",),
                                             vmem_limit_bytes=VMEM_LIMIT),
        name="memkv",
    )(mem2, w_mkv)


def _gla_kernel(gla_ref, ga_ref, wa2_ref, ba_ref, nw_ref, out_ref, st_ref, *, tt):
    c_sz = GLA_CHUNK

    @pl.when(pl.program_id(1) == 0)
    def _():
        st_ref[...] = jnp.zeros_like(st_ref)

    row = lax.broadcasted_iota(jnp.int32, (c_sz, c_sz), 0)
    col = lax.broadcasted_iota(jnp.int32, (c_sz, c_sz), 1)
    causal = col <= row
    r2 = lax.broadcasted_iota(jnp.int32, (c_sz, 2 * c_sz), 0)
    c2 = lax.broadcasted_iota(jnp.int32, (c_sz, 2 * c_sz), 1) & (c_sz - 1)
    tri_incl2 = jnp.where(c2 <= r2, 1.0, 0.0).astype(BF16)
    low_c = lax.broadcasted_iota(jnp.int32, (c_sz, LANES), 1) < GLA_DK
    chunks = range(tt // c_sz)
    pairs = range(GLA_HEADS // 2)
    rows = [slice(c * c_sz, (c + 1) * c_sz) for c in chunks]
    lanes = [slice(p * LANES, (p + 1) * LANES) for p in pairs]
    v_col = 2 * GLA_QK_W
    g_col = v_col + GLA_V_W

    zz = _dot(ga_ref[0], wa2_ref[0]) + ba_ref[0]
    log_a = _log_sigmoid(zz) * (1.0 / GLA_GATE_NORM)
    hi, lo = _split_bf16(log_a, 2)
    g_cum, kd, decay, q2, a2, upd, inter2 = {}, {}, {}, {}, {}, {}, {}
    state = {p: st_ref[p] for p in pairs}
    norm_w = nw_ref[0] * GLA_DV ** 0.5

    def v_head(c, h):
        return gla_ref[0, rows[c], v_col + h * GLA_DV:v_col + (h + 1) * GLA_DV]

    def cumsum(c):
        g_cum[c] = _dot(tri_incl2, jnp.concatenate([hi[rows[c]], lo[rows[c]]], axis=0))

    def scale(c):
        g_last = g_cum[c][c_sz - 1:c_sz, :]
        q = gla_ref[0, rows[c], 0:GLA_QK_W].astype(F32)
        k = gla_ref[0, rows[c], GLA_QK_W:2 * GLA_QK_W].astype(F32)
        decay[c] = jnp.exp(g_last)
        qg = (q * jnp.exp(g_cum[c])).astype(BF16)
        kgf = k * _exp_neg(g_cum[c])
        kg = kgf.astype(BF16)
        kd[c] = (kgf * decay[c]).astype(BF16)
        for p in pairs:
            qgp = qg[:, lanes[p]]
            q2[c, p] = jnp.concatenate([jnp.where(low_c, qgp, jnp.zeros_like(qgp)),
                                        jnp.where(low_c, jnp.zeros_like(qgp), qgp)], axis=0)
            a2[c, p] = _dot_nt(q2[c, p], kg[:, lanes[p]])

    def update(c):
        for p in pairs:
            kdp = kd[c][:, lanes[p]]
            zero = jnp.zeros_like(kdp)
            upd[c, p] = _dot_tn(jnp.concatenate([v_head(c, 2 * p), v_head(c, 2 * p + 1)], axis=0),
                                jnp.concatenate([jnp.where(low_c, kdp, zero), jnp.where(low_c, zero, kdp)], axis=0))

    def recur(c):
        for p in pairs:
            inter2[c, p] = _dot_nt(q2[c, p], state[p].astype(BF16))
            state[p] = state[p] * decay[c][:, lanes[p]] + upd[c, p]

    def finish(c):
        for p in pairs:
            for hh in range(2):
                h = 2 * p + hh
                a = jnp.where(causal, a2[c, p][hh * c_sz:(hh + 1) * c_sz], 0.0).astype(BF16)
                o = _dot(a, v_head(c, h)) + inter2[c, p][hh * c_sz:(hh + 1) * c_sz]
                ssq = jnp.sum(o * o, axis=-1, keepdims=True)
                gate = gla_ref[0, rows[c], g_col + h * GLA_DV:g_col + (h + 1) * GLA_DV].astype(F32)
                res = o * lax.rsqrt(ssq + GLA_DV * RMS_EPS) * norm_w * _silu(gate)
                out_ref[0, rows[c], h * GLA_DV:(h + 1) * GLA_DV] = res.astype(BF16)

    _emit_pipelined((cumsum, scale, update, recur, finish), list(chunks))
    for p in pairs:
        st_ref[p] = state[p]


def _gla(gla, ga, wa2, ba, nw, layer, tt=GLA_TILE):
    b, t, _ = gla.shape
    return pl.pallas_call(
        functools.partial(_gla_kernel, tt=tt),
        grid=(b, t // tt),
        in_specs=[pl.BlockSpec((1, tt, GLA_COLS), lambda i, j: (i, j, 0)),
                  pl.BlockSpec((1, tt, GA_COLS), lambda i, j: (i, j, 0)),
                  pl.BlockSpec((1, GA_COLS, GLA_QK_W), lambda i, j: (layer, 0, 0)),
                  pl.BlockSpec((1, 1, GLA_QK_W), lambda i, j: (layer, 0, 0)),
                  pl.BlockSpec((1, 1, GLA_DV), lambda i, j: (layer, 0, 0))],
        out_specs=pl.BlockSpec((1, tt, GLA_V_W), lambda i, j: (i, j, 0)),
        out_shape=jax.ShapeDtypeStruct((b, t, GLA_V_W), BF16),
        scratch_shapes=[pltpu.VMEM((GLA_HEADS // 2, GLA_DV, LANES), F32)],
        compiler_params=pltpu.CompilerParams(dimension_semantics=("arbitrary", "arbitrary"),
                                             vmem_limit_bytes=VMEM_LIMIT),
        name="gla",
    )(gla, ga, wa2, ba, nw)


def _sb_kernel(q_ref, k_ref, v_ref, g_ref, o_ref, oacc_ref, lacc_ref):
    hb, kb = SB_HALF, SB_BLOCK
    pairs = SB_HEADS // 2
    halves = SB_QTILE // hb
    q_base = pl.program_id(1) * SB_QTILE
    row = lax.broadcasted_iota(jnp.int32, (hb, kb), 0)
    lane = lax.broadcasted_iota(jnp.int32, (hb, kb), 1)
    low = lane < SB_DH
    rk = lax.broadcasted_iota(jnp.int32, (kb, 2 * kb), 0)
    ck = lax.broadcasted_iota(jnp.int32, (kb, 2 * kb), 1)
    tri_ones = jnp.where((rk > ck) | (ck >= kb), 1.0, 0.0).astype(BF16)

    def pair_rows(m, p):
        st = m * pairs + p
        return slice(st * 2 * hb, (st + 1) * 2 * hb)

    qneg = {}
    for m in range(halves):
        for p in range(pairs):
            qn = -q_ref[0, m * hb:(m + 1) * hb, p * LANES:(p + 1) * LANES]
            zero = jnp.zeros_like(qn)
            qneg[m, p] = jnp.concatenate([jnp.where(low, qn, zero), jnp.where(low, zero, qn)], axis=0)
    def lead_group(start, valid):
        rows = {m: pl.ds(start[m] if isinstance(start[m], int) else pl.multiple_of(start[m], hb), 2 * kb)
                for m in start}
        r4 = lax.broadcasted_iota(jnp.int32, (2 * kb, 2 * kb), 0)
        c4 = lax.broadcasted_iota(jnp.int32, (2 * kb, 2 * kb), 1)
        later = ((r4 > c4) & ((r4 >= kb) == (c4 >= kb))) | ((r4 < kb) & (c4 >= kb))
        tri_pair = jnp.where(later, 1.0, 0.0).astype(BF16)
        keep1 = jnp.where(valid, 1.0, 0.0).astype(BF16)
        keep = jnp.concatenate([keep1, keep1], axis=0)
        log_beta, lf, sums = {}, {}, {}

        def scores(m):
            for p in range(pairs):
                s = _dot_nt(qneg[m, p], k_ref[0, rows[m], p * LANES:(p + 1) * LANES])
                for g in (1, 0):
                    sg = s[:, g * kb:(g + 1) * kb].astype(BF16)
                    lsn = _log_sigmoid(sg)
                    log_beta[m, p, g] = lsn - sg
                    lf[m, p, g] = lsn * keep if g == 1 else lsn

        def suffix_sums(m):
            lhs = [jnp.concatenate([lf[m, p, 1], lf[m, p, 0]], axis=1) for p in range(pairs)]
            sums[m] = _dot(jnp.concatenate(lhs, axis=0), tri_pair)

        def weights(m):
            for p in range(pairs):
                part = sums[m][p * 2 * hb:(p + 1) * 2 * hb]
                w1 = jnp.exp(part[:, :kb].astype(BF16) + log_beta[m, p, 1]) * keep
                w0 = jnp.exp(part[:, kb:].astype(BF16) + log_beta[m, p, 0])
                total = part[:, kb:kb + 1] + lf[m, p, 0][:, 0:1].astype(F32)
                lacc_ref[pair_rows(m, p)] = jnp.broadcast_to(total, (2 * hb, kb))
                oacc_ref[pair_rows(m, p)] = _dot(jnp.concatenate([w0, w1], axis=1),
                                                 v_ref[0, rows[m], p * LANES:(p + 1) * LANES])

        _emit_pipelined((scores, suffix_sums, weights), list(start))

    def group(start, n, valid):
        order = list(reversed(range(n)))
        rows = {m: pl.ds(start[m] if isinstance(start[m], int) else pl.multiple_of(start[m], hb), n * kb)
                for m in start}
        log_beta, lf, sums = {}, {}, {}
        keep1 = jnp.where(valid, 1.0, 0.0).astype(BF16)
        keep = jnp.concatenate([keep1, keep1], axis=0)
        parts = [(p, g) for g in order for p in range(pairs)]

        def scores(m):
            s = {p: _dot_nt(qneg[m, p], k_ref[0, rows[m], p * LANES:(p + 1) * LANES])
                 for p in range(pairs)}
            for p, g in parts:
                sg = s[p][:, g * kb:(g + 1) * kb].astype(BF16)
                lsn = _log_sigmoid(sg)
                log_beta[m, p, g] = lsn - sg
                lf[m, p, g] = lsn * keep if g == n - 1 else lsn

        def suffix_sums(m):
            sums[m] = _dot(jnp.concatenate([lf[m, p, g] for p, g in parts], axis=0), tri_ones)

        def weights(m):
            acc = {p: lacc_ref[pair_rows(m, p)] for p in range(pairs)}
            ws = {p: [None] * n for p in range(pairs)}
            for idx, (p, g) in enumerate(parts):
                part = sums[m][idx * 2 * hb:(idx + 1) * 2 * hb]
                w = jnp.exp((part[:, :kb] + acc[p]).astype(BF16) + log_beta[m, p, g])
                ws[p][g] = w * keep if g == n - 1 else w
                acc[p] = acc[p] + part[:, kb:]
            for p in range(pairs):
                lacc_ref[pair_rows(m, p)] = acc[p]
                oacc_ref[pair_rows(m, p)] += _dot(jnp.concatenate(ws[p], axis=1),
                                                  v_ref[0, rows[m], p * LANES:(p + 1) * LANES])

        _emit_pipelined((scores, suffix_sums, weights), list(start))

    def sweep(m, bound):
        per_m = 2 * pairs * hb

        def live():
            return jnp.max(lacc_ref[m * per_m:(m + 1) * per_m])

        def cond(c):
            return jnp.logical_and(c[0] > 0, c[1] > SB_SKIP_LOG)

        def body(c):
            first = jnp.maximum(c[0] - kb, 0)
            group({m: first}, 1, lane < c[0] - first)
            return first, live()

        lax.while_loop(cond, body, (bound, live()))

    def lead_and_sweep(base, ms):
        lead = {m: base + (m + 1) * hb - SB_LEAD_KEYS for m in ms}
        lead_group(lead, lane < row + (kb - hb))
        def any_live(group):
            rows_of = slice(group[0] * 2 * pairs * hb, (group[-1] + 1) * 2 * pairs * hb)
            return jnp.max(lacc_ref[rows_of]) > SB_SKIP_LOG

        @pl.when(any_live(ms))
        def _():
            for g0 in range(0, len(ms), SB_LIVE_GROUP):
                group = ms[g0:g0 + SB_LIVE_GROUP]

                @pl.when(any_live(group))
                def _(group=group):
                    for m in group:
                        sweep(m, lead[m])

    short = (SB_LEAD_KEYS - hb) // hb

    @pl.when(q_base > 0)
    def _():
        lead_and_sweep(q_base, list(range(halves)))

    @pl.when(q_base == 0)
    def _():
        for m in range(short):
            oacc_ref[m * 2 * pairs * hb:(m + 1) * 2 * pairs * hb] = jnp.zeros((2 * pairs * hb, LANES), F32)
            lacc_ref[m * 2 * pairs * hb:(m + 1) * 2 * pairs * hb] = jnp.zeros((2 * pairs * hb, LANES), F32)
            first = max(m * hb + hb - kb, 0)
            group({m: first}, 1, lane < row + (m * hb - first))
            sweep(m, first)
        lead_and_sweep(0, list(range(short, halves)))

    for m in range(halves):
        for p in range(pairs):
            both = oacc_ref[pair_rows(m, p)]
            o = jnp.where(low, both[:hb], both[hb:])
            gate = g_ref[0, m * hb:(m + 1) * hb, p * LANES:(p + 1) * LANES].astype(F32)
            o_ref[0, m * hb:(m + 1) * hb, p * LANES:(p + 1) * LANES] = (o * _silu(gate)).astype(BF16)


def _sb(sb, c3):
    b, t, _ = sb.shape
    n_streams = (SB_QTILE // SB_HALF) * SB_HEADS
    return pl.pallas_call(
        _sb_kernel,
        grid=(b, t // SB_QTILE),
        in_specs=[pl.BlockSpec((1, SB_QTILE, SB_W), lambda bi, i: (bi, i, 0)),
                  pl.BlockSpec((1, t, SB_W), lambda bi, i: (bi, 0, 1)),
                  pl.BlockSpec((1, t, SB_W), lambda bi, i: (bi, 0, 2)),
                  pl.BlockSpec((1, SB_QTILE, SB_W), lambda bi, i: (bi, i, 0))],
        out_specs=pl.BlockSpec((1, SB_QTILE, SB_W), lambda bi, i: (bi, i, 0)),
        out_shape=jax.ShapeDtypeStruct((b, t, SB_W), BF16),
        scratch_shapes=[pltpu.VMEM((n_streams * SB_HALF, LANES), F32),
                        pltpu.VMEM((n_streams * SB_HALF, LANES), F32)],
        compiler_params=pltpu.CompilerParams(dimension_semantics=("arbitrary", "arbitrary"),
                                             vmem_limit_bytes=VMEM_LIMIT),
        name="sb",
    )(sb, sb, sb, c3)


def _out_kernel(x_ref, mixg_ref, mixs_ref, mq_ref, mg_ref, kvm_ref, wout_ref, lng_ref, lnb_ref, o_ref, *, tm, sub):
    wout_ref = wout_ref.at[0]
    low = lax.broadcasted_iota(jnp.int32, (sub, LANES), 1) < MEM_DH
    pairs = range(MEM_HEADS // 2)
    s, e, den, y = {}, {}, {}, {}

    def rows(j):
        return slice(j * sub, (j + 1) * sub)

    def scores(j):
        for p in pairs:
            qp = mq_ref[0, rows(j), p * LANES:(p + 1) * LANES]
            km = kvm_ref[0, 0, :, p * LANES:(p + 1) * LANES]
            zero = jnp.zeros_like(qp)
            s[j, p, 0] = _dot_nt(jnp.where(low, qp, zero), km)
            s[j, p, 1] = _dot_nt(jnp.where(low, zero, qp), km)

    def softmax(j):
        for p in pairs:
            for hh in range(2):
                sc = s[j, p, hh]
                ex = jnp.exp(sc - jnp.max(sc, axis=-1, keepdims=True))
                den[j, p, hh] = jnp.sum(ex, axis=-1, keepdims=True)
                e[j, p, hh] = ex.astype(BF16)

    def mix(j):
        mixm = []
        for p in pairs:
            vm = kvm_ref[0, 0, :, MEM_W + p * LANES:MEM_W + (p + 1) * LANES]
            om = jnp.where(low, _dot(e[j, p, 0], vm) / den[j, p, 0], _dot(e[j, p, 1], vm) / den[j, p, 1])
            gate = mg_ref[0, rows(j), p * LANES:(p + 1) * LANES].astype(F32)
            mixm.append((om * _silu(gate)).astype(BF16))
        y[j] = _dot(jnp.concatenate([mixg_ref[0, rows(j), :], mixs_ref[0, rows(j), :]] + mixm, axis=1), wout_ref[...])

    def norm(j):
        r = ALPHA * x_ref[0, rows(j), :] + y[j]
        mu = jnp.mean(r, axis=-1, keepdims=True)
        d = r - mu
        var = jnp.mean(d * d, axis=-1, keepdims=True)
        o_ref[0, rows(j), :] = d * lax.rsqrt(var + LN_EPS) * lng_ref[0] + lnb_ref[0]

    _emit_pipelined((scores, softmax, mix, norm), list(range(tm // sub)))


def _out(x, mixg, mixs, c3, kvm, layer, wout, lng, lnb, tm=OUT_TILE, sub=ROW_TILE):
    b, t, _ = x.shape
    return pl.pallas_call(
        functools.partial(_out_kernel, tm=tm, sub=sub),
        grid=(b, t // tm),
        in_specs=[pl.BlockSpec((1, tm, D_MODEL), lambda i, j: (i, j, 0)),
                  pl.BlockSpec((1, tm, GLA_V_W), lambda i, j: (i, j, 0)),
                  pl.BlockSpec((1, tm, SB_W), lambda i, j: (i, j, 0)),
                  pl.BlockSpec((1, tm, MEM_W), lambda i, j: (i, j, SB_W // MEM_W)),
                  pl.BlockSpec((1, tm, MEM_W), lambda i, j: (i, j, SB_W // MEM_W + 1)),
                  pl.BlockSpec((1, 1, N_MEM, 2 * MEM_W), lambda i, j: (layer, i, 0, 0)),
                  pl.BlockSpec((1, D_MODEL, D_MODEL), lambda i, j: (layer, 0, 0), pipeline_mode=pl.Buffered(1)),
                  pl.BlockSpec((1, 1, D_MODEL), lambda i, j: (layer, 0, 0)),
                  pl.BlockSpec((1, 1, D_MODEL), lambda i, j: (layer, 0, 0))],
        out_specs=pl.BlockSpec((1, tm, D_MODEL), lambda i, j: (i, j, 0)),
        out_shape=jax.ShapeDtypeStruct((b, t, D_MODEL), F32),
        compiler_params=pltpu.CompilerParams(dimension_semantics=("arbitrary", "arbitrary"),
                                             vmem_limit_bytes=VMEM_LIMIT),
        name="out",
    )(x, mixg, mixs, c3, c3, kvm, wout, lng, lnb)


_W_IN_SEGMENTS = (("gq", GLA_QK_W), ("gk", GLA_QK_W), ("gv", GLA_V_W), ("gg", GLA_V_W), ("ga", GLA_RANK),
                  ("sq", SB_W), ("sk", SB_W), ("sv", SB_W), ("sg", SB_W), ("mq", MEM_W), ("mg", MEM_W))
_W_PROJ_ORDER = ("gq", "gk", "gv", "gg", "sq", "sk", "sv", "sg", "mq", "mg", "ga")
_Q_SCALED = {"gq": GLA_DK ** -0.5, "sq": SB_DH ** -0.5, "mq": MEM_DH ** -0.5}
D_IN = sum(n for _, n in _W_IN_SEGMENTS)


def _wprep_kernel(wt_ref, o_ref):
    src, off = {}, 0
    for name, n in _W_IN_SEGMENTS:
        src[name] = (off, n)
        off += n
    dst = 0
    for name in _W_PROJ_ORDER:
        lo, n = src[name]
        rows = max(n, LANES)
        cols = wt_ref[0, lo:lo + rows, :].T
        if name in _Q_SCALED:
            cols = cols * _Q_SCALED[name]
        if rows > n:
            cols = jnp.where(lax.broadcasted_iota(jnp.int32, cols.shape, 1) < n, cols, 0.0)
        o_ref[0, :, dst:dst + rows] = cols.astype(BF16)
        dst += rows
    assert dst == PROJ_COLS


def _regroup_w_in(wt, tk=256):
    depth, _, d = wt.shape
    return pl.pallas_call(
        _wprep_kernel,
        grid=(depth, d // tk),
        in_specs=[pl.BlockSpec((1, D_IN, tk), lambda l, i: (l, 0, i))],
        out_specs=pl.BlockSpec((1, tk, PROJ_COLS), lambda l, i: (l, i, 0)),
        out_shape=jax.ShapeDtypeStruct((depth, d, PROJ_COLS), BF16),
        compiler_params=pltpu.CompilerParams(dimension_semantics=("arbitrary", "arbitrary"),
                                             vmem_limit_bytes=VMEM_LIMIT),
        name="wprep",
    )(wt)


def kernel(x, mem, w_in, w_alpha2, b_alpha, gla_norm_w, w_mem_kv, w_out, ln_g, ln_b):
    b, t, d = x.shape
    assert d == D_MODEL and mem.shape == (b, N_MEM, d) and w_in.shape == (DEPTH, d, D_IN)
    assert t % GLA_TILE == 0 and t % SB_QTILE == 0 and t % OUT_TILE == 0 and (b * t) % PROJ_TILE == 0
    kvm = _memkv(mem.reshape(b * N_MEM, d), w_mem_kv.astype(BF16)).reshape(DEPTH, b, N_MEM, 2 * MEM_W)
    w_proj = _regroup_w_in(jnp.swapaxes(w_in, 1, 2))
    wa2 = jnp.pad(w_alpha2.astype(BF16), ((0, 0), (0, GA_COLS - GLA_RANK), (0, 0)))
    ba = b_alpha.reshape(DEPTH, 1, GLA_QK_W)
    nw = gla_norm_w.reshape(DEPTH, 1, GLA_DV)
    wout = w_out.astype(BF16)
    lng = ln_g.reshape(DEPTH, 1, d)
    lnb = ln_b.reshape(DEPTH, 1, d)
    for l in range(DEPTH):
        gla, sb, c3, ga = _proj(x.reshape(b * t, d), w_proj, l)
        mixg = _gla(gla.reshape(b, t, GLA_COLS), ga.reshape(b, t, GA_COLS), wa2, ba, nw, l)
        c3 = c3.reshape(b, t, C3_COLS)
        mixs = _sb(sb.reshape(b, t, SB_COLS), c3)
        x = _out(x, mixg, mixs, c3, kvm, l, wout, lng, lnb)
    return x
```
